```python
import math
import jax, jax.numpy as jnp
from jax import lax
import numpy as np

D_MODEL = 2048
BATCH = 2
SEQ = 4096
DEPTH = 1

HEAD_DIM = 128
N_HEADS = D_MODEL // HEAD_DIM
N_MOBA_HEADS = N_HEADS // 2
N_FOX_HEADS = N_HEADS - N_MOBA_HEADS
W_MOBA = N_MOBA_HEADS * HEAD_DIM
W_FOX = N_FOX_HEADS * HEAD_DIM
IN_COLS = 3 * W_MOBA + 3 * W_FOX + N_FOX_HEADS
MOBA_BLOCK = 256
MOBA_TOPK = 3
MOBA_Q_CHUNK = 32
FOX_Q_BLOCK = 128
FOX_GATE_BIAS = 2.0
ROPE_THETA = 10000.0
N_MEM = 256
N_XATTN_HEADS = 4
XATTN_DIM = N_XATTN_HEADS * HEAD_DIM
D_FF = 5632
CONV_WIDTH = 3
RMS_EPS = 1e-6
NEG = -1e30
SCALE = 1.0 / math.sqrt(HEAD_DIM)

kernel_name = "hymba_moba_fox_convffn_xattn"


def rmsnorm(x, g):
    xf = x.astype(jnp.float32)
    y = xf * lax.rsqrt(jnp.mean(xf * xf, axis=-1, keepdims=True) + RMS_EPS)
    return (y * g.astype(jnp.float32)).astype(x.dtype)


def split_heads(x, n_heads):
    b, t, _ = x.shape
    return x.reshape(b, t, n_heads, HEAD_DIM).transpose(0, 2, 1, 3)


def merge_heads(x):
    b, h, t, d = x.shape
    return x.transpose(0, 2, 1, 3).reshape(b, t, h * d)


def rope(x, pos):
    half = HEAD_DIM // 2
    inv = ROPE_THETA ** (-jnp.arange(half, dtype=jnp.float32) / half)
    ang = pos.astype(jnp.float32)[:, None] * inv[None, :]
    cos, sin = jnp.cos(ang), jnp.sin(ang)
    xf = x.astype(jnp.float32)
    x1, x2 = xf[..., :half], xf[..., half:]
    out = jnp.concatenate([x1 * cos - x2 * sin, x2 * cos + x1 * sin], axis=-1)
    return out.astype(x.dtype)


def moba_attention(q, k, v):
    b, h, t, hd = q.shape
    L = MOBA_BLOCK
    nb = -(-t // L)
    tp = nb * L
    pad = ((0, 0), (0, 0), (0, tp - t), (0, 0))
    q, k, v = jnp.pad(q, pad), jnp.pad(k, pad), jnp.pad(v, pad)
    kb = k.reshape(b, h, nb, L, hd)
    vb = v.reshape(b, h, nb, L, hd)
    kmean = jnp.mean(kb.astype(jnp.float32), axis=3)
    gate = jnp.einsum('bhtd,bhnd->bhtn', q.astype(jnp.float32), kmean)
    qblk = jnp.arange(tp) // L
    past = jnp.arange(nb)[None, :] < qblk[:, None]
    gate = jnp.where(past[None, None], gate, NEG)
    k_sel = min(MOBA_TOPK, nb)
    _, top_idx = lax.top_k(gate, k_sel)
    sel_ok = top_idx < qblk[None, None, :, None]
    C = MOBA_Q_CHUNK
    nc = tp // C
    qc = jnp.moveaxis(q.reshape(b, h, nc, C, hd), 2, 0)
    idxc = jnp.moveaxis(top_idx.reshape(b, h, nc, C, k_sel), 2, 0)
    okc = jnp.moveaxis(sel_ok.reshape(b, h, nc, C, k_sel), 2, 0)
    bi = jnp.arange(b)[:, None, None, None]
    hi = jnp.arange(h)[None, :, None, None]

    def chunk(args):
        ci, qi, idx, ok = args
        start = ci * C
        blk = start // L
        ks = kb[bi, hi, idx]
        vs = vb[bi, hi, idx]
        s_sel = jnp.einsum('bhcd,bhcnld->bhcnl', qi, ks).astype(jnp.float32) * SCALE
        s_sel = jnp.where(ok[..., None], s_sel, NEG).reshape(b, h, C, k_sel * L)
        k_own = lax.dynamic_index_in_dim(kb, blk, axis=2, keepdims=False)
        v_own = lax.dynamic_index_in_dim(vb, blk, axis=2, keepdims=False)
        s_own = jnp.einsum('bhcd,bhld->bhcl', qi, k_own).astype(jnp.float32) * SCALE
        qpos = start + jnp.arange(C)
        kpos = blk * L + jnp.arange(L)
        s_own = jnp.where((kpos[None, :] <= qpos[:, None])[None, None], s_own, NEG)
        p = jax.nn.softmax(jnp.concatenate([s_sel, s_own], axis=-1), axis=-1)
        p_sel = p[..., :k_sel * L].reshape(b, h, C, k_sel, L).astype(v.dtype)
        p_own = p[..., k_sel * L:].astype(v.dtype)
        return (jnp.einsum('bhcnl,bhcnld->bhcd', p_sel, vs)
                + jnp.einsum('bhcl,bhld->bhcd', p_own, v_own))

    out = lax.map(chunk, (jnp.arange(nc), qc, idxc, okc))
    out = jnp.moveaxis(out, 0, 2).reshape(b, h, tp, hd)
    return out[:, :, :t]


def forgetting_attention(q, k, v, log_f):
    b, h, t, hd = q.shape
    c = jnp.cumsum(log_f, axis=-1)
    QB = FOX_Q_BLOCK
    nq = t // QB
    qb = jnp.moveaxis(q.reshape(b, h, nq, QB, hd), 2, 0)
    cb = jnp.moveaxis(c.reshape(b, h, nq, QB), 2, 0)
    kpos = jnp.arange(t)

    def block(args):
        i, qi, ci = args
        s = jnp.einsum('bhqd,bhkd->bhqk', qi, k).astype(jnp.float32) * SCALE
        s = s + ci[..., None] - c[:, :, None, :]
        qpos = i * QB + jnp.arange(QB)
        s = jnp.where((kpos[None, :] <= qpos[:, None])[None, None], s, NEG)
        p = jax.nn.softmax(s, axis=-1).astype(v.dtype)
        return jnp.einsum('bhqk,bhkd->bhqd', p, v)

    out = lax.map(block, (jnp.arange(nq), qb, cb))
    return jnp.moveaxis(out, 0, 2).reshape(b, h, t, hd)


def cross_attention(xn, memn, w_cq, w_ckv, w_co):
    q = split_heads(xn @ w_cq, N_XATTN_HEADS)
    k, v = jnp.split(memn @ w_ckv, 2, axis=-1)
    k = split_heads(k, N_XATTN_HEADS)
    v = split_heads(v, N_XATTN_HEADS)
    s = jnp.einsum('bhtd,bhmd->bhtm', q, k).astype(jnp.float32) * SCALE
    p = jax.nn.softmax(s, axis=-1).astype(v.dtype)
    o = jnp.einsum('bhtm,bhmd->bhtd', p, v)
    return merge_heads(o) @ w_co


def conv_ffn(xn, w_up, conv_w, conv_b, w_down):
    hdn = xn @ w_up
    ch = hdn.shape[-1]
    hdn = lax.conv_general_dilated(
        hdn, conv_w[:, None, :].astype(hdn.dtype), window_strides=(1,),
        padding=[(CONV_WIDTH - 1, 0)], dimension_numbers=('NWC', 'WIO', 'NWC'),
        feature_group_count=ch) + conv_b
    g, u = jnp.split(hdn, 2, axis=-1)
    return (jax.nn.silu(g) * u) @ w_down


def setup_inputs(seed: int = 0) -> dict:
    key = jax.random.key(seed)
    ks = jax.random.split(key, 20)
    f32 = jnp.float32

    def nrm(k, shape, fan_in):
        return jax.random.normal(k, shape, f32) * (fan_in ** -0.5)

    def gain(k, shape):
        return 1.0 + 0.02 * jax.random.normal(k, shape, f32)

    return {
        "x": jax.random.normal(ks[0], (BATCH, SEQ, D_MODEL), f32),
        "mem": jax.random.normal(ks[1], (BATCH, N_MEM, D_MODEL), f32),
        "attn_norm_g": gain(ks[2], (DEPTH, D_MODEL)),
        "w_in": nrm(ks[3], (DEPTH, D_MODEL, IN_COLS), D_MODEL),
        "b_f": FOX_GATE_BIAS + 0.1 * jax.random.normal(ks[4], (DEPTH, N_FOX_HEADS), f32),
        "w_o": nrm(ks[5], (DEPTH, W_MOBA + W_FOX, D_MODEL), W_MOBA + W_FOX),
        "xattn_norm_g": gain(ks[6], (DEPTH, D_MODEL)),
        "mem_norm_g": gain(ks[7], (DEPTH, D_MODEL)),
        "w_cq": nrm(ks[8], (DEPTH, D_MODEL, XATTN_DIM), D_MODEL),
        "w_ckv": nrm(ks[9], (DEPTH, D_MODEL, 2 * XATTN_DIM), D_MODEL),
        "w_co": nrm(ks[10], (DEPTH, XATTN_DIM, D_MODEL), XATTN_DIM),
        "ffn_norm_g": gain(ks[11], (DEPTH, D_MODEL)),
        "w_up": nrm(ks[12], (DEPTH, D_MODEL, 2 * D_FF), D_MODEL),
        "conv_w": nrm(ks[13], (DEPTH, CONV_WIDTH, 2 * D_FF), CONV_WIDTH),
        "conv_b": 0.02 * jax.random.normal(ks[14], (DEPTH, 2 * D_FF), f32),
        "w_down": nrm(ks[15], (DEPTH, D_FF, D_MODEL), D_FF),
        "final_norm_g": gain(ks[16], (D_MODEL,)),
    }


def reference(x, mem, attn_norm_g, w_in, b_f, w_o, xattn_norm_g, mem_norm_g,
              w_cq, w_ckv, w_co, ffn_norm_g, w_up, conv_w, conv_b, w_down,
              final_norm_g):
    t = x.shape[1]
    pos = jnp.arange(t)
    splits = [W_MOBA, 2 * W_MOBA, 3 * W_MOBA,
              3 * W_MOBA + W_FOX, 3 * W_MOBA + 2 * W_FOX, 3 * W_MOBA + 3 * W_FOX]
    h = x
    for l in range(DEPTH):
        xn = rmsnorm(h, attn_norm_g[l])
        proj = xn @ w_in[l]
        qa, ka, va, qf, kf, vf, zf = jnp.split(proj, splits, axis=-1)
        qa = rope(split_heads(qa, N_MOBA_HEADS), pos)
        ka = rope(split_heads(ka, N_MOBA_HEADS), pos)
        va = split_heads(va, N_MOBA_HEADS)
        o_moba = moba_attention(qa, ka, va)
        log_f = jax.nn.log_sigmoid(zf.astype(jnp.float32) + b_f[l].astype(jnp.float32))
        o_fox = forgetting_attention(split_heads(qf, N_FOX_HEADS),
                                     split_heads(kf, N_FOX_HEADS),
                                     split_heads(vf, N_FOX_HEADS),
                                     jnp.transpose(log_f, (0, 2, 1)))
        mixed = jnp.concatenate([merge_heads(o_moba), merge_heads(o_fox)], axis=-1)
        h = h + mixed @ w_o[l]
        h = h + cross_attention(rmsnorm(h, xattn_norm_g[l]), rmsnorm(mem, mem_norm_g[l]),
                                w_cq[l], w_ckv[l], w_co[l])
        h = h + conv_ffn(rmsnorm(h, ffn_norm_g[l]), w_up[l], conv_w[l], conv_b[l], w_down[l])
    return rmsnorm(h, final_norm_g)
```

```python
import functools
import math

import jax
import jax.numpy as jnp
from jax import lax
from jax.experimental import pallas as pl
from jax.experimental.pallas import tpu as pltpu

HEAD_DIM = 128
MOBA_BLOCK = 256
MOBA_TOPK = 3
ROPE_THETA = 10000.0
N_XATTN_HEADS = 4
CONV_WIDTH = 3
RMS_EPS = 1e-6
NEG = -1e30
SCALE = 1.0 / math.sqrt(HEAD_DIM)

LANES = 128
SUBLANES = 8
VMEM_LIMIT_BYTES = 56 * 1024 * 1024

F32 = jnp.float32
BF16 = jnp.bfloat16

_NT = (((1,), (1,)), ((), ()))


def _cparams(*sem):
    return pltpu.CompilerParams(dimension_semantics=sem, vmem_limit_bytes=VMEM_LIMIT_BYTES)


def _rmsnorm(x, g):
    ms = jnp.mean(x * x, axis=-1, keepdims=True)
    return x * lax.rsqrt(ms + RMS_EPS) * g


def _split3(x):
    hi = x.astype(BF16)
    r = x - hi.astype(F32)
    mid = r.astype(BF16)
    lo = (r - mid.astype(F32)).astype(BF16)
    return hi, mid, lo


def _in_proj_kernel(x_ref, g_ref, w_ref, wz_ref, bf_ref, cos_ref, sin_ref,
                    o_ref, lf_ref, xn_sc, *, group_of_tile):
    j = pl.program_id(1)

    @pl.when(j == 0)
    def _():
        xn = _rmsnorm(x_ref[...], g_ref[...]).astype(BF16)
        xn_sc[...] = xn
        z = jnp.dot(xn, wz_ref[...], preferred_element_type=F32) + bf_ref[...]
        lf_ref[...] = jnp.minimum(z, 0.0) - jnp.log(1.0 + jnp.exp(-jnp.abs(z)))

    y = jnp.dot(xn_sc[...], w_ref[...], preferred_element_type=F32)
    n_heads_tile = y.shape[1] // HEAD_DIM

    def rope(scale):
        cos = cos_ref[...]
        sin = sin_ref[...]
        for h in range(n_heads_tile):
            sl = slice(h * HEAD_DIM, (h + 1) * HEAD_DIM)
            yh = y[:, sl]
            r = yh * cos + pltpu.roll(yh, HEAD_DIM // 2, 1) * sin
            if scale != 1.0:
                r = r * scale
            o_ref[:, sl] = r.astype(o_ref.dtype)

    grp = group_of_tile(j)

    @pl.when(grp == 0)
    def _():
        rope(SCALE)

    @pl.when(grp == 1)
    def _():
        rope(1.0)

    @pl.when(grp == 3)
    def _():
        o_ref[...] = (y * SCALE).astype(o_ref.dtype)

    @pl.when((grp == 2) | (grp >= 4))
    def _():
        o_ref[...] = y.astype(o_ref.dtype)


def _in_proj(x2, g, w_main, w_z, b_f, cos, sin, *, seq, group_width):
    n, d = x2.shape
    cols = w_main.shape[1]
    tm = min(1024, seq)
    tn = min(1024, group_width)
    tiles_per_group = group_width // tn
    t_tiles = seq // tm
    kern = functools.partial(_in_proj_kernel, group_of_tile=lambda j: j // tiles_per_group)
    return pl.pallas_call(
        kern,
        grid=(n // tm, cols // tn),
        in_specs=[
            pl.BlockSpec((tm, d), lambda i, j: (i, 0)),
            pl.BlockSpec((1, d), lambda i, j: (0, 0)),
            pl.BlockSpec((d, tn), lambda i, j: (0, j)),
            pl.BlockSpec((d, LANES), lambda i, j: (0, 0)),
            pl.BlockSpec((1, LANES), lambda i, j: (0, 0)),
            pl.BlockSpec((tm, HEAD_DIM), lambda i, j: (i % t_tiles, 0)),
            pl.BlockSpec((tm, HEAD_DIM), lambda i, j: (i % t_tiles, 0)),
        ],
        out_specs=[
            pl.BlockSpec((tm, tn), lambda i, j: (i, j)),
            pl.BlockSpec((tm, LANES), lambda i, j: (i, 0)),
        ],
        out_shape=[
            jax.ShapeDtypeStruct((n, cols), BF16),
            jax.ShapeDtypeStruct((n, LANES), F32),
        ],
        scratch_shapes=[pltpu.VMEM((tm, d), BF16)],
        compiler_params=_cparams("arbitrary", "arbitrary"),
        name="in_proj",
    )(x2, g, w_main, w_z, b_f, cos, sin)


def _fox_gate_kernel(lf_ref, qx_ref, kx_ref, carry_sc, *, n_heads):
    t = pl.program_id(1)

    @pl.when(t == 0)
    def _():
        carry_sc[...] = jnp.zeros_like(carry_sc)

    lf = lf_ref[...]
    tc = lf.shape[0]
    row = lax.broadcasted_iota(jnp.int32, (tc, tc), 0)
    col = lax.broadcasted_iota(jnp.int32, (tc, tc), 1)
    tri = (col <= row).astype(BF16)
    c = carry_sc[0:1, :]
    for part in _split3(lf):
        c = c + jnp.dot(tri, part, preferred_element_type=F32)
    carry_sc[0:1, :] = c[tc - 1:tc, :]

    lane = lax.broadcasted_iota(jnp.int32, (tc, LANES), 1)
    one = jnp.ones((tc, LANES), F32)
    zero = jnp.zeros((tc, LANES), F32)
    for h in range(n_heads):
        ch = jnp.sum(jnp.where(lane == h, c, 0.0), axis=1, keepdims=True)
        h0, h1, h2 = (p.astype(F32) for p in _split3(ch))
        qx = jnp.where(lane == 0, h0, jnp.where(lane == 1, h1, jnp.where(
            lane == 2, h2, jnp.where(lane < 6, one, zero))))
        kx = jnp.where(lane < 3, one, jnp.where(lane == 3, -h0, jnp.where(
            lane == 4, -h1, jnp.where(lane == 5, -h2, zero))))
        qx_ref[h] = qx.astype(BF16)
        kx_ref[h] = kx.astype(BF16)


def _fox_gate(lf3, n_heads):
    b, t, _ = lf3.shape
    tc = min(512, t)
    out = jax.ShapeDtypeStruct((b, n_heads, t, LANES), BF16)
    spec = pl.BlockSpec((None, n_heads, tc, LANES), lambda bi, ti: (bi, 0, ti, 0))
    return pl.pallas_call(
        functools.partial(_fox_gate_kernel, n_heads=n_heads),
        grid=(b, t // tc),
        in_specs=[pl.BlockSpec((None, tc, LANES), lambda bi, ti: (bi, ti, 0))],
        out_specs=[spec, spec],
        out_shape=[out, out],
        scratch_shapes=[pltpu.VMEM((SUBLANES, LANES), F32)],
        compiler_params=_cparams("arbitrary", "arbitrary"),
        name="fox_gate",
    )(lf3)


def _causal(s):
    row = lax.broadcasted_iota(jnp.int32, s.shape, 0)
    col = lax.broadcasted_iota(jnp.int32, s.shape, 1)
    return jnp.where(col <= row, s, NEG)


def _flash_init(s, v):
    m = jnp.max(s, axis=1, keepdims=True)
    p = jnp.exp(s - m)
    l = jnp.sum(p, axis=1, keepdims=True)
    acc = jnp.dot(p.astype(BF16), v, preferred_element_type=F32)
    return m, l, acc


def _flash_step(s, v, m, l, acc):
    m_new = jnp.maximum(m, jnp.max(s, axis=1, keepdims=True))
    alpha = jnp.exp(m - m_new)
    p = jnp.exp(s - m_new)
    l = alpha * l + jnp.sum(p, axis=1, keepdims=True)
    acc = alpha * acc + jnp.dot(p.astype(BF16), v, preferred_element_type=F32)
    return m_new, l, acc


def _past_blocks(qa, kaug_sc, v_ref, iq, blk, carry):
    def body(j, c):
        off = pl.multiple_of(j * blk, blk)
        s = lax.dot_general(qa, kaug_sc[pl.ds(off, blk), :], _NT, preferred_element_type=F32)
        return _flash_step(s, v_ref[pl.ds(off, blk), :], *c)
    return lax.fori_loop(0, iq, body, carry)


def _fox_attn_kernel(q_ref, k_ref, v_ref, qx_ref, kx_ref, o_ref, kaug_sc):
    iq = pl.program_id(2)
    blk = q_ref.shape[0]

    @pl.when(iq == 0)
    def _():
        kaug_sc[:, :HEAD_DIM] = k_ref[...]
        kaug_sc[:, HEAD_DIM:] = kx_ref[...]

    qa = jnp.concatenate([q_ref[...], qx_ref[...]], axis=1)
    start = pl.multiple_of(iq * blk, blk)
    s = lax.dot_general(qa, kaug_sc[pl.ds(start, blk), :], _NT, preferred_element_type=F32)
    carry = _flash_init(_causal(s), v_ref[pl.ds(start, blk), :])
    m, l, acc = _past_blocks(qa, kaug_sc, v_ref, iq, blk, carry)
    o_ref[...] = (acc / l).astype(o_ref.dtype)


def _moba_attn_kernel(q_ref, k_ref, v_ref, o_ref, kaug_sc, kmean_sc, *, n_blocks):
    iq = pl.program_id(2)
    blk = q_ref.shape[0]

    @pl.when(iq == 0)
    def _():
        t = k_ref.shape[0]
        kaug_sc[:, :HEAD_DIM] = k_ref[...]
        row = lax.broadcasted_iota(jnp.int32, (t, LANES), 0)
        lane = lax.broadcasted_iota(jnp.int32, (t, LANES), 1)
        kaug_sc[:, HEAD_DIM:] = ((row // blk) == lane).astype(BF16)
        kmean_sc[...] = jnp.zeros_like(kmean_sc)
        for n in range(n_blocks):
            kb = k_ref[n * blk:(n + 1) * blk, :].astype(F32)
            kmean_sc[n:n + 1, :] = jnp.mean(kb, axis=0, keepdims=True)

    q = q_ref[...]
    km_hi = kmean_sc[...].astype(BF16)
    km_lo = (kmean_sc[...] - km_hi.astype(F32)).astype(BF16)
    gate = (lax.dot_general(q, km_hi, _NT, preferred_element_type=F32)
            + lax.dot_general(q, km_lo, _NT, preferred_element_type=F32))
    lane_i = lax.broadcasted_iota(jnp.int32, gate.shape, 1)
    lane_f = lane_i.astype(F32)
    past = lane_i < iq
    g = jnp.where(past, gate, -jnp.inf)
    sel = jnp.zeros(gate.shape, jnp.bool_)
    for _ in range(MOBA_TOPK):
        mx = jnp.max(g, axis=1, keepdims=True)
        idx = jnp.min(jnp.where(g == mx, lane_f, float(LANES)), axis=1, keepdims=True)
        pick = lane_f == idx
        sel = sel | pick
        g = jnp.where(pick, -jnp.inf, g)
    bias = jnp.where(sel & past, 0.0, NEG).astype(BF16)
    qa = jnp.concatenate([q, bias], axis=1)

    start = pl.multiple_of(iq * blk, blk)
    s = lax.dot_general(q, k_ref[pl.ds(start, blk), :], _NT, preferred_element_type=F32)
    carry = _flash_init(_causal(s), v_ref[pl.ds(start, blk), :])
    m, l, acc = _past_blocks(qa, kaug_sc, v_ref, iq, blk, carry)
    o_ref[...] = (acc / l).astype(o_ref.dtype)


def _head_specs(b, t, blk, col0):
    q_spec = pl.BlockSpec((None, blk, HEAD_DIM), lambda bi, h, i: (bi, i, col0[0] + h))
    k_spec = pl.BlockSpec((None, t, HEAD_DIM), lambda bi, h, i: (bi, 0, col0[1] + h))
    v_spec = pl.BlockSpec((None, t, HEAD_DIM), lambda bi, h, i: (bi, 0, col0[2] + h))
    o_spec = pl.BlockSpec((None, blk, HEAD_DIM), lambda bi, h, i: (bi, i, h))
    return q_spec, k_spec, v_spec, o_spec


def _moba_attn(proj3, n_heads):
    b, t, _ = proj3.shape
    blk = MOBA_BLOCK
    assert t % blk == 0 and t // blk <= LANES
    q_spec, k_spec, v_spec, o_spec = _head_specs(b, t, blk, (0, n_heads, 2 * n_heads))
    return pl.pallas_call(
        functools.partial(_moba_attn_kernel, n_blocks=t // blk),
        grid=(b, n_heads, t // blk),
        in_specs=[q_spec, k_spec, v_spec],
        out_specs=o_spec,
        out_shape=jax.ShapeDtypeStruct((b, t, n_heads * HEAD_DIM), BF16),
        scratch_shapes=[pltpu.VMEM((t, 2 * HEAD_DIM), BF16), pltpu.VMEM((LANES, HEAD_DIM), F32)],
        compiler_params=_cparams("arbitrary", "arbitrary", "arbitrary"),
        name="moba_attn",
    )(proj3, proj3, proj3)


def _fox_attn(proj3, qx, kx, n_heads, head0):
    b, t, _ = proj3.shape
    blk = MOBA_BLOCK
    q_spec, k_spec, v_spec, o_spec = _head_specs(
        b, t, blk, (head0, head0 + n_heads, head0 + 2 * n_heads))
    qx_spec = pl.BlockSpec((None, None, blk, LANES), lambda bi, h, i: (bi, h, i, 0))
    kx_spec = pl.BlockSpec((None, None, t, LANES), lambda bi, h, i: (bi, h, 0, 0))
    return pl.pallas_call(
        _fox_attn_kernel,
        grid=(b, n_heads, t // blk),
        in_specs=[q_spec, k_spec, v_spec, qx_spec, kx_spec],
        out_specs=o_spec,
        out_shape=jax.ShapeDtypeStruct((b, t, n_heads * HEAD_DIM), BF16),
        scratch_shapes=[pltpu.VMEM((t, 2 * HEAD_DIM), BF16)],
        compiler_params=_cparams("arbitrary", "arbitrary", "arbitrary"),
        name="fox_attn",
    )(proj3, proj3, proj3, qx, kx)


def _mem_kv_kernel(mem_ref, g_ref, w_ref, k_ref, v_ref):
    xn = _rmsnorm(mem_ref[...], g_ref[...]).astype(BF16)
    kv = jnp.dot(xn, w_ref[...], preferred_element_type=F32)
    half = kv.shape[1] // 2
    k_ref[...] = kv[:, :half].astype(k_ref.dtype)
    v_ref[...] = kv[:, half:].astype(v_ref.dtype)


def _mem_kv(mem, g, w_ckv):
    b, nm, d = mem.shape
    xd = w_ckv.shape[1] // 2
    out = jax.ShapeDtypeStruct((b, nm, xd), BF16)
    o_spec = pl.BlockSpec((None, nm, xd), lambda bi: (bi, 0, 0))
    return pl.pallas_call(
        _mem_kv_kernel,
        grid=(b,),
        in_specs=[
            pl.BlockSpec((None, nm, d), lambda bi: (bi, 0, 0)),
            pl.BlockSpec((1, d), lambda bi: (0, 0)),
            pl.BlockSpec((d, 2 * xd), lambda bi: (0, 0)),
        ],
        out_specs=[o_spec, o_spec],
        out_shape=[out, out],
        compiler_params=_cparams("arbitrary"),
        name="mem_kv",
    )(mem, g, w_ckv)


def _mid_kernel(x_ref, om_ref, of_ref, wom_ref, wof_ref, gx_ref, wcq_ref, kc_ref, vc_ref,
                wco_ref, gf_ref, h_ref, xn_ref):
    h1 = (x_ref[...]
          + jnp.dot(om_ref[...], wom_ref[...], preferred_element_type=F32)
          + jnp.dot(of_ref[...], wof_ref[...], preferred_element_type=F32))
    xn2 = _rmsnorm(h1, gx_ref[...]).astype(BF16)
    q = (jnp.dot(xn2, wcq_ref[...], preferred_element_type=F32) * SCALE).astype(BF16)
    outs = []
    for h in range(N_XATTN_HEADS):
        sl = slice(h * HEAD_DIM, (h + 1) * HEAD_DIM)
        s = lax.dot_general(q[:, sl], kc_ref[:, sl], _NT, preferred_element_type=F32)
        m, l, acc = _flash_init(s, vc_ref[:, sl])
        outs.append((acc / l).astype(BF16))
    oc = jnp.concatenate(outs, axis=1)
    h2 = h1 + jnp.dot(oc, wco_ref[...], preferred_element_type=F32)
    h_ref[...] = h2
    xn_ref[...] = _rmsnorm(h2, gf_ref[...]).astype(xn_ref.dtype)


def _mid(x2, o_moba, o_fox, w_om, w_of, gx, w_cq, kc, vc, w_co, gf, *, seq):
    n, d = x2.shape
    wm, wf = o_moba.shape[1], o_fox.shape[1]
    nm, xd = kc.shape[1], kc.shape[2]
    tm = min(512, seq)
    t_tiles = seq // tm
    const = lambda i: (0, 0)
    return pl.pallas_call(
        _mid_kernel,
        grid=(n // tm,),
        in_specs=[
            pl.BlockSpec((tm, d), lambda i: (i, 0)),
            pl.BlockSpec((tm, wm), lambda i: (i, 0)),
            pl.BlockSpec((tm, wf), lambda i: (i, 0)),
            pl.BlockSpec((wm, d), const),
            pl.BlockSpec((wf, d), const),
            pl.BlockSpec((1, d), const),
            pl.BlockSpec((d, xd), const),
            pl.BlockSpec((None, nm, xd), lambda i: (i // t_tiles, 0, 0)),
            pl.BlockSpec((None, nm, xd), lambda i: (i // t_tiles, 0, 0)),
            pl.BlockSpec((xd, d), const),
            pl.BlockSpec((1, d), const),
        ],
        out_specs=[
            pl.BlockSpec((tm, d), lambda i: (i, 0)),
            pl.BlockSpec((tm, d), lambda i: (i, 0)),
        ],
        out_shape=[
            jax.ShapeDtypeStruct((n, d), F32),
            jax.ShapeDtypeStruct((n, d), BF16),
        ],
        compiler_params=_cparams("arbitrary"),
        name="mid",
    )(x2, o_moba, o_fox, w_om, w_of, gx, w_cq, kc, vc, w_co, gf)


def _conv_ffn_kernel(xn_ref, h_ref, wg_ref, wu_ref, cwg_ref, cwu_ref, cbg_ref, cbu_ref,
                     wd_ref, fg_ref, o_ref, hs_sc, tail_sc, *, tiles_per_seq):
    i = pl.program_id(0)
    f = pl.program_id(1)
    nf = pl.num_programs(1)
    tm = xn_ref.shape[0]
    tf = wg_ref.shape[1]
    halo = SUBLANES

    xn = xn_ref[...]
    first = (i % tiles_per_seq) == 0

    @pl.when(first)
    def _():
        hs_sc[0:halo, :] = jnp.zeros((halo, 2 * tf), F32)

    @pl.when(jnp.logical_not(first))
    def _():
        hs_sc[0:halo, :] = tail_sc[f]

    hs_sc[halo:, :tf] = jnp.dot(xn, wg_ref[...], preferred_element_type=F32)
    hs_sc[halo:, tf:] = jnp.dot(xn, wu_ref[...], preferred_element_type=F32)
    tail_sc[f] = hs_sc[tm:tm + halo, :]

    cw = jnp.concatenate([cwg_ref[...], cwu_ref[...]], axis=1)
    cb = jnp.concatenate([cbg_ref[...], cbu_ref[...]], axis=1)
    y = cb
    for k in range(CONV_WIDTH):
        lo = halo - (CONV_WIDTH - 1) + k
        y = y + hs_sc[lo:lo + tm, :] * cw[k:k + 1, :]
    gte = y[:, :tf]
    act = (gte * (1.0 / (1.0 + jnp.exp(-gte))) * y[:, tf:]).astype(BF16)
    part = jnp.dot(act, wd_ref[...], preferred_element_type=F32)

    @pl.when(f == 0)
    def _():
        o_ref[...] = h_ref[...] + part

    @pl.when(f > 0)
    def _():
        o_ref[...] += part

    @pl.when(f == nf - 1)
    def _():
        o_ref[...] = _rmsnorm(o_ref[...], fg_ref[...])


def _conv_ffn(xn3, h2, w_up, conv_w, conv_b, w_down, fg, *, seq):
    n, d = h2.shape
    ff = w_down.shape[0]
    tm = min(512, seq)
    tf = 512
    assert ff % tf == 0
    nf = ff // tf
    return pl.pallas_call(
        functools.partial(_conv_ffn_kernel, tiles_per_seq=seq // tm),
        grid=(n // tm, nf),
        in_specs=[
            pl.BlockSpec((tm, d), lambda i, f: (i, 0)),
            pl.BlockSpec((tm, d), lambda i, f: (i, 0)),
            pl.BlockSpec((d, tf), lambda i, f: (0, f)),
            pl.BlockSpec((d, tf), lambda i, f: (0, f + nf)),
            pl.BlockSpec((CONV_WIDTH, tf), lambda i, f: (0, f)),
            pl.BlockSpec((CONV_WIDTH, tf), lambda i, f: (0, f + nf)),
            pl.BlockSpec((1, tf), lambda i, f: (0, f)),
            pl.BlockSpec((1, tf), lambda i, f: (0, f + nf)),
            pl.BlockSpec((tf, d), lambda i, f: (f, 0)),
            pl.BlockSpec((1, d), lambda i, f: (0, 0)),
        ],
        out_specs=pl.BlockSpec((tm, d), lambda i, f: (i, 0)),
        out_shape=jax.ShapeDtypeStruct((n, d), F32),
        scratch_shapes=[
            pltpu.VMEM((tm + SUBLANES, 2 * tf), F32),
            pltpu.VMEM((nf, SUBLANES, 2 * tf), F32),
        ],
        compiler_params=_cparams("arbitrary", "arbitrary"),
        name="conv_ffn",
    )(xn3, h2, w_up, w_up, conv_w, conv_w, conv_b, conv_b, w_down, fg)


def _rope_tables(t):
    half = HEAD_DIM // 2
    inv = ROPE_THETA ** (-jnp.arange(half, dtype=F32) / half)
    ang = jnp.arange(t, dtype=F32)[:, None] * inv[None, :]
    cos, sin = jnp.cos(ang), jnp.sin(ang)
    return jnp.concatenate([cos, cos], axis=1), jnp.concatenate([-sin, sin], axis=1)


def kernel(x, mem, attn_norm_g, w_in, b_f, w_o, xattn_norm_g, mem_norm_g, w_cq, w_ckv, w_co,
           ffn_norm_g, w_up, conv_w, conv_b, w_down, final_norm_g):
    b, t, d = x.shape
    depth = w_in.shape[0]
    n_heads = d // HEAD_DIM
    n_moba = n_heads // 2
    n_fox = n_heads - n_moba
    assert n_moba == n_fox and n_fox <= LANES
    wgrp = n_moba * HEAD_DIM
    main_cols = 6 * wgrp
    cos, sin = _rope_tables(t)

    h = x.reshape(b * t, d)
    out = h
    for l in range(depth):
        w_main = w_in[l, :, :main_cols].astype(BF16)
        w_z = jnp.pad(w_in[l, :, main_cols:], ((0, 0), (0, LANES - n_fox))).astype(BF16)
        bf = jnp.pad(b_f[l].astype(F32), (0, LANES - n_fox)).reshape(1, LANES)
        proj, lf = _in_proj(h, attn_norm_g[l].reshape(1, d), w_main, w_z, bf, cos, sin,
                            seq=t, group_width=wgrp)
        proj3 = proj.reshape(b, t, main_cols)
        qx, kx = _fox_gate(lf.reshape(b, t, LANES), n_fox)
        o_moba = _moba_attn(proj3, n_moba)
        o_fox = _fox_attn(proj3, qx, kx, n_fox, 3 * n_moba)
        kc, vc = _mem_kv(mem, mem_norm_g[l].reshape(1, d), w_ckv[l].astype(BF16))
        wo = w_o[l].astype(BF16)
        h2, xn3 = _mid(h, o_moba.reshape(b * t, wgrp), o_fox.reshape(b * t, wgrp),
                       wo[:wgrp], wo[wgrp:], xattn_norm_g[l].reshape(1, d),
                       w_cq[l].astype(BF16), kc, vc, w_co[l].astype(BF16),
                       ffn_norm_g[l].reshape(1, d), seq=t)
        last = l == depth - 1
        fg = final_norm_g if last else jnp.ones_like(final_norm_g)
        out = _conv_ffn(xn3, h2, w_up[l].astype(BF16), conv_w[l], conv_b[l].reshape(1, -1),
                        w_down[l].astype(BF16), fg.reshape(1, d), seq=t)
        h = out
    return out.reshape(b, t, d)
```

```python
import functools
import math

import jax
import jax.numpy as jnp
from jax import lax
from jax.experimental import pallas as pl
from jax.experimental.pallas import tpu as pltpu

HEAD_DIM = 128
MOBA_BLOCK = 256
MOBA_TOPK = 3
ROPE_THETA = 10000.0
N_XATTN_HEADS = 4
CONV_WIDTH = 3
RMS_EPS = 1e-6
NEG = -1e30
SCALE = 1.0 / math.sqrt(HEAD_DIM)

LANES = 128
SUBLANES = 8
VMEM_LIMIT_BYTES = 56 * 1024 * 1024

F32 = jnp.float32
BF16 = jnp.bfloat16

_NT = (((1,), (1,)), ((), ()))


def _cparams(*sem):
    return pltpu.CompilerParams(dimension_semantics=sem, vmem_limit_bytes=VMEM_LIMIT_BYTES)


def _rmsnorm(x, g):
    ms = jnp.mean(x * x, axis=-1, keepdims=True)
    return x * lax.rsqrt(ms + RMS_EPS) * g


def _split3(x):
    hi = x.astype(BF16)
    r = x - hi.astype(F32)
    mid = r.astype(BF16)
    lo = (r - mid.astype(F32)).astype(BF16)
    return hi, mid, lo


def _in_proj_kernel(x_ref, g_ref, w_ref, wz_ref, bf_ref, cos_ref, sin_ref,
                    o_ref, lf_ref, xn_sc, *, group_of_tile):
    j = pl.program_id(1)

    @pl.when(j == 0)
    def _():
        xn = _rmsnorm(x_ref[...], g_ref[...]).astype(BF16)
        xn_sc[...] = xn
        z = jnp.dot(xn, wz_ref[...], preferred_element_type=F32) + bf_ref[...]
        lf_ref[...] = jnp.minimum(z, 0.0) - jnp.log(1.0 + jnp.exp(-jnp.abs(z)))

    y = jnp.dot(xn_sc[...], w_ref[...], preferred_element_type=F32)
    n_heads_tile = y.shape[1] // HEAD_DIM

    def rope(scale):
        cos = cos_ref[...]
        sin = sin_ref[...]
        for h in range(n_heads_tile):
            sl = slice(h * HEAD_DIM, (h + 1) * HEAD_DIM)
            yh = y[:, sl]
            r = yh * cos + pltpu.roll(yh, HEAD_DIM // 2, 1) * sin
            if scale != 1.0:
                r = r * scale
            o_ref[:, sl] = r.astype(o_ref.dtype)

    grp = group_of_tile(j)

    @pl.when(grp == 0)
    def _():
        rope(SCALE)

    @pl.when(grp == 1)
    def _():
        rope(1.0)

    @pl.when(grp == 3)
    def _():
        o_ref[...] = (y * SCALE).astype(o_ref.dtype)

    @pl.when((grp == 2) | (grp >= 4))
    def _():
        o_ref[...] = y.astype(o_ref.dtype)


def _in_proj(x2, g, w_main, w_z, b_f, cos, sin, *, seq, group_width):
    n, d = x2.shape
    cols = w_main.shape[1]
    tm = min(1024, seq)
    tn = min(1024, group_width)
    tiles_per_group = group_width // tn
    t_tiles = seq // tm
    kern = functools.partial(_in_proj_kernel, group_of_tile=lambda j: j // tiles_per_group)
    return pl.pallas_call(
        kern,
        grid=(n // tm, cols // tn),
        in_specs=[
            pl.BlockSpec((tm, d), lambda i, j: (i, 0)),
            pl.BlockSpec((1, d), lambda i, j: (0, 0)),
            pl.BlockSpec((d, tn), lambda i, j: (0, j)),
            pl.BlockSpec((d, LANES), lambda i, j: (0, 0)),
            pl.BlockSpec((1, LANES), lambda i, j: (0, 0)),
            pl.BlockSpec((tm, HEAD_DIM), lambda i, j: (i % t_tiles, 0)),
            pl.BlockSpec((tm, HEAD_DIM), lambda i, j: (i % t_tiles, 0)),
        ],
        out_specs=[
            pl.BlockSpec((tm, tn), lambda i, j: (i, j)),
            pl.BlockSpec((tm, LANES), lambda i, j: (i, 0)),
        ],
        out_shape=[
            jax.ShapeDtypeStruct((n, cols), BF16),
            jax.ShapeDtypeStruct((n, LANES), F32),
        ],
        scratch_shapes=[pltpu.VMEM((tm, d), BF16)],
        compiler_params=_cparams("arbitrary", "arbitrary"),
        name="in_proj",
    )(x2, g, w_main, w_z, b_f, cos, sin)


def _fox_gate_kernel(lf_ref, qx_ref, kx_ref, carry_sc, *, n_heads):
    t = pl.program_id(1)

    @pl.when(t == 0)
    def _():
        carry_sc[...] = jnp.zeros_like(carry_sc)

    lf = lf_ref[...]
    tc = lf.shape[0]
    row = lax.broadcasted_iota(jnp.int32, (tc, tc), 0)
    col = lax.broadcasted_iota(jnp.int32, (tc, tc), 1)
    tri = (col <= row).astype(BF16)
    c = carry_sc[0:1, :]
    for part in _split3(lf):
        c = c + jnp.dot(tri, part, preferred_element_type=F32)
    carry_sc[0:1, :] = c[tc - 1:tc, :]

    lane = lax.broadcasted_iota(jnp.int32, (tc, LANES), 1)
    one = jnp.ones((tc, LANES), F32)
    zero = jnp.zeros((tc, LANES), F32)
    for h in range(n_heads):
        ch = jnp.sum(jnp.where(lane == h, c, 0.0), axis=1, keepdims=True)
        h0, h1, h2 = (p.astype(F32) for p in _split3(ch))
        qx = jnp.where(lane == 0, h0, jnp.where(lane == 1, h1, jnp.where(
            lane == 2, h2, jnp.where(lane < 6, one, zero))))
        kx = jnp.where(lane < 3, one, jnp.where(lane == 3, -h0, jnp.where(
            lane == 4, -h1, jnp.where(lane == 5, -h2, zero))))
        qx_ref[h] = qx.astype(BF16)
        kx_ref[h] = kx.astype(BF16)


def _fox_gate(lf3, n_heads):
    b, t, _ = lf3.shape
    tc = min(512, t)
    out = jax.ShapeDtypeStruct((b, n_heads, t, LANES), BF16)
    spec = pl.BlockSpec((None, n_heads, tc, LANES), lambda bi, ti: (bi, 0, ti, 0))
    return pl.pallas_call(
        functools.partial(_fox_gate_kernel, n_heads=n_heads),
        grid=(b, t // tc),
        in_specs=[pl.BlockSpec((None, tc, LANES), lambda bi, ti: (bi, ti, 0))],
        out_specs=[spec, spec],
        out_shape=[out, out],
        scratch_shapes=[pltpu.VMEM((SUBLANES, LANES), F32)],
        compiler_params=_cparams("arbitrary", "arbitrary"),
        name="fox_gate",
    )(lf3)


ATT_TILE = 512
ATT_HEADS = 2


def _flash_init(s, v):
    m = jnp.max(s, axis=1, keepdims=True)
    p = jnp.exp(s - m)
    l = jnp.sum(p, axis=1, keepdims=True)
    acc = jnp.dot(p.astype(BF16), v, preferred_element_type=F32)
    return m, l, acc


def _moba_bias(q, kmean, iq):
    tq = q.shape[0]
    km_hi = kmean.astype(BF16)
    km_lo = (kmean - km_hi.astype(F32)).astype(BF16)
    gate = (lax.dot_general(q, km_hi, _NT, preferred_element_type=F32)
            + lax.dot_general(q, km_lo, _NT, preferred_element_type=F32))
    lane_i = lax.broadcasted_iota(jnp.int32, gate.shape, 1)
    row_i = lax.broadcasted_iota(jnp.int32, gate.shape, 0)
    own = iq * (tq // MOBA_BLOCK) + row_i // MOBA_BLOCK
    lane_f = lane_i.astype(F32)
    past = lane_i < own
    g = jnp.where(past, gate, -jnp.inf)
    sel = jnp.zeros(gate.shape, jnp.bool_)
    for _ in range(MOBA_TOPK):
        mx = jnp.max(g, axis=1, keepdims=True)
        idx = jnp.min(jnp.where(g == mx, lane_f, float(LANES)), axis=1, keepdims=True)
        pick = lane_f == idx
        sel = sel | pick
        g = jnp.where(pick, -jnp.inf, g)
    return jnp.where((sel & past) | (lane_i == own), 0.0, NEG).astype(BF16)


def _attn_kernel(*refs, moba):
    if moba:
        q_ref, k_ref, v_ref, o_ref, kaug_sc, vt_sc, kmean_sc = refs
    else:
        q_ref, k_ref, v_ref, qx_ref, kx_ref, o_ref, kaug_sc, vt_sc = refs
    iq = pl.program_id(2)
    tq = q_ref.shape[0]
    t = k_ref.shape[0]
    n_heads = q_ref.shape[1] // HEAD_DIM
    heads = [slice(g * HEAD_DIM, (g + 1) * HEAD_DIM) for g in range(n_heads)]

    @pl.when(iq == 0)
    def _():
        for g, hs in enumerate(heads):
            kaug_sc[g, :, :HEAD_DIM] = k_ref[:, hs]
            if moba:
                row = lax.broadcasted_iota(jnp.int32, (t, LANES), 0)
                lane = lax.broadcasted_iota(jnp.int32, (t, LANES), 1)
                kaug_sc[g, :, HEAD_DIM:] = ((row // MOBA_BLOCK) == lane).astype(BF16)
                kmean_sc[g] = jnp.zeros((LANES, HEAD_DIM), F32)
                for n in range(t // MOBA_BLOCK):
                    kb = k_ref[n * MOBA_BLOCK:(n + 1) * MOBA_BLOCK, hs].astype(F32)
                    kmean_sc[g, n:n + 1, :] = jnp.mean(kb, axis=0, keepdims=True)
            else:
                kaug_sc[g, :, HEAD_DIM:] = kx_ref[g]
            for j in range(t // tq):
                vt_sc[g, j] = v_ref[j * tq:(j + 1) * tq, hs].astype(F32).T.astype(BF16)

    qa = []
    for g, hs in enumerate(heads):
        q = q_ref[:, hs]
        extra = _moba_bias(q, kmean_sc[g], iq) if moba else qx_ref[g]
        qa.append(jnp.concatenate([q, extra], axis=1))

    def scores(g, j):
        off = pl.multiple_of(j * tq, tq)
        return lax.dot_general(kaug_sc[g, pl.ds(off, tq), :], qa[g], _NT,
                               preferred_element_type=F32)

    def update(g, j, s, m, l, acc):
        m_new = jnp.maximum(m, jnp.max(s, axis=0, keepdims=True))
        alpha = jnp.exp(m - m_new)
        p = jnp.exp(s - m_new)
        l = alpha * l + jnp.sum(p, axis=0, keepdims=True)
        acc = alpha * acc + jnp.dot(vt_sc[g, j], p.astype(BF16), preferred_element_type=F32)
        return m_new, l, acc

    stats = tuple((jnp.full((1, tq), NEG, F32), jnp.zeros((1, tq), F32),
                   jnp.zeros((HEAD_DIM, tq), F32)) for _ in range(n_heads))
    s0 = tuple(scores(g, 0) for g in range(n_heads))

    def body(j, carry):
        s_cur, stats = carry
        s_next = tuple(scores(g, j + 1) for g in range(n_heads))
        stats = tuple(update(g, j, s_cur[g], *stats[g]) for g in range(n_heads))
        return s_next, stats

    s_cur, stats = lax.fori_loop(0, iq, body, (s0, stats))
    for g, hs in enumerate(heads):
        key = lax.broadcasted_iota(jnp.int32, (tq, tq), 0)
        qry = lax.broadcasted_iota(jnp.int32, (tq, tq), 1)
        m, l, acc = update(g, iq, jnp.where(key <= qry, s_cur[g], NEG), *stats[g])
        o_ref[:, hs] = (acc / l).T.astype(o_ref.dtype)


def _attn(proj3, n_heads, head0, fox_extras=None):
    b, t, _ = proj3.shape
    moba = fox_extras is None
    tq = min(ATT_TILE, t)
    g = min(ATT_HEADS, n_heads)
    assert t % tq == 0 and tq % MOBA_BLOCK == 0 and t // MOBA_BLOCK <= LANES and n_heads % g == 0
    w = g * HEAD_DIM
    c0 = head0 // g
    in_specs = [
        pl.BlockSpec((None, tq, w), lambda bi, h, i: (bi, i, c0 + h)),
        pl.BlockSpec((None, t, w), lambda bi, h, i: (bi, 0, c0 + n_heads // g + h)),
        pl.BlockSpec((None, t, w), lambda bi, h, i: (bi, 0, c0 + 2 * (n_heads // g) + h)),
    ]
    args = [proj3, proj3, proj3]
    scratch = [pltpu.VMEM((g, t, 2 * HEAD_DIM), BF16), pltpu.VMEM((g, t // tq, HEAD_DIM, tq), BF16)]
    if moba:
        scratch.append(pltpu.VMEM((g, LANES, HEAD_DIM), F32))
    else:
        in_specs += [
            pl.BlockSpec((None, g, tq, LANES), lambda bi, h, i: (bi, h, i, 0)),
            pl.BlockSpec((None, g, t, LANES), lambda bi, h, i: (bi, h, 0, 0)),
        ]
        args += list(fox_extras)
    return pl.pallas_call(
        functools.partial(_attn_kernel, moba=moba),
        grid=(b, n_heads // g, t // tq),
        in_specs=in_specs,
        out_specs=pl.BlockSpec((None, tq, w), lambda bi, h, i: (bi, i, h)),
        out_shape=jax.ShapeDtypeStruct((b, t, n_heads * HEAD_DIM), BF16),
        scratch_shapes=scratch,
        compiler_params=_cparams("arbitrary", "arbitrary", "arbitrary"),
        name="moba_attn" if moba else "fox_attn",
    )(*args)


def _mem_kv_kernel(mem_ref, g_ref, w_ref, k_ref, v_ref):
    xn = _rmsnorm(mem_ref[...], g_ref[...]).astype(BF16)
    kv = jnp.dot(xn, w_ref[...], preferred_element_type=F32)
    half = kv.shape[1] // 2
    k_ref[...] = kv[:, :half].astype(k_ref.dtype)
    v_ref[...] = kv[:, half:].astype(v_ref.dtype)


def _mem_kv(mem, g, w_ckv):
    b, nm, d = mem.shape
    xd = w_ckv.shape[1] // 2
    out = jax.ShapeDtypeStruct((b, nm, xd), BF16)
    o_spec = pl.BlockSpec((None, nm, xd), lambda bi: (bi, 0, 0))
    return pl.pallas_call(
        _mem_kv_kernel,
        grid=(b,),
        in_specs=[
            pl.BlockSpec((None, nm, d), lambda bi: (bi, 0, 0)),
            pl.BlockSpec((1, d), lambda bi: (0, 0)),
            pl.BlockSpec((d, 2 * xd), lambda bi: (0, 0)),
        ],
        out_specs=[o_spec, o_spec],
        out_shape=[out, out],
        compiler_params=_cparams("arbitrary"),
        name="mem_kv",
    )(mem, g, w_ckv)


def _mid_kernel(x_ref, om_ref, of_ref, wom_ref, wof_ref, gx_ref, wcq_ref, kc_ref, vc_ref,
                wco_ref, gf_ref, h_ref, xn_ref):
    h1 = (x_ref[...]
          + jnp.dot(om_ref[...], wom_ref[...], preferred_element_type=F32)
          + jnp.dot(of_ref[...], wof_ref[...], preferred_element_type=F32))
    xn2 = _rmsnorm(h1, gx_ref[...]).astype(BF16)
    q = (jnp.dot(xn2, wcq_ref[...], preferred_element_type=F32) * SCALE).astype(BF16)
    outs = []
    for h in range(N_XATTN_HEADS):
        sl = slice(h * HEAD_DIM, (h + 1) * HEAD_DIM)
        s = lax.dot_general(q[:, sl], kc_ref[:, sl], _NT, preferred_element_type=F32)
        m, l, acc = _flash_init(s, vc_ref[:, sl])
        outs.append((acc / l).astype(BF16))
    oc = jnp.concatenate(outs, axis=1)
    h2 = h1 + jnp.dot(oc, wco_ref[...], preferred_element_type=F32)
    h_ref[...] = h2
    xn_ref[...] = _rmsnorm(h2, gf_ref[...]).astype(xn_ref.dtype)


def _mid(x2, o_moba, o_fox, w_om, w_of, gx, w_cq, kc, vc, w_co, gf, *, seq):
    n, d = x2.shape
    wm, wf = o_moba.shape[1], o_fox.shape[1]
    nm, xd = kc.shape[1], kc.shape[2]
    tm = min(512, seq)
    t_tiles = seq // tm
    const = lambda i: (0, 0)
    return pl.pallas_call(
        _mid_kernel,
        grid=(n // tm,),
        in_specs=[
            pl.BlockSpec((tm, d), lambda i: (i, 0)),
            pl.BlockSpec((tm, wm), lambda i: (i, 0)),
            pl.BlockSpec((tm, wf), lambda i: (i, 0)),
            pl.BlockSpec((wm, d), const),
            pl.BlockSpec((wf, d), const),
            pl.BlockSpec((1, d), const),
            pl.BlockSpec((d, xd), const),
            pl.BlockSpec((None, nm, xd), lambda i: (i // t_tiles, 0, 0)),
            pl.BlockSpec((None, nm, xd), lambda i: (i // t_tiles, 0, 0)),
            pl.BlockSpec((xd, d), const),
            pl.BlockSpec((1, d), const),
        ],
        out_specs=[
            pl.BlockSpec((tm, d), lambda i: (i, 0)),
            pl.BlockSpec((tm, d), lambda i: (i, 0)),
        ],
        out_shape=[
            jax.ShapeDtypeStruct((n, d), F32),
            jax.ShapeDtypeStruct((n, d), BF16),
        ],
        compiler_params=_cparams("arbitrary"),
        name="mid",
    )(x2, o_moba, o_fox, w_om, w_of, gx, w_cq, kc, vc, w_co, gf)


def _conv_ffn_kernel(xn_ref, h_ref, wg_ref, wu_ref, cwg_ref, cwu_ref, cbg_ref, cbu_ref,
                     wd_ref, fg_ref, o_ref, hs_sc, tail_sc, *, tiles_per_seq):
    i = pl.program_id(0)
    f = pl.program_id(1)
    nf = pl.num_programs(1)
    tm = xn_ref.shape[0]
    tf = wg_ref.shape[1]
    halo = SUBLANES

    xn = xn_ref[...]
    first = (i % tiles_per_seq) == 0

    @pl.when(first)
    def _():
        hs_sc[0:halo, :] = jnp.zeros((halo, 2 * tf), F32)

    @pl.when(jnp.logical_not(first))
    def _():
        hs_sc[0:halo, :] = tail_sc[f]

    hs_sc[halo:, :tf] = jnp.dot(xn, wg_ref[...], preferred_element_type=F32)
    hs_sc[halo:, tf:] = jnp.dot(xn, wu_ref[...], preferred_element_type=F32)
    tail_sc[f] = hs_sc[tm:tm + halo, :]

    cw = jnp.concatenate([cwg_ref[...], cwu_ref[...]], axis=1)
    cb = jnp.concatenate([cbg_ref[...], cbu_ref[...]], axis=1)
    y = cb
    for k in range(CONV_WIDTH):
        lo = halo - (CONV_WIDTH - 1) + k
        y = y + hs_sc[lo:lo + tm, :] * cw[k:k + 1, :]
    gte = y[:, :tf]
    act = (gte * (1.0 / (1.0 + jnp.exp(-gte))) * y[:, tf:]).astype(BF16)
    part = jnp.dot(act, wd_ref[...], preferred_element_type=F32)

    @pl.when(f == 0)
    def _():
        o_ref[...] = h_ref[...] + part

    @pl.when(f > 0)
    def _():
        o_ref[...] += part

    @pl.when(f == nf - 1)
    def _():
        o_ref[...] = _rmsnorm(o_ref[...], fg_ref[...])


def _conv_ffn(xn3, h2, w_up, conv_w, conv_b, w_down, fg, *, seq):
    n, d = h2.shape
    ff = w_down.shape[0]
    tm = min(512, seq)
    tf = 512
    assert ff % tf == 0
    nf = ff // tf
    return pl.pallas_call(
        functools.partial(_conv_ffn_kernel, tiles_per_seq=seq // tm),
        grid=(n // tm, nf),
        in_specs=[
            pl.BlockSpec((tm, d), lambda i, f: (i, 0)),
            pl.BlockSpec((tm, d), lambda i, f: (i, 0)),
            pl.BlockSpec((d, tf), lambda i, f: (0, f)),
            pl.BlockSpec((d, tf), lambda i, f: (0, f + nf)),
            pl.BlockSpec((CONV_WIDTH, tf), lambda i, f: (0, f)),
            pl.BlockSpec((CONV_WIDTH, tf), lambda i, f: (0, f + nf)),
            pl.BlockSpec((1, tf), lambda i, f: (0, f)),
            pl.BlockSpec((1, tf), lambda i, f: (0, f + nf)),
            pl.BlockSpec((tf, d), lambda i, f: (f, 0)),
            pl.BlockSpec((1, d), lambda i, f: (0, 0)),
        ],
        out_specs=pl.BlockSpec((tm, d), lambda i, f: (i, 0)),
        out_shape=jax.ShapeDtypeStruct((n, d), F32),
        scratch_shapes=[
            pltpu.VMEM((tm + SUBLANES, 2 * tf), F32),
            pltpu.VMEM((nf, SUBLANES, 2 * tf), F32),
        ],
        compiler_params=_cparams("arbitrary", "arbitrary"),
        name="conv_ffn",
    )(xn3, h2, w_up, w_up, conv_w, conv_w, conv_b, conv_b, w_down, fg)


def _rope_tables(t):
    half = HEAD_DIM // 2
    inv = ROPE_THETA ** (-jnp.arange(half, dtype=F32) / half)
    ang = jnp.arange(t, dtype=F32)[:, None] * inv[None, :]
    cos, sin = jnp.cos(ang), jnp.sin(ang)
    return jnp.concatenate([cos, cos], axis=1), jnp.concatenate([-sin, sin], axis=1)


def kernel(x, mem, attn_norm_g, w_in, b_f, w_o, xattn_norm_g, mem_norm_g, w_cq, w_ckv, w_co,
           ffn_norm_g, w_up, conv_w, conv_b, w_down, final_norm_g):
    b, t, d = x.shape
    depth = w_in.shape[0]
    n_heads = d // HEAD_DIM
    n_moba = n_heads // 2
    n_fox = n_heads - n_moba
    assert n_moba == n_fox and n_fox <= LANES
    wgrp = n_moba * HEAD_DIM
    main_cols = 6 * wgrp
    cos, sin = _rope_tables(t)

    h = x.reshape(b * t, d)
    out = h
    for l in range(depth):
        w_main = w_in[l, :, :main_cols].astype(BF16)
        w_z = jnp.pad(w_in[l, :, main_cols:], ((0, 0), (0, LANES - n_fox))).astype(BF16)
        bf = jnp.pad(b_f[l].astype(F32), (0, LANES - n_fox)).reshape(1, LANES)
        proj, lf = _in_proj(h, attn_norm_g[l].reshape(1, d), w_main, w_z, bf, cos, sin,
                            seq=t, group_width=wgrp)
        proj3 = proj.reshape(b, t, main_cols)
        qx, kx = _fox_gate(lf.reshape(b, t, LANES), n_fox)
        o_moba = _attn(proj3, n_moba, 0)
        o_fox = _attn(proj3, n_fox, 3 * n_moba, (qx, kx))
        kc, vc = _mem_kv(mem, mem_norm_g[l].reshape(1, d), w_ckv[l].astype(BF16))
        wo = w_o[l].astype(BF16)
        h2, xn3 = _mid(h, o_moba.reshape(b * t, wgrp), o_fox.reshape(b * t, wgrp),
                       wo[:wgrp], wo[wgrp:], xattn_norm_g[l].reshape(1, d),
                       w_cq[l].astype(BF16), kc, vc, w_co[l].astype(BF16),
                       ffn_norm_g[l].reshape(1, d), seq=t)
        last = l == depth - 1
        fg = final_norm_g if last else jnp.ones_like(final_norm_g)
        out = _conv_ffn(xn3, h2, w_up[l].astype(BF16), conv_w[l], conv_b[l].reshape(1, -1),
                        w_down[l].astype(BF16), fg.reshape(1, d), seq=t)
        h = out
    return out.reshape(b, t, d)
```

```python
import functools
import math

import jax
import jax.numpy as jnp
from jax import lax
from jax.experimental import pallas as pl
from jax.experimental.pallas import tpu as pltpu

HEAD_DIM = 128
MOBA_BLOCK = 256
MOBA_TOPK = 3
ROPE_THETA = 10000.0
N_XATTN_HEADS = 4
CONV_WIDTH = 3
RMS_EPS = 1e-6
NEG = -1e30
SCALE = 1.0 / math.sqrt(HEAD_DIM)
LOG2E = math.log2(math.e)
QSCALE = SCALE * LOG2E

LANES = 128
SUBLANES = 8
VMEM_LIMIT_BYTES = 56 * 1024 * 1024

F32 = jnp.float32
BF16 = jnp.bfloat16

_NT = (((1,), (1,)), ((), ()))


def _cparams(*sem):
    return pltpu.CompilerParams(dimension_semantics=sem, vmem_limit_bytes=VMEM_LIMIT_BYTES)


def _rmsnorm(x, g):
    ms = jnp.mean(x * x, axis=-1, keepdims=True)
    return x * lax.rsqrt(ms + RMS_EPS) * g


def _split3(x):
    hi = x.astype(BF16)
    r = x - hi.astype(F32)
    mid = r.astype(BF16)
    lo = (r - mid.astype(F32)).astype(BF16)
    return hi, mid, lo


def _in_proj_kernel(x_ref, g_ref, w_ref, wz_ref, bf_ref, cos_ref, sin_ref,
                    o_ref, lf_ref, xn_sc, *, group_of_tile):
    j = pl.program_id(1)

    @pl.when(j == 0)
    def _():
        xn = _rmsnorm(x_ref[...], g_ref[...]).astype(BF16)
        xn_sc[...] = xn
        z = jnp.dot(xn, wz_ref[...], preferred_element_type=F32) + bf_ref[...]
        lf_ref[...] = jnp.minimum(z, 0.0) - jnp.log(1.0 + jnp.exp(-jnp.abs(z)))

    y = jnp.dot(xn_sc[...], w_ref[...], preferred_element_type=F32)
    n_heads_tile = y.shape[1] // HEAD_DIM

    def rope(scale):
        cos = cos_ref[...]
        sin = sin_ref[...]
        for h in range(n_heads_tile):
            sl = slice(h * HEAD_DIM, (h + 1) * HEAD_DIM)
            yh = y[:, sl]
            r = yh * cos + pltpu.roll(yh, HEAD_DIM // 2, 1) * sin
            if scale != 1.0:
                r = r * scale
            o_ref[:, sl] = r.astype(o_ref.dtype)

    grp = group_of_tile(j)

    @pl.when(grp == 0)
    def _():
        rope(QSCALE)

    @pl.when(grp == 1)
    def _():
        rope(1.0)

    @pl.when(grp == 3)
    def _():
        o_ref[...] = (y * QSCALE).astype(o_ref.dtype)

    @pl.when((grp == 2) | (grp >= 4))
    def _():
        o_ref[...] = y.astype(o_ref.dtype)


def _in_proj(x2, g, w_main, w_z, b_f, cos, sin, *, seq, group_width):
    n, d = x2.shape
    cols = w_main.shape[1]
    tm = min(1024, seq)
    tn = min(1024, group_width)
    tiles_per_group = group_width // tn
    t_tiles = seq // tm
    kern = functools.partial(_in_proj_kernel, group_of_tile=lambda j: j // tiles_per_group)
    return pl.pallas_call(
        kern,
        grid=(n // tm, cols // tn),
        in_specs=[
            pl.BlockSpec((tm, d), lambda i, j: (i, 0)),
            pl.BlockSpec((1, d), lambda i, j: (0, 0)),
            pl.BlockSpec((d, tn), lambda i, j: (0, j)),
            pl.BlockSpec((d, LANES), lambda i, j: (0, 0)),
            pl.BlockSpec((1, LANES), lambda i, j: (0, 0)),
            pl.BlockSpec((tm, HEAD_DIM), lambda i, j: (i % t_tiles, 0)),
            pl.BlockSpec((tm, HEAD_DIM), lambda i, j: (i % t_tiles, 0)),
        ],
        out_specs=[
            pl.BlockSpec((tm, tn), lambda i, j: (i, j)),
            pl.BlockSpec((tm, LANES), lambda i, j: (i, 0)),
        ],
        out_shape=[
            jax.ShapeDtypeStruct((n, cols), BF16),
            jax.ShapeDtypeStruct((n, LANES), F32),
        ],
        scratch_shapes=[pltpu.VMEM((tm, d), BF16)],
        compiler_params=_cparams("arbitrary", "arbitrary"),
        name="in_proj",
    )(x2, g, w_main, w_z, b_f, cos, sin)


def _fox_gate_kernel(lf_ref, qx_ref, kx_ref, carry_sc, *, n_heads):
    t = pl.program_id(1)

    @pl.when(t == 0)
    def _():
        carry_sc[...] = jnp.zeros_like(carry_sc)

    lf = lf_ref[...]
    tc = lf.shape[0]
    row = lax.broadcasted_iota(jnp.int32, (tc, tc), 0)
    col = lax.broadcasted_iota(jnp.int32, (tc, tc), 1)
    tri = (col <= row).astype(BF16)
    c = carry_sc[0:1, :]
    for part in _split3(lf):
        c = c + jnp.dot(tri, part, preferred_element_type=F32)
    carry_sc[0:1, :] = c[tc - 1:tc, :]

    lane = lax.broadcasted_iota(jnp.int32, (tc, LANES), 1)
    one = jnp.ones((tc, LANES), F32)
    zero = jnp.zeros((tc, LANES), F32)
    for h in range(n_heads):
        ch = jnp.sum(jnp.where(lane == h, c, 0.0), axis=1, keepdims=True)
        h0, h1, h2 = (p.astype(F32) for p in _split3(ch * LOG2E))
        qx = jnp.where(lane == 0, h0, jnp.where(lane == 1, h1, jnp.where(
            lane == 2, h2, jnp.where(lane < 6, one, zero))))
        kx = jnp.where(lane < 3, one, jnp.where(lane == 3, -h0, jnp.where(
            lane == 4, -h1, jnp.where(lane == 5, -h2, zero))))
        qx_ref[h] = qx.astype(BF16)
        kx_ref[h] = kx.astype(BF16)


def _fox_gate(lf3, n_heads):
    b, t, _ = lf3.shape
    tc = min(512, t)
    out = jax.ShapeDtypeStruct((b, n_heads, t, LANES), BF16)
    spec = pl.BlockSpec((None, n_heads, tc, LANES), lambda bi, ti: (bi, 0, ti, 0))
    return pl.pallas_call(
        functools.partial(_fox_gate_kernel, n_heads=n_heads),
        grid=(b, t // tc),
        in_specs=[pl.BlockSpec((None, tc, LANES), lambda bi, ti: (bi, ti, 0))],
        out_specs=[spec, spec],
        out_shape=[out, out],
        scratch_shapes=[pltpu.VMEM((SUBLANES, LANES), F32)],
        compiler_params=_cparams("arbitrary", "arbitrary"),
        name="fox_gate",
    )(lf3)


ATT_TILE = 512
ATT_HEADS = 4
VT_ROWS = HEAD_DIM + 16


def _flash_init(s, v):
    m = jnp.max(s, axis=1, keepdims=True)
    p = jnp.exp(s - m)
    l = jnp.sum(p, axis=1, keepdims=True)
    acc = jnp.dot(p.astype(BF16), v, preferred_element_type=F32)
    return m, l, acc


def _moba_bias(q, kmean, iq):
    tq = q.shape[0]
    nb = kmean.shape[0]
    km_hi = kmean.astype(BF16)
    km_lo = (kmean - km_hi.astype(F32)).astype(BF16)
    gate = (lax.dot_general(km_hi, q, _NT, preferred_element_type=F32)
            + lax.dot_general(km_lo, q, _NT, preferred_element_type=F32))
    blk_i = lax.broadcasted_iota(jnp.int32, gate.shape, 0)
    qry_i = lax.broadcasted_iota(jnp.int32, gate.shape, 1)
    own = iq * (tq // MOBA_BLOCK) + qry_i // MOBA_BLOCK
    blk_f = blk_i.astype(F32)
    past = blk_i < own
    g = jnp.where(past, gate, -jnp.inf)
    sel = jnp.zeros(gate.shape, jnp.bool_)
    for _ in range(MOBA_TOPK):
        mx = jnp.max(g, axis=0, keepdims=True)
        idx = jnp.min(jnp.where(g == mx, blk_f, float(LANES)), axis=0, keepdims=True)
        pick = blk_f == idx
        sel = sel | pick
        g = jnp.where(pick, -jnp.inf, g)
    bias_t = jnp.where((sel & past) | (blk_i == own), 0.0, NEG)
    bias_t = jnp.concatenate([bias_t, jnp.zeros((LANES - nb, tq), F32)], axis=0)
    return bias_t.T.astype(BF16)


def _attn_kernel(*refs, moba):
    if moba:
        q_ref, k_ref, v_ref, o_ref = refs[:4]
        kmean_sc = refs[-1]
        refs = refs[:-1]
    else:
        q_ref, k_ref, v_ref, qx_ref, kx_ref, o_ref = refs[:6]
    kaug_sc, vt_sc, sa_sc, sb_sc, mxa_sc, mxb_sc, m_sc, acc_sc = refs[-8:]
    iq = pl.program_id(2)
    tq = q_ref.shape[0]
    t = k_ref.shape[0]
    n_heads = q_ref.shape[1] // HEAD_DIM
    heads = [slice(g * HEAD_DIM, (g + 1) * HEAD_DIM) for g in range(n_heads)]

    @pl.when(iq == 0)
    def _():
        for g, hs in enumerate(heads):
            kaug_sc[g, :, :HEAD_DIM] = k_ref[:, hs]
            if moba:
                row = lax.broadcasted_iota(jnp.int32, (t, LANES), 0)
                lane = lax.broadcasted_iota(jnp.int32, (t, LANES), 1)
                kaug_sc[g, :, HEAD_DIM:] = ((row // MOBA_BLOCK) == lane).astype(BF16)
                kmean_sc[g] = jnp.zeros((LANES, HEAD_DIM), F32)
                for n in range(t // MOBA_BLOCK):
                    kb = k_ref[n * MOBA_BLOCK:(n + 1) * MOBA_BLOCK, hs].astype(F32)
                    kmean_sc[g, n:n + 1, :] = jnp.mean(kb, axis=0, keepdims=True)
            else:
                kaug_sc[g, :, HEAD_DIM:] = kx_ref[g]
            pad_row = lax.broadcasted_iota(jnp.int32, (VT_ROWS - HEAD_DIM, tq), 0)
            for j in range(t // tq):
                vt_sc[g, j, :HEAD_DIM, :] = (
                    v_ref[j * tq:(j + 1) * tq, hs].astype(F32).T.astype(BF16))
                vt_sc[g, j, HEAD_DIM:, :] = (pad_row == 0).astype(BF16)

    qa = []
    for g, hs in enumerate(heads):
        q = q_ref[:, hs]
        if moba:
            nb = -(-(t // MOBA_BLOCK) // 16) * 16
            extra = _moba_bias(q, kmean_sc[g, :nb, :], iq)
        else:
            extra = qx_ref[g]
        qa.append(jnp.concatenate([q, extra], axis=1))

    def scores(g, j):
        off = pl.multiple_of(j * tq, tq)
        return lax.dot_general(kaug_sc[g, pl.ds(off, tq), :], qa[g], _NT,
                               preferred_element_type=F32)

    half = tq // 2
    halves = (slice(0, half), slice(half, tq))

    def produce(j, s_ref, mx_ref):
        for g in range(n_heads):
            s = scores(g, j)
            s_ref[g] = s
            mx_ref[g] = jnp.max(s, axis=0, keepdims=True)

    def consume(j, s_ref, mx_ref):
        for g in range(n_heads):
            m = m_sc[g]
            m_new = jnp.maximum(m, mx_ref[g])
            alpha = jnp.exp2(m - m_new)
            m_sc[g] = m_new
            for qs in halves:
                pv = alpha[:, qs] * acc_sc[g, :, qs]
                for ks in halves:
                    p = jnp.exp2(s_ref[g, ks, qs] - m_new[:, qs])
                    pv = pv + jnp.dot(vt_sc[g, j, :, ks], p.astype(BF16),
                                      preferred_element_type=F32)
                acc_sc[g, :, qs] = pv

    for g in range(n_heads):
        m_sc[g] = jnp.full((1, tq), NEG, F32)
        acc_sc[g] = jnp.zeros((VT_ROWS, tq), F32)
    produce(0, sa_sc, mxa_sc)

    def body(jj, carry):
        j = 2 * jj
        produce(j + 1, sb_sc, mxb_sc)
        consume(j, sa_sc, mxa_sc)
        produce(j + 2, sa_sc, mxa_sc)
        consume(j + 1, sb_sc, mxb_sc)
        return carry

    lax.fori_loop(0, iq // 2, body, 0)

    def finish(s_ref, mx_ref):
        key = lax.broadcasted_iota(jnp.int32, (tq, tq), 0)
        qry = lax.broadcasted_iota(jnp.int32, (tq, tq), 1)
        for g in range(n_heads):
            s = jnp.where(key <= qry, s_ref[g], NEG)
            s_ref[g] = s
            mx_ref[g] = jnp.max(s, axis=0, keepdims=True)
        consume(iq, s_ref, mx_ref)
        for g, hs in enumerate(heads):
            out = acc_sc[g, :HEAD_DIM, :] / acc_sc[g, HEAD_DIM:HEAD_DIM + 1, :]
            o_ref[:, hs] = out.T.astype(o_ref.dtype)

    @pl.when(iq % 2 == 0)
    def _():
        finish(sa_sc, mxa_sc)

    @pl.when(iq % 2 == 1)
    def _():
        produce(iq, sb_sc, mxb_sc)
        consume(iq - 1, sa_sc, mxa_sc)
        finish(sb_sc, mxb_sc)


def _attn(proj3, n_heads, head0, fox_extras=None):
    b, t, _ = proj3.shape
    moba = fox_extras is None
    tq = min(ATT_TILE, t)
    g = min(ATT_HEADS, n_heads)
    assert t % tq == 0 and tq % MOBA_BLOCK == 0 and t // MOBA_BLOCK <= LANES and n_heads % g == 0
    w = g * HEAD_DIM
    c0 = head0 // g
    in_specs = [
        pl.BlockSpec((None, tq, w), lambda bi, h, i: (bi, i, c0 + h)),
        pl.BlockSpec((None, t, w), lambda bi, h, i: (bi, 0, c0 + n_heads // g + h)),
        pl.BlockSpec((None, t, w), lambda bi, h, i: (bi, 0, c0 + 2 * (n_heads // g) + h)),
    ]
    args = [proj3, proj3, proj3]
    scratch = [
        pltpu.VMEM((g, t, 2 * HEAD_DIM), BF16),
        pltpu.VMEM((g, t // tq, VT_ROWS, tq), BF16),
        pltpu.VMEM((g, tq, tq), F32),
        pltpu.VMEM((g, tq, tq), F32),
        pltpu.VMEM((g, 1, tq), F32),
        pltpu.VMEM((g, 1, tq), F32),
        pltpu.VMEM((g, 1, tq), F32),
        pltpu.VMEM((g, VT_ROWS, tq), F32),
    ]
    if moba:
        scratch.append(pltpu.VMEM((g, LANES, HEAD_DIM), F32))
    else:
        in_specs += [
            pl.BlockSpec((None, g, tq, LANES), lambda bi, h, i: (bi, h, i, 0)),
            pl.BlockSpec((None, g, t, LANES), lambda bi, h, i: (bi, h, 0, 0)),
        ]
        args += list(fox_extras)
    return pl.pallas_call(
        functools.partial(_attn_kernel, moba=moba),
        grid=(b, n_heads // g, t // tq),
        in_specs=in_specs,
        out_specs=pl.BlockSpec((None, tq, w), lambda bi, h, i: (bi, i, h)),
        out_shape=jax.ShapeDtypeStruct((b, t, n_heads * HEAD_DIM), BF16),
        scratch_shapes=scratch,
        compiler_params=_cparams("arbitrary", "arbitrary", "arbitrary"),
        name="moba_attn" if moba else "fox_attn",
    )(*args)


def _mem_kv_kernel(mem_ref, g_ref, w_ref, k_ref, v_ref):
    xn = _rmsnorm(mem_ref[...], g_ref[...]).astype(BF16)
    kv = jnp.dot(xn, w_ref[...], preferred_element_type=F32)
    half = kv.shape[1] // 2
    k_ref[...] = kv[:, :half].astype(k_ref.dtype)
    v_ref[...] = kv[:, half:].astype(v_ref.dtype)


def _mem_kv(mem, g, w_ckv):
    b, nm, d = mem.shape
    xd = w_ckv.shape[1] // 2
    out = jax.ShapeDtypeStruct((b, nm, xd), BF16)
    o_spec = pl.BlockSpec((None, nm, xd), lambda bi: (bi, 0, 0))
    return pl.pallas_call(
        _mem_kv_kernel,
        grid=(b,),
        in_specs=[
            pl.BlockSpec((None, nm, d), lambda bi: (bi, 0, 0)),
            pl.BlockSpec((1, d), lambda bi: (0, 0)),
            pl.BlockSpec((d, 2 * xd), lambda bi: (0, 0)),
        ],
        out_specs=[o_spec, o_spec],
        out_shape=[out, out],
        compiler_params=_cparams("arbitrary"),
        name="mem_kv",
    )(mem, g, w_ckv)


def _mid_kernel(x_ref, om_ref, of_ref, wom_ref, wof_ref, gx_ref, wcq_ref, kc_ref, vc_ref,
                wco_ref, gf_ref, h_ref, xn_ref):
    h1 = (x_ref[...]
          + jnp.dot(om_ref[...], wom_ref[...], preferred_element_type=F32)
          + jnp.dot(of_ref[...], wof_ref[...], preferred_element_type=F32))
    xn2 = _rmsnorm(h1, gx_ref[...]).astype(BF16)
    q = (jnp.dot(xn2, wcq_ref[...], preferred_element_type=F32) * SCALE).astype(BF16)
    outs = []
    for h in range(N_XATTN_HEADS):
        sl = slice(h * HEAD_DIM, (h + 1) * HEAD_DIM)
        s = lax.dot_general(q[:, sl], kc_ref[:, sl], _NT, preferred_element_type=F32)
        m, l, acc = _flash_init(s, vc_ref[:, sl])
        outs.append((acc / l).astype(BF16))
    oc = jnp.concatenate(outs, axis=1)
    h2 = h1 + jnp.dot(oc, wco_ref[...], preferred_element_type=F32)
    h_ref[...] = h2
    xn_ref[...] = _rmsnorm(h2, gf_ref[...]).astype(xn_ref.dtype)


def _mid(x2, o_moba, o_fox, w_om, w_of, gx, w_cq, kc, vc, w_co, gf, *, seq):
    n, d = x2.shape
    wm, wf = o_moba.shape[1], o_fox.shape[1]
    nm, xd = kc.shape[1], kc.shape[2]
    tm = min(512, seq)
    t_tiles = seq // tm
    const = lambda i: (0, 0)
    return pl.pallas_call(
        _mid_kernel,
        grid=(n // tm,),
        in_specs=[
            pl.BlockSpec((tm, d), lambda i: (i, 0)),
            pl.BlockSpec((tm, wm), lambda i: (i, 0)),
            pl.BlockSpec((tm, wf), lambda i: (i, 0)),
            pl.BlockSpec((wm, d), const),
            pl.BlockSpec((wf, d), const),
            pl.BlockSpec((1, d), const),
            pl.BlockSpec((d, xd), const),
            pl.BlockSpec((None, nm, xd), lambda i: (i // t_tiles, 0, 0)),
            pl.BlockSpec((None, nm, xd), lambda i: (i // t_tiles, 0, 0)),
            pl.BlockSpec((xd, d), const),
            pl.BlockSpec((1, d), const),
        ],
        out_specs=[
            pl.BlockSpec((tm, d), lambda i: (i, 0)),
            pl.BlockSpec((tm, d), lambda i: (i, 0)),
        ],
        out_shape=[
            jax.ShapeDtypeStruct((n, d), F32),
            jax.ShapeDtypeStruct((n, d), BF16),
        ],
        compiler_params=_cparams("arbitrary"),
        name="mid",
    )(x2, o_moba, o_fox, w_om, w_of, gx, w_cq, kc, vc, w_co, gf)


def _conv_ffn_kernel(xn_ref, h_ref, wg_ref, wu_ref, cwg_ref, cwu_ref, cbg_ref, cbu_ref,
                     wd_ref, fg_ref, o_ref, hs_sc, tail_sc, *, tiles_per_seq):
    i = pl.program_id(0)
    f = pl.program_id(1)
    nf = pl.num_programs(1)
    tm = xn_ref.shape[0]
    tf = wg_ref.shape[1]
    halo = SUBLANES

    xn = xn_ref[...]
    first = (i % tiles_per_seq) == 0

    @pl.when(first)
    def _():
        hs_sc[0:halo, :] = jnp.zeros((halo, 2 * tf), F32)

    @pl.when(jnp.logical_not(first))
    def _():
        hs_sc[0:halo, :] = tail_sc[f]

    hs_sc[halo:, :tf] = jnp.dot(xn, wg_ref[...], preferred_element_type=F32)
    hs_sc[halo:, tf:] = jnp.dot(xn, wu_ref[...], preferred_element_type=F32)
    tail_sc[f] = hs_sc[tm:tm + halo, :]

    cw = jnp.concatenate([cwg_ref[...], cwu_ref[...]], axis=1)
    cb = jnp.concatenate([cbg_ref[...], cbu_ref[...]], axis=1)
    y = cb
    for k in range(CONV_WIDTH):
        lo = halo - (CONV_WIDTH - 1) + k
        y = y + hs_sc[lo:lo + tm, :] * cw[k:k + 1, :]
    gte = y[:, :tf]
    act = (gte * (1.0 / (1.0 + jnp.exp2(gte * (-LOG2E)))) * y[:, tf:]).astype(BF16)
    part = jnp.dot(act, wd_ref[...], preferred_element_type=F32)

    @pl.when(f == 0)
    def _():
        o_ref[...] = h_ref[...] + part

    @pl.when(f > 0)
    def _():
        o_ref[...] += part

    @pl.when(f == nf - 1)
    def _():
        o_ref[...] = _rmsnorm(o_ref[...], fg_ref[...])


def _conv_ffn(xn3, h2, w_up, conv_w, conv_b, w_down, fg, *, seq):
    n, d = h2.shape
    ff = w_down.shape[0]
    tm = min(512, seq)
    tf = 512
    assert ff % tf == 0
    nf = ff // tf
    return pl.pallas_call(
        functools.partial(_conv_ffn_kernel, tiles_per_seq=seq // tm),
        grid=(n // tm, nf),
        in_specs=[
            pl.BlockSpec((tm, d), lambda i, f: (i, 0)),
            pl.BlockSpec((tm, d), lambda i, f: (i, 0)),
            pl.BlockSpec((d, tf), lambda i, f: (0, f)),
            pl.BlockSpec((d, tf), lambda i, f: (0, f + nf)),
            pl.BlockSpec((CONV_WIDTH, tf), lambda i, f: (0, f)),
            pl.BlockSpec((CONV_WIDTH, tf), lambda i, f: (0, f + nf)),
            pl.BlockSpec((1, tf), lambda i, f: (0, f)),
            pl.BlockSpec((1, tf), lambda i, f: (0, f + nf)),
            pl.BlockSpec((tf, d), lambda i, f: (f, 0)),
            pl.BlockSpec((1, d), lambda i, f: (0, 0)),
        ],
        out_specs=pl.BlockSpec((tm, d), lambda i, f: (i, 0)),
        out_shape=jax.ShapeDtypeStruct((n, d), F32),
        scratch_shapes=[
            pltpu.VMEM((tm + SUBLANES, 2 * tf), F32),
            pltpu.VMEM((nf, SUBLANES, 2 * tf), F32),
        ],
        compiler_params=_cparams("arbitrary", "arbitrary"),
        name="conv_ffn",
    )(xn3, h2, w_up, w_up, conv_w, conv_w, conv_b, conv_b, w_down, fg)


def _rope_tables(t):
    half = HEAD_DIM // 2
    inv = ROPE_THETA ** (-jnp.arange(half, dtype=F32) / half)
    ang = jnp.arange(t, dtype=F32)[:, None] * inv[None, :]
    cos, sin = jnp.cos(ang), jnp.sin(ang)
    return jnp.concatenate([cos, cos], axis=1), jnp.concatenate([-sin, sin], axis=1)


def kernel(x, mem, attn_norm_g, w_in, b_f, w_o, xattn_norm_g, mem_norm_g, w_cq, w_ckv, w_co,
           ffn_norm_g, w_up, conv_w, conv_b, w_down, final_norm_g):
    b, t, d = x.shape
    depth = w_in.shape[0]
    n_heads = d // HEAD_DIM
    n_moba = n_heads // 2
    n_fox = n_heads - n_moba
    assert n_moba == n_fox and n_fox <= LANES
    wgrp = n_moba * HEAD_DIM
    main_cols = 6 * wgrp
    cos, sin = _rope_tables(t)

    h = x.reshape(b * t, d)
    out = h
    for l in range(depth):
        w_main = w_in[l, :, :main_cols].astype(BF16)
        w_z = jnp.pad(w_in[l, :, main_cols:], ((0, 0), (0, LANES - n_fox))).astype(BF16)
        bf = jnp.pad(b_f[l].astype(F32), (0, LANES - n_fox)).reshape(1, LANES)
        proj, lf = _in_proj(h, attn_norm_g[l].reshape(1, d), w_main, w_z, bf, cos, sin,
                            seq=t, group_width=wgrp)
        proj3 = proj.reshape(b, t, main_cols)
        qx, kx = _fox_gate(lf.reshape(b, t, LANES), n_fox)
        o_moba = _attn(proj3, n_moba, 0)
        o_fox = _attn(proj3, n_fox, 3 * n_moba, (qx, kx))
        kc, vc = _mem_kv(mem, mem_norm_g[l].reshape(1, d), w_ckv[l].astype(BF16))
        wo = w_o[l].astype(BF16)
        h2, xn3 = _mid(h, o_moba.reshape(b * t, wgrp), o_fox.reshape(b * t, wgrp),
                       wo[:wgrp], wo[wgrp:], xattn_norm_g[l].reshape(1, d),
                       w_cq[l].astype(BF16), kc, vc, w_co[l].astype(BF16),
                       ffn_norm_g[l].reshape(1, d), seq=t)
        last = l == depth - 1
        fg = final_norm_g if last else jnp.ones_like(final_norm_g)
        out = _conv_ffn(xn3, h2, w_up[l].astype(BF16), conv_w[l], conv_b[l].reshape(1, -1),
                        w_down[l].astype(BF16), fg.reshape(1, d), seq=t)
        h = out
    return out.reshape(b, t, d)
```

```python
import functools
import math

import jax
import jax.numpy as jnp
from jax import lax
from jax.experimental import pallas as pl
from jax.experimental.pallas import tpu as pltpu

HEAD_DIM = 128
MOBA_BLOCK = 256
MOBA_TOPK = 3
ROPE_THETA = 10000.0
N_XATTN_HEADS = 4
CONV_WIDTH = 3
RMS_EPS = 1e-6
NEG = -1e30
SCALE = 1.0 / math.sqrt(HEAD_DIM)
LOG2E = math.log2(math.e)
QSCALE = SCALE * LOG2E

LANES = 128
SUBLANES = 8
VMEM_LIMIT_BYTES = 56 * 1024 * 1024

F32 = jnp.float32
BF16 = jnp.bfloat16

_NT = (((1,), (1,)), ((), ()))


def _cparams(*sem):
    return pltpu.CompilerParams(dimension_semantics=sem, vmem_limit_bytes=VMEM_LIMIT_BYTES)


def _rmsnorm(x, g):
    ms = jnp.mean(x * x, axis=-1, keepdims=True)
    return x * lax.rsqrt(ms + RMS_EPS) * g


def _split3(x):
    hi = x.astype(BF16)
    r = x - hi.astype(F32)
    mid = r.astype(BF16)
    lo = (r - mid.astype(F32)).astype(BF16)
    return hi, mid, lo


def _in_proj_kernel(xn_ref, w_ref, cos_ref, sin_ref, o_ref, w_sc, *, group_of_tile):
    j = pl.program_id(0)

    @pl.when(pl.program_id(1) == 0)
    def _():
        w_sc[...] = w_ref[...].astype(BF16)

    y = jnp.dot(xn_ref[...], w_sc[...], preferred_element_type=F32)
    n_heads_tile = y.shape[1] // HEAD_DIM

    def rope(scale):
        cos = cos_ref[...]
        sin = sin_ref[...]
        for h in range(n_heads_tile):
            sl = slice(h * HEAD_DIM, (h + 1) * HEAD_DIM)
            yh = y[:, sl]
            r = yh * cos + pltpu.roll(yh, HEAD_DIM // 2, 1) * sin
            if scale != 1.0:
                r = r * scale
            o_ref[:, sl] = r.astype(o_ref.dtype)

    grp = group_of_tile(j)

    @pl.when(grp == 0)
    def _():
        rope(QSCALE)

    @pl.when(grp == 1)
    def _():
        rope(1.0)

    @pl.when(grp == 3)
    def _():
        o_ref[...] = (y * QSCALE).astype(o_ref.dtype)

    @pl.when((grp == 2) | (grp >= 4))
    def _():
        o_ref[...] = y.astype(o_ref.dtype)


def _in_proj(xn, w_in, cos, sin, *, seq, group_width):
    n, d = xn.shape
    cols = 6 * group_width
    tm = min(1024, seq)
    tn = min(1024, group_width)
    tiles_per_group = group_width // tn
    t_tiles = seq // tm
    kern = functools.partial(_in_proj_kernel, group_of_tile=lambda j: j // tiles_per_group)
    return pl.pallas_call(
        kern,
        grid=(cols // tn, n // tm),
        in_specs=[
            pl.BlockSpec((tm, d), lambda j, i: (i, 0)),
            pl.BlockSpec((d, tn), lambda j, i: (0, j)),
            pl.BlockSpec((tm, HEAD_DIM), lambda j, i: (i % t_tiles, 0)),
            pl.BlockSpec((tm, HEAD_DIM), lambda j, i: (i % t_tiles, 0)),
        ],
        out_specs=pl.BlockSpec((tm, tn), lambda j, i: (i, j)),
        out_shape=jax.ShapeDtypeStruct((n, cols), BF16),
        scratch_shapes=[pltpu.VMEM((d, tn), BF16)],
        compiler_params=_cparams("arbitrary", "arbitrary"),
        name="in_proj",
    )(xn, w_in, cos, sin)


def _norm_gate_kernel(x_ref, g_ref, wz_ref, bf_ref, route_ref, xn_ref, qx_ref, kx_ref, carry_sc,
                      *, n_heads):
    t = pl.program_id(1)

    @pl.when(t == 0)
    def _():
        carry_sc[...] = jnp.zeros_like(carry_sc)

    xn = _rmsnorm(x_ref[...], g_ref[...]).astype(BF16)
    xn_ref[...] = xn
    z = jnp.dot(xn, wz_ref[...], preferred_element_type=F32) + bf_ref[...]
    lf = jnp.minimum(z, 0.0) - jnp.log(1.0 + jnp.exp(-jnp.abs(z)))

    tc = lf.shape[0]
    row = lax.broadcasted_iota(jnp.int32, (tc, tc), 0)
    col = lax.broadcasted_iota(jnp.int32, (tc, tc), 1)
    tri = (col <= row).astype(BF16)
    c = carry_sc[0:1, :]
    for part in _split3(lf):
        c = c + jnp.dot(tri, part, preferred_element_type=F32)
    carry_sc[0:1, :] = c[tc - 1:tc, :]

    parts = jnp.concatenate(_split3(c * LOG2E), axis=1)
    routed = jnp.dot(parts, route_ref[...], preferred_element_type=F32)
    lane = lax.broadcasted_iota(jnp.int32, (tc, LANES), 1)
    q_ones = ((lane >= 3) & (lane < 6)).astype(F32)
    k_ones = (lane < 3).astype(F32)
    for h in range(n_heads):
        qx_ref[h] = (routed[:, (2 * h) * LANES:(2 * h + 1) * LANES] + q_ones).astype(BF16)
        kx_ref[h] = (routed[:, (2 * h + 1) * LANES:(2 * h + 2) * LANES] + k_ones).astype(BF16)


def _gate_routing(n_heads):
    import numpy as np
    r = np.zeros((3 * LANES, n_heads * 2 * LANES), np.float32)
    for h in range(n_heads):
        for p in range(3):
            r[p * LANES + h, (2 * h) * LANES + p] = 1.0
            r[p * LANES + h, (2 * h + 1) * LANES + 3 + p] = -1.0
    return jnp.asarray(r, BF16)


def _norm_gate(x3, g, w_z, b_f, n_heads):
    b, t, d = x3.shape
    tc = min(512, t)
    ext = jax.ShapeDtypeStruct((b, n_heads, t, LANES), BF16)
    ext_spec = pl.BlockSpec((None, n_heads, tc, LANES), lambda bi, ti: (bi, 0, ti, 0))
    const = lambda bi, ti: (0, 0)
    return pl.pallas_call(
        functools.partial(_norm_gate_kernel, n_heads=n_heads),
        grid=(b, t // tc),
        in_specs=[
            pl.BlockSpec((None, tc, d), lambda bi, ti: (bi, ti, 0)),
            pl.BlockSpec((1, d), const),
            pl.BlockSpec((d, LANES), const),
            pl.BlockSpec((1, LANES), const),
            pl.BlockSpec((3 * LANES, n_heads * 2 * LANES), const),
        ],
        out_specs=[pl.BlockSpec((None, tc, d), lambda bi, ti: (bi, ti, 0)), ext_spec, ext_spec],
        out_shape=[jax.ShapeDtypeStruct((b, t, d), BF16), ext, ext],
        scratch_shapes=[pltpu.VMEM((SUBLANES, LANES), F32)],
        compiler_params=_cparams("arbitrary", "arbitrary"),
        name="norm_gate",
    )(x3, g, w_z, b_f, _gate_routing(n_heads))


ATT_TILE = 512
ATT_HEADS = 4
VT_ROWS = HEAD_DIM + 16


def _flash_init(s, v):
    m = jnp.max(s, axis=1, keepdims=True)
    p = jnp.exp(s - m)
    l = jnp.sum(p, axis=1, keepdims=True)
    acc = jnp.dot(p.astype(BF16), v, preferred_element_type=F32)
    return m, l, acc


def _moba_bias(q, kmean, iq):
    tq = q.shape[0]
    nb = kmean.shape[0]
    km_hi = kmean.astype(BF16)
    km_lo = (kmean - km_hi.astype(F32)).astype(BF16)
    gate = (lax.dot_general(km_hi, q, _NT, preferred_element_type=F32)
            + lax.dot_general(km_lo, q, _NT, preferred_element_type=F32))
    blk_i = lax.broadcasted_iota(jnp.int32, gate.shape, 0)
    qry_i = lax.broadcasted_iota(jnp.int32, gate.shape, 1)
    own = iq * (tq // MOBA_BLOCK) + qry_i // MOBA_BLOCK
    blk_f = blk_i.astype(F32)
    past = blk_i < own
    g = jnp.where(past, gate, -jnp.inf)
    sel = jnp.zeros(gate.shape, jnp.bool_)
    for _ in range(MOBA_TOPK):
        mx = jnp.max(g, axis=0, keepdims=True)
        idx = jnp.min(jnp.where(g == mx, blk_f, float(LANES)), axis=0, keepdims=True)
        pick = blk_f == idx
        sel = sel | pick
        g = jnp.where(pick, -jnp.inf, g)
    bias_t = jnp.where((sel & past) | (blk_i == own), 0.0, NEG)
    bias_t = jnp.concatenate([bias_t, jnp.zeros((LANES - nb, tq), F32)], axis=0)
    return bias_t.T.astype(BF16)


def _attn_kernel(*refs, moba):
    if moba:
        q_ref, k_ref, v_ref, o_ref = refs[:4]
        kmean_sc = refs[-1]
        refs = refs[:-1]
    else:
        q_ref, k_ref, v_ref, qx_ref, kx_ref, o_ref = refs[:6]
    kaug_sc, vt_sc, sa_sc, sb_sc, mxa_sc, mxb_sc, m_sc, acc_sc = refs[-8:]
    iq = pl.program_id(2)
    tq = q_ref.shape[0]
    t = k_ref.shape[0]
    n_heads = q_ref.shape[1] // HEAD_DIM
    heads = [slice(g * HEAD_DIM, (g + 1) * HEAD_DIM) for g in range(n_heads)]

    @pl.when(iq == 0)
    def _():
        for g, hs in enumerate(heads):
            kaug_sc[g, :, :HEAD_DIM] = k_ref[:, hs]
            if moba:
                row = lax.broadcasted_iota(jnp.int32, (t, LANES), 0)
                lane = lax.broadcasted_iota(jnp.int32, (t, LANES), 1)
                kaug_sc[g, :, HEAD_DIM:] = ((row // MOBA_BLOCK) == lane).astype(BF16)
                kmean_sc[g] = jnp.zeros((LANES, HEAD_DIM), F32)
                for n in range(t // MOBA_BLOCK):
                    kb = k_ref[n * MOBA_BLOCK:(n + 1) * MOBA_BLOCK, hs].astype(F32)
                    kmean_sc[g, n:n + 1, :] = jnp.mean(kb, axis=0, keepdims=True)
            else:
                kaug_sc[g, :, HEAD_DIM:] = kx_ref[g]
            pad_row = lax.broadcasted_iota(jnp.int32, (VT_ROWS - HEAD_DIM, tq), 0)
            for j in range(t // tq):
                vt_sc[g, j, :HEAD_DIM, :] = (
                    v_ref[j * tq:(j + 1) * tq, hs].astype(F32).T.astype(BF16))
                vt_sc[g, j, HEAD_DIM:, :] = (pad_row == 0).astype(BF16)

    qa = []
    for g, hs in enumerate(heads):
        q = q_ref[:, hs]
        if moba:
            nb = -(-(t // MOBA_BLOCK) // 16) * 16
            extra = _moba_bias(q, kmean_sc[g, :nb, :], iq)
        else:
            extra = qx_ref[g]
        qa.append(jnp.concatenate([q, extra], axis=1))

    def scores(g, j):
        off = pl.multiple_of(j * tq, tq)
        return lax.dot_general(kaug_sc[g, pl.ds(off, tq), :], qa[g], _NT,
                               preferred_element_type=F32)

    half = tq // 2
    halves = (slice(0, half), slice(half, tq))

    def produce(j, s_ref, mx_ref):
        for g in range(n_heads):
            s = scores(g, j)
            s_ref[g] = s
            mx_ref[g] = jnp.max(s, axis=0, keepdims=True)

    def consume(j, s_ref, mx_ref, diagonal=False):
        for g in range(n_heads):
            m = m_sc[g]
            m_new = jnp.maximum(m, mx_ref[g])
            alpha = jnp.exp2(m - m_new)
            m_sc[g] = m_new
            for qi, qs in enumerate(halves):
                pv = alpha[:, qs] * acc_sc[g, :, qs]
                for ki, ks in enumerate(halves):
                    if diagonal and ki > qi:
                        continue
                    p = jnp.exp2(s_ref[g, ks, qs] - m_new[:, qs])
                    pv = pv + jnp.dot(vt_sc[g, j, :, ks], p.astype(BF16),
                                      preferred_element_type=F32)
                acc_sc[g, :, qs] = pv

    for g in range(n_heads):
        m_sc[g] = jnp.full((1, tq), NEG, F32)
        acc_sc[g] = jnp.zeros((VT_ROWS, tq), F32)
    produce(0, sa_sc, mxa_sc)

    def body(jj, carry):
        j = 2 * jj
        produce(j + 1, sb_sc, mxb_sc)
        consume(j, sa_sc, mxa_sc)
        produce(j + 2, sa_sc, mxa_sc)
        consume(j + 1, sb_sc, mxb_sc)
        return carry

    lax.fori_loop(0, iq // 2, body, 0)

    def finish(s_ref, mx_ref):
        key = lax.broadcasted_iota(jnp.int32, (tq, tq), 0)
        qry = lax.broadcasted_iota(jnp.int32, (tq, tq), 1)
        for g in range(n_heads):
            s = jnp.where(key <= qry, s_ref[g], NEG)
            s_ref[g] = s
            mx_ref[g] = jnp.max(s, axis=0, keepdims=True)
        consume(iq, s_ref, mx_ref, diagonal=True)
        for g, hs in enumerate(heads):
            out = acc_sc[g, :HEAD_DIM, :] / acc_sc[g, HEAD_DIM:HEAD_DIM + 1, :]
            o_ref[:, hs] = out.T.astype(o_ref.dtype)

    @pl.when(iq % 2 == 0)
    def _():
        finish(sa_sc, mxa_sc)

    @pl.when(iq % 2 == 1)
    def _():
        produce(iq, sb_sc, mxb_sc)
        consume(iq - 1, sa_sc, mxa_sc)
        finish(sb_sc, mxb_sc)


def _attn(proj3, n_heads, head0, fox_extras=None):
    b, t, _ = proj3.shape
    moba = fox_extras is None
    tq = min(ATT_TILE, t)
    g = min(ATT_HEADS, n_heads)
    assert t % tq == 0 and tq % MOBA_BLOCK == 0 and t // MOBA_BLOCK <= LANES and n_heads % g == 0
    w = g * HEAD_DIM
    c0 = head0 // g
    in_specs = [
        pl.BlockSpec((None, tq, w), lambda bi, h, i: (bi, i, c0 + h)),
        pl.BlockSpec((None, t, w), lambda bi, h, i: (bi, 0, c0 + n_heads // g + h)),
        pl.BlockSpec((None, t, w), lambda bi, h, i: (bi, 0, c0 + 2 * (n_heads // g) + h)),
    ]
    args = [proj3, proj3, proj3]
    scratch = [
        pltpu.VMEM((g, t, 2 * HEAD_DIM), BF16),
        pltpu.VMEM((g, t // tq, VT_ROWS, tq), BF16),
        pltpu.VMEM((g, tq, tq), F32),
        pltpu.VMEM((g, tq, tq), F32),
        pltpu.VMEM((g, 1, tq), F32),
        pltpu.VMEM((g, 1, tq), F32),
        pltpu.VMEM((g, 1, tq), F32),
        pltpu.VMEM((g, VT_ROWS, tq), F32),
    ]
    if moba:
        scratch.append(pltpu.VMEM((g, LANES, HEAD_DIM), F32))
    else:
        in_specs += [
            pl.BlockSpec((None, g, tq, LANES), lambda bi, h, i: (bi, h, i, 0)),
            pl.BlockSpec((None, g, t, LANES), lambda bi, h, i: (bi, h, 0, 0)),
        ]
        args += list(fox_extras)
    return pl.pallas_call(
        functools.partial(_attn_kernel, moba=moba),
        grid=(b, n_heads // g, t // tq),
        in_specs=in_specs,
        out_specs=pl.BlockSpec((None, tq, w), lambda bi, h, i: (bi, i, h)),
        out_shape=jax.ShapeDtypeStruct((b, t, n_heads * HEAD_DIM), BF16),
        scratch_shapes=scratch,
        compiler_params=_cparams("arbitrary", "arbitrary", "arbitrary"),
        name="moba_attn" if moba else "fox_attn",
    )(*args)


def _mem_kv_kernel(mem_ref, g_ref, w_ref, k_ref, v_ref):
    xn = _rmsnorm(mem_ref[...], g_ref[...]).astype(BF16)
    kv = jnp.dot(xn, w_ref[...], preferred_element_type=F32)
    half = kv.shape[1] // 2
    k_ref[...] = kv[:, :half].astype(k_ref.dtype)
    v_ref[...] = kv[:, half:].astype(v_ref.dtype)


def _mem_kv(mem, g, w_ckv):
    b, nm, d = mem.shape
    xd = w_ckv.shape[1] // 2
    out = jax.ShapeDtypeStruct((b, nm, xd), BF16)
    o_spec = pl.BlockSpec((None, nm, xd), lambda bi: (bi, 0, 0))
    return pl.pallas_call(
        _mem_kv_kernel,
        grid=(b,),
        in_specs=[
            pl.BlockSpec((None, nm, d), lambda bi: (bi, 0, 0)),
            pl.BlockSpec((1, d), lambda bi: (0, 0)),
            pl.BlockSpec((d, 2 * xd), lambda bi: (0, 0)),
        ],
        out_specs=[o_spec, o_spec],
        out_shape=[out, out],
        compiler_params=_cparams("arbitrary"),
        name="mem_kv",
    )(mem, g, w_ckv)


def _mid_kernel(x_ref, om_ref, of_ref, wom_ref, wof_ref, gx_ref, wcq_ref, kc_ref, vc_ref,
                wco_ref, gf_ref, h_ref, xn_ref):
    h1 = (x_ref[...]
          + jnp.dot(om_ref[...], wom_ref[...], preferred_element_type=F32)
          + jnp.dot(of_ref[...], wof_ref[...], preferred_element_type=F32))
    xn2 = _rmsnorm(h1, gx_ref[...]).astype(BF16)
    q = (jnp.dot(xn2, wcq_ref[...], preferred_element_type=F32) * SCALE).astype(BF16)
    outs = []
    for h in range(N_XATTN_HEADS):
        sl = slice(h * HEAD_DIM, (h + 1) * HEAD_DIM)
        s = lax.dot_general(q[:, sl], kc_ref[:, sl], _NT, preferred_element_type=F32)
        m, l, acc = _flash_init(s, vc_ref[:, sl])
        outs.append((acc / l).astype(BF16))
    oc = jnp.concatenate(outs, axis=1)
    h2 = h1 + jnp.dot(oc, wco_ref[...], preferred_element_type=F32)
    h_ref[...] = h2
    xn_ref[...] = _rmsnorm(h2, gf_ref[...]).astype(xn_ref.dtype)


def _mid(x2, o_moba, o_fox, w_om, w_of, gx, w_cq, kc, vc, w_co, gf, *, seq):
    n, d = x2.shape
    wm, wf = o_moba.shape[1], o_fox.shape[1]
    nm, xd = kc.shape[1], kc.shape[2]
    tm = min(512, seq)
    t_tiles = seq // tm
    const = lambda i: (0, 0)
    return pl.pallas_call(
        _mid_kernel,
        grid=(n // tm,),
        in_specs=[
            pl.BlockSpec((tm, d), lambda i: (i, 0)),
            pl.BlockSpec((tm, wm), lambda i: (i, 0)),
            pl.BlockSpec((tm, wf), lambda i: (i, 0)),
            pl.BlockSpec((wm, d), const),
            pl.BlockSpec((wf, d), const),
            pl.BlockSpec((1, d), const),
            pl.BlockSpec((d, xd), const),
            pl.BlockSpec((None, nm, xd), lambda i: (i // t_tiles, 0, 0)),
            pl.BlockSpec((None, nm, xd), lambda i: (i // t_tiles, 0, 0)),
            pl.BlockSpec((xd, d), const),
            pl.BlockSpec((1, d), const),
        ],
        out_specs=[
            pl.BlockSpec((tm, d), lambda i: (i, 0)),
            pl.BlockSpec((tm, d), lambda i: (i, 0)),
        ],
        out_shape=[
            jax.ShapeDtypeStruct((n, d), F32),
            jax.ShapeDtypeStruct((n, d), BF16),
        ],
        compiler_params=_cparams("arbitrary"),
        name="mid",
    )(x2, o_moba, o_fox, w_om, w_of, gx, w_cq, kc, vc, w_co, gf)


def _conv_ffn_kernel(xn_ref, h_ref, wg_ref, wu_ref, cwg_ref, cwu_ref, cbg_ref, cbu_ref,
                     wd_ref, fg_ref, o_ref, hs_sc, tail_sc, *, tiles_per_seq):
    i = pl.program_id(0)
    f = pl.program_id(1)
    nf = pl.num_programs(1)
    tm = xn_ref.shape[0]
    tf = wg_ref.shape[1]
    halo = SUBLANES

    xn = xn_ref[...]
    first = (i % tiles_per_seq) == 0

    @pl.when(first)
    def _():
        hs_sc[0:halo, :] = jnp.zeros((halo, 2 * tf), F32)

    @pl.when(jnp.logical_not(first))
    def _():
        hs_sc[0:halo, :] = tail_sc[f]

    hs_sc[halo:, :tf] = jnp.dot(xn, wg_ref[...], preferred_element_type=F32)
    hs_sc[halo:, tf:] = jnp.dot(xn, wu_ref[...], preferred_element_type=F32)
    tail_sc[f] = hs_sc[tm:tm + halo, :]

    cw = jnp.concatenate([cwg_ref[...], cwu_ref[...]], axis=1)
    cb = jnp.concatenate([cbg_ref[...], cbu_ref[...]], axis=1)
    y = cb
    for k in range(CONV_WIDTH):
        lo = halo - (CONV_WIDTH - 1) + k
        y = y + hs_sc[lo:lo + tm, :] * cw[k:k + 1, :]
    gte = y[:, :tf]
    act = (gte * (1.0 / (1.0 + jnp.exp2(gte * (-LOG2E)))) * y[:, tf:]).astype(BF16)
    part = jnp.dot(act, wd_ref[...], preferred_element_type=F32)

    @pl.when(f == 0)
    def _():
        o_ref[...] = h_ref[...] + part

    @pl.when(f > 0)
    def _():
        o_ref[...] += part

    @pl.when(f == nf - 1)
    def _():
        o_ref[...] = _rmsnorm(o_ref[...], fg_ref[...])


def _conv_ffn(xn3, h2, w_up, conv_w, conv_b, w_down, fg, *, seq):
    n, d = h2.shape
    ff = w_down.shape[0]
    tm = min(512, seq)
    tf = 512
    assert ff % tf == 0
    nf = ff // tf
    return pl.pallas_call(
        functools.partial(_conv_ffn_kernel, tiles_per_seq=seq // tm),
        grid=(n // tm, nf),
        in_specs=[
            pl.BlockSpec((tm, d), lambda i, f: (i, 0)),
            pl.BlockSpec((tm, d), lambda i, f: (i, 0)),
            pl.BlockSpec((d, tf), lambda i, f: (0, f)),
            pl.BlockSpec((d, tf), lambda i, f: (0, f + nf)),
            pl.BlockSpec((CONV_WIDTH, tf), lambda i, f: (0, f)),
            pl.BlockSpec((CONV_WIDTH, tf), lambda i, f: (0, f + nf)),
            pl.BlockSpec((1, tf), lambda i, f: (0, f)),
            pl.BlockSpec((1, tf), lambda i, f: (0, f + nf)),
            pl.BlockSpec((tf, d), lambda i, f: (f, 0)),
            pl.BlockSpec((1, d), lambda i, f: (0, 0)),
        ],
        out_specs=pl.BlockSpec((tm, d), lambda i, f: (i, 0)),
        out_shape=jax.ShapeDtypeStruct((n, d), F32),
        scratch_shapes=[
            pltpu.VMEM((tm + SUBLANES, 2 * tf), F32),
            pltpu.VMEM((nf, SUBLANES, 2 * tf), F32),
        ],
        compiler_params=_cparams("arbitrary", "arbitrary"),
        name="conv_ffn",
    )(xn3, h2, w_up, w_up, conv_w, conv_w, conv_b, conv_b, w_down, fg)


def _rope_tables(t):
    half = HEAD_DIM // 2
    inv = ROPE_THETA ** (-jnp.arange(half, dtype=F32) / half)
    ang = jnp.arange(t, dtype=F32)[:, None] * inv[None, :]
    cos, sin = jnp.cos(ang), jnp.sin(ang)
    return jnp.concatenate([cos, cos], axis=1), jnp.concatenate([-sin, sin], axis=1)


def kernel(x, mem, attn_norm_g, w_in, b_f, w_o, xattn_norm_g, mem_norm_g, w_cq, w_ckv, w_co,
           ffn_norm_g, w_up, conv_w, conv_b, w_down, final_norm_g):
    b, t, d = x.shape
    assert w_in.shape[0] == 1, "one layer: the final rmsnorm is fused into the ffn kernel"
    n_heads = d // HEAD_DIM
    n_moba = n_heads // 2
    n_fox = n_heads - n_moba
    assert n_moba == n_fox and n_fox <= LANES
    wgrp = n_moba * HEAD_DIM
    main_cols = 6 * wgrp
    cos, sin = _rope_tables(t)
    row = lambda v: v.reshape(1, -1)

    w_z = jnp.pad(w_in[0, :, main_cols:], ((0, 0), (0, LANES - n_fox))).astype(BF16)
    bf = row(jnp.pad(b_f[0].astype(F32), (0, LANES - n_fox)))
    xn, qx, kx = _norm_gate(x, row(attn_norm_g[0]), w_z, bf, n_fox)
    proj = _in_proj(xn.reshape(b * t, d), w_in[0], cos, sin, seq=t, group_width=wgrp)
    proj3 = proj.reshape(b, t, main_cols)
    o_moba = _attn(proj3, n_moba, 0)
    o_fox = _attn(proj3, n_fox, 3 * n_moba, (qx, kx))
    kc, vc = _mem_kv(mem, row(mem_norm_g[0]), w_ckv[0].astype(BF16))
    wo = w_o[0].astype(BF16)
    h2, xn3 = _mid(x.reshape(b * t, d), o_moba.reshape(b * t, wgrp), o_fox.reshape(b * t, wgrp),
                   wo[:wgrp], wo[wgrp:], row(xattn_norm_g[0]), w_cq[0].astype(BF16), kc, vc,
                   w_co[0].astype(BF16), row(ffn_norm_g[0]), seq=t)
    out = _conv_ffn(xn3, h2, w_up[0].astype(BF16), conv_w[0], row(conv_b[0]),
                    w_down[0].astype(BF16), row(final_norm_g), seq=t)
    return out.reshape(b, t, d)
```

```python
import functools
import math

import jax
import jax.numpy as jnp
from jax import lax
from jax.experimental import pallas as pl
from jax.experimental.pallas import tpu as pltpu

HEAD_DIM = 128
MOBA_BLOCK = 256
MOBA_TOPK = 3
ROPE_THETA = 10000.0
N_XATTN_HEADS = 4
CONV_WIDTH = 3
RMS_EPS = 1e-6
NEG = -1e30
SCALE = 1.0 / math.sqrt(HEAD_DIM)
LOG2E = math.log2(math.e)
QSCALE = SCALE * LOG2E

LANES = 128
SUBLANES = 8
VMEM_LIMIT_BYTES = 56 * 1024 * 1024

F32 = jnp.float32
BF16 = jnp.bfloat16

_NT = (((1,), (1,)), ((), ()))


def _cparams(*sem):
    return pltpu.CompilerParams(dimension_semantics=sem, vmem_limit_bytes=VMEM_LIMIT_BYTES)


def _rmsnorm(x, g):
    ms = jnp.mean(x * x, axis=-1, keepdims=True)
    return x * lax.rsqrt(ms + RMS_EPS) * g


def _split3(x):
    hi = x.astype(BF16)
    r = x - hi.astype(F32)
    mid = r.astype(BF16)
    lo = (r - mid.astype(F32)).astype(BF16)
    return hi, mid, lo


def _in_proj_kernel(xn_ref, w_ref, cos_ref, sin_ref, o_ref, w_sc, *, group_of_tile):
    j = pl.program_id(0)

    @pl.when(pl.program_id(1) == 0)
    def _():
        w_sc[...] = w_ref[...].astype(BF16)

    y = lax.dot_general(xn_ref[...], w_sc[...], _NT, preferred_element_type=F32)
    n_heads_tile = y.shape[1] // HEAD_DIM

    def rope(scale):
        cos = cos_ref[...]
        sin = sin_ref[...]
        for h in range(n_heads_tile):
            sl = slice(h * HEAD_DIM, (h + 1) * HEAD_DIM)
            yh = y[:, sl]
            r = yh * cos + pltpu.roll(yh, HEAD_DIM // 2, 1) * sin
            if scale != 1.0:
                r = r * scale
            o_ref[:, sl] = r.astype(o_ref.dtype)

    grp = group_of_tile(j)

    @pl.when(grp == 0)
    def _():
        rope(QSCALE)

    @pl.when(grp == 1)
    def _():
        rope(1.0)

    @pl.when(grp == 3)
    def _():
        o_ref[...] = (y * QSCALE).astype(o_ref.dtype)

    @pl.when((grp == 2) | (grp >= 4))
    def _():
        o_ref[...] = y.astype(o_ref.dtype)


def _in_proj(xn, w_in_t, cos, sin, *, seq, group_width):
    n, d = xn.shape
    cols = 6 * group_width
    tm = min(1024, seq)
    tn = min(1024, group_width)
    tiles_per_group = group_width // tn
    t_tiles = seq // tm
    kern = functools.partial(_in_proj_kernel, group_of_tile=lambda j: j // tiles_per_group)
    return pl.pallas_call(
        kern,
        grid=(cols // tn, n // tm),
        in_specs=[
            pl.BlockSpec((tm, d), lambda j, i: (i, 0)),
            pl.BlockSpec((tn, d), lambda j, i: (j, 0)),
            pl.BlockSpec((tm, HEAD_DIM), lambda j, i: (i % t_tiles, 0)),
            pl.BlockSpec((tm, HEAD_DIM), lambda j, i: (i % t_tiles, 0)),
        ],
        out_specs=pl.BlockSpec((tm, tn), lambda j, i: (i, j)),
        out_shape=jax.ShapeDtypeStruct((n, cols), BF16),
        scratch_shapes=[pltpu.VMEM((tn, d), BF16)],
        compiler_params=_cparams("arbitrary", "arbitrary"),
        name="in_proj",
    )(xn, w_in_t, cos, sin)


def _norm_gate_kernel(x_ref, g_ref, wz_ref, bf_ref, route_ref, xn_ref, qx_ref, kx_ref, carry_sc,
                      *, n_heads):
    t = pl.program_id(1)

    @pl.when(t == 0)
    def _():
        carry_sc[...] = jnp.zeros_like(carry_sc)

    xn = _rmsnorm(x_ref[...], g_ref[...]).astype(BF16)
    xn_ref[...] = xn
    z = lax.dot_general(xn, wz_ref[...], _NT, preferred_element_type=F32) + bf_ref[...]
    lf = jnp.minimum(z, 0.0) - jnp.log(1.0 + jnp.exp(-jnp.abs(z)))

    tc = lf.shape[0]
    row = lax.broadcasted_iota(jnp.int32, (tc, tc), 0)
    col = lax.broadcasted_iota(jnp.int32, (tc, tc), 1)
    tri = (col <= row).astype(BF16)
    c = carry_sc[0:1, :]
    for part in _split3(lf):
        c = c + jnp.dot(tri, part, preferred_element_type=F32)
    carry_sc[0:1, :] = c[tc - 1:tc, :]

    parts = jnp.concatenate(_split3(c * LOG2E), axis=1)
    routed = jnp.dot(parts, route_ref[...], preferred_element_type=F32)
    lane = lax.broadcasted_iota(jnp.int32, (tc, LANES), 1)
    q_ones = ((lane >= 3) & (lane < 6)).astype(F32)
    k_ones = (lane < 3).astype(F32)
    for h in range(n_heads):
        qx_ref[h] = (routed[:, (2 * h) * LANES:(2 * h + 1) * LANES] + q_ones).astype(BF16)
        kx_ref[h] = (routed[:, (2 * h + 1) * LANES:(2 * h + 2) * LANES] + k_ones).astype(BF16)


def _gate_routing(n_heads):
    import numpy as np
    r = np.zeros((3 * LANES, n_heads * 2 * LANES), np.float32)
    for h in range(n_heads):
        for p in range(3):
            r[p * LANES + h, (2 * h) * LANES + p] = 1.0
            r[p * LANES + h, (2 * h + 1) * LANES + 3 + p] = -1.0
    return jnp.asarray(r, BF16)


def _norm_gate(x3, g, w_z, b_f, n_heads):
    b, t, d = x3.shape
    tc = min(512, t)
    ext = jax.ShapeDtypeStruct((b, n_heads, t, LANES), BF16)
    ext_spec = pl.BlockSpec((None, n_heads, tc, LANES), lambda bi, ti: (bi, 0, ti, 0))
    const = lambda bi, ti: (0, 0)
    return pl.pallas_call(
        functools.partial(_norm_gate_kernel, n_heads=n_heads),
        grid=(b, t // tc),
        in_specs=[
            pl.BlockSpec((None, tc, d), lambda bi, ti: (bi, ti, 0)),
            pl.BlockSpec((1, d), const),
            pl.BlockSpec((LANES, d), const),
            pl.BlockSpec((1, LANES), const),
            pl.BlockSpec((3 * LANES, n_heads * 2 * LANES), const),
        ],
        out_specs=[pl.BlockSpec((None, tc, d), lambda bi, ti: (bi, ti, 0)), ext_spec, ext_spec],
        out_shape=[jax.ShapeDtypeStruct((b, t, d), BF16), ext, ext],
        scratch_shapes=[pltpu.VMEM((SUBLANES, LANES), F32)],
        compiler_params=_cparams("arbitrary", "arbitrary"),
        name="norm_gate",
    )(x3, g, w_z, b_f, _gate_routing(n_heads))


ATT_TILE = 512
ATT_HEADS = 4
VT_ROWS = HEAD_DIM + 16


def _flash_init(s, v):
    m = jnp.max(s, axis=1, keepdims=True)
    p = jnp.exp(s - m)
    l = jnp.sum(p, axis=1, keepdims=True)
    acc = jnp.dot(p.astype(BF16), v, preferred_element_type=F32)
    return m, l, acc


def _moba_bias(q, kmean, iq):
    tq = q.shape[0]
    nb = kmean.shape[0]
    km_hi = kmean.astype(BF16)
    km_lo = (kmean - km_hi.astype(F32)).astype(BF16)
    gate = (lax.dot_general(km_hi, q, _NT, preferred_element_type=F32)
            + lax.dot_general(km_lo, q, _NT, preferred_element_type=F32))
    blk_i = lax.broadcasted_iota(jnp.int32, gate.shape, 0)
    qry_i = lax.broadcasted_iota(jnp.int32, gate.shape, 1)
    own = iq * (tq // MOBA_BLOCK) + qry_i // MOBA_BLOCK
    blk_f = blk_i.astype(F32)
    past = blk_i < own
    g = jnp.where(past, gate, -jnp.inf)
    sel = jnp.zeros(gate.shape, jnp.bool_)
    for _ in range(MOBA_TOPK):
        mx = jnp.max(g, axis=0, keepdims=True)
        idx = jnp.min(jnp.where(g == mx, blk_f, float(LANES)), axis=0, keepdims=True)
        pick = blk_f == idx
        sel = sel | pick
        g = jnp.where(pick, -jnp.inf, g)
    bias_t = jnp.where((sel & past) | (blk_i == own), 0.0, NEG)
    bias_t = jnp.concatenate([bias_t, jnp.zeros((LANES - nb, tq), F32)], axis=0)
    return bias_t.T.astype(BF16)


def _attn_kernel(*refs, moba):
    if moba:
        q_ref, k_ref, v_ref, o_ref = refs[:4]
        kmean_sc = refs[-1]
        refs = refs[:-1]
    else:
        q_ref, k_ref, v_ref, qx_ref, kx_ref, o_ref = refs[:6]
    kaug_sc, vt_sc, sa_sc, sb_sc, mxa_sc, mxb_sc, m_sc, acc_sc = refs[-8:]
    iq = pl.program_id(2)
    tq = q_ref.shape[0]
    t = k_ref.shape[0]
    n_heads = q_ref.shape[1] // HEAD_DIM
    heads = [slice(g * HEAD_DIM, (g + 1) * HEAD_DIM) for g in range(n_heads)]

    @pl.when(iq == 0)
    def _():
        for g, hs in enumerate(heads):
            kaug_sc[g, :, :HEAD_DIM] = k_ref[:, hs]
            if moba:
                row = lax.broadcasted_iota(jnp.int32, (t, LANES), 0)
                lane = lax.broadcasted_iota(jnp.int32, (t, LANES), 1)
                kaug_sc[g, :, HEAD_DIM:] = ((row // MOBA_BLOCK) == lane).astype(BF16)
                kmean_sc[g] = jnp.zeros((LANES, HEAD_DIM), F32)
                for n in range(t // MOBA_BLOCK):
                    kb = k_ref[n * MOBA_BLOCK:(n + 1) * MOBA_BLOCK, hs].astype(F32)
                    kmean_sc[g, n:n + 1, :] = jnp.mean(kb, axis=0, keepdims=True)
            else:
                kaug_sc[g, :, HEAD_DIM:] = kx_ref[g]
            pad_row = lax.broadcasted_iota(jnp.int32, (VT_ROWS - HEAD_DIM, tq), 0)
            for j in range(t // tq):
                vt_sc[g, j, :HEAD_DIM, :] = (
                    v_ref[j * tq:(j + 1) * tq, hs].astype(F32).T.astype(BF16))
                vt_sc[g, j, HEAD_DIM:, :] = (pad_row == 0).astype(BF16)

    qa = []
    for g, hs in enumerate(heads):
        q = q_ref[:, hs]
        if moba:
            nb = -(-(t // MOBA_BLOCK) // 16) * 16
            extra = _moba_bias(q, kmean_sc[g, :nb, :], iq)
        else:
            extra = qx_ref[g]
        qa.append(jnp.concatenate([q, extra], axis=1))

    def scores(g, j):
        off = pl.multiple_of(j * tq, tq)
        return lax.dot_general(kaug_sc[g, pl.ds(off, tq), :], qa[g], _NT,
                               preferred_element_type=F32)

    half = tq // 2
    halves = (slice(0, half), slice(half, tq))

    def produce(j, s_ref, mx_ref):
        for g in range(n_heads):
            s = scores(g, j)
            s_ref[g] = s
            mx_ref[g] = jnp.max(s, axis=0, keepdims=True)

    def consume(j, s_ref, mx_ref, diagonal=False):
        for g in range(n_heads):
            m = m_sc[g]
            m_new = jnp.maximum(m, mx_ref[g])
            alpha = jnp.exp2(m - m_new)
            m_sc[g] = m_new
            for qi, qs in enumerate(halves):
                pv = alpha[:, qs] * acc_sc[g, :, qs]
                for ki, ks in enumerate(halves):
                    if diagonal and ki > qi:
                        continue
                    p = jnp.exp2(s_ref[g, ks, qs] - m_new[:, qs])
                    pv = pv + jnp.dot(vt_sc[g, j, :, ks], p.astype(BF16),
                                      preferred_element_type=F32)
                acc_sc[g, :, qs] = pv

    for g in range(n_heads):
        m_sc[g] = jnp.full((1, tq), NEG, F32)
        acc_sc[g] = jnp.zeros((VT_ROWS, tq), F32)
    produce(0, sa_sc, mxa_sc)

    def body(jj, carry):
        j = 2 * jj
        produce(j + 1, sb_sc, mxb_sc)
        consume(j, sa_sc, mxa_sc)
        produce(j + 2, sa_sc, mxa_sc)
        consume(j + 1, sb_sc, mxb_sc)
        return carry

    lax.fori_loop(0, iq // 2, body, 0)

    def finish(s_ref, mx_ref):
        key = lax.broadcasted_iota(jnp.int32, (tq, tq), 0)
        qry = lax.broadcasted_iota(jnp.int32, (tq, tq), 1)
        for g in range(n_heads):
            s = jnp.where(key <= qry, s_ref[g], NEG)
            s_ref[g] = s
            mx_ref[g] = jnp.max(s, axis=0, keepdims=True)
        consume(iq, s_ref, mx_ref, diagonal=True)
        for g, hs in enumerate(heads):
            out = acc_sc[g, :HEAD_DIM, :] / acc_sc[g, HEAD_DIM:HEAD_DIM + 1, :]
            o_ref[:, hs] = out.T.astype(o_ref.dtype)

    @pl.when(iq % 2 == 0)
    def _():
        finish(sa_sc, mxa_sc)

    @pl.when(iq % 2 == 1)
    def _():
        produce(iq, sb_sc, mxb_sc)
        consume(iq - 1, sa_sc, mxa_sc)
        finish(sb_sc, mxb_sc)


def _attn(proj3, n_heads, head0, fox_extras=None):
    b, t, _ = proj3.shape
    moba = fox_extras is None
    tq = min(ATT_TILE, t)
    g = min(ATT_HEADS, n_heads)
    assert t % tq == 0 and tq % MOBA_BLOCK == 0 and t // MOBA_BLOCK <= LANES and n_heads % g == 0
    w = g * HEAD_DIM
    c0 = head0 // g
    in_specs = [
        pl.BlockSpec((None, tq, w), lambda bi, h, i: (bi, i, c0 + h)),
        pl.BlockSpec((None, t, w), lambda bi, h, i: (bi, 0, c0 + n_heads // g + h)),
        pl.BlockSpec((None, t, w), lambda bi, h, i: (bi, 0, c0 + 2 * (n_heads // g) + h)),
    ]
    args = [proj3, proj3, proj3]
    scratch = [
        pltpu.VMEM((g, t, 2 * HEAD_DIM), BF16),
        pltpu.VMEM((g, t // tq, VT_ROWS, tq), BF16),
        pltpu.VMEM((g, tq, tq), F32),
        pltpu.VMEM((g, tq, tq), F32),
        pltpu.VMEM((g, 1, tq), F32),
        pltpu.VMEM((g, 1, tq), F32),
        pltpu.VMEM((g, 1, tq), F32),
        pltpu.VMEM((g, VT_ROWS, tq), F32),
    ]
    if moba:
        scratch.append(pltpu.VMEM((g, LANES, HEAD_DIM), F32))
    else:
        in_specs += [
            pl.BlockSpec((None, g, tq, LANES), lambda bi, h, i: (bi, h, i, 0)),
            pl.BlockSpec((None, g, t, LANES), lambda bi, h, i: (bi, h, 0, 0)),
        ]
        args += list(fox_extras)
    return pl.pallas_call(
        functools.partial(_attn_kernel, moba=moba),
        grid=(b, n_heads // g, t // tq),
        in_specs=in_specs,
        out_specs=pl.BlockSpec((None, tq, w), lambda bi, h, i: (bi, i, h)),
        out_shape=jax.ShapeDtypeStruct((b, t, n_heads * HEAD_DIM), BF16),
        scratch_shapes=scratch,
        compiler_params=_cparams("arbitrary", "arbitrary", "arbitrary"),
        name="moba_attn" if moba else "fox_attn",
    )(*args)


def _mem_kv_kernel(mem_ref, g_ref, w_ref, k_ref, v_ref):
    xn = _rmsnorm(mem_ref[...], g_ref[...]).astype(BF16)
    kv = jnp.dot(xn, w_ref[...], preferred_element_type=F32)
    half = kv.shape[1] // 2
    k_ref[...] = kv[:, :half].astype(k_ref.dtype)
    v_ref[...] = kv[:, half:].astype(v_ref.dtype)


def _mem_kv(mem, g, w_ckv):
    b, nm, d = mem.shape
    xd = w_ckv.shape[1] // 2
    out = jax.ShapeDtypeStruct((b, nm, xd), BF16)
    o_spec = pl.BlockSpec((None, nm, xd), lambda bi: (bi, 0, 0))
    return pl.pallas_call(
        _mem_kv_kernel,
        grid=(b,),
        in_specs=[
            pl.BlockSpec((None, nm, d), lambda bi: (bi, 0, 0)),
            pl.BlockSpec((1, d), lambda bi: (0, 0)),
            pl.BlockSpec((d, 2 * xd), lambda bi: (0, 0)),
        ],
        out_specs=[o_spec, o_spec],
        out_shape=[out, out],
        compiler_params=_cparams("arbitrary"),
        name="mem_kv",
    )(mem, g, w_ckv)


def _mid_kernel(x_ref, om_ref, of_ref, wom_ref, wof_ref, gx_ref, wcq_ref, kc_ref, vc_ref,
                wco_ref, gf_ref, h_ref, xn_ref):
    h1 = (x_ref[...]
          + jnp.dot(om_ref[...], wom_ref[...], preferred_element_type=F32)
          + jnp.dot(of_ref[...], wof_ref[...], preferred_element_type=F32))
    xn2 = _rmsnorm(h1, gx_ref[...]).astype(BF16)
    q = (jnp.dot(xn2, wcq_ref[...], preferred_element_type=F32) * SCALE).astype(BF16)
    outs = []
    for h in range(N_XATTN_HEADS):
        sl = slice(h * HEAD_DIM, (h + 1) * HEAD_DIM)
        s = lax.dot_general(q[:, sl], kc_ref[:, sl], _NT, preferred_element_type=F32)
        m, l, acc = _flash_init(s, vc_ref[:, sl])
        outs.append((acc / l).astype(BF16))
    oc = jnp.concatenate(outs, axis=1)
    h2 = h1 + jnp.dot(oc, wco_ref[...], preferred_element_type=F32)
    h_ref[...] = h2
    xn_ref[...] = _rmsnorm(h2, gf_ref[...]).astype(xn_ref.dtype)


def _mid(x2, o_moba, o_fox, w_om, w_of, gx, w_cq, kc, vc, w_co, gf, *, seq):
    n, d = x2.shape
    wm, wf = o_moba.shape[1], o_fox.shape[1]
    nm, xd = kc.shape[1], kc.shape[2]
    tm = min(512, seq)
    t_tiles = seq // tm
    const = lambda i: (0, 0)
    return pl.pallas_call(
        _mid_kernel,
        grid=(n // tm,),
        in_specs=[
            pl.BlockSpec((tm, d), lambda i: (i, 0)),
            pl.BlockSpec((tm, wm), lambda i: (i, 0)),
            pl.BlockSpec((tm, wf), lambda i: (i, 0)),
            pl.BlockSpec((wm, d), const),
            pl.BlockSpec((wf, d), const),
            pl.BlockSpec((1, d), const),
            pl.BlockSpec((d, xd), const),
            pl.BlockSpec((None, nm, xd), lambda i: (i // t_tiles, 0, 0)),
            pl.BlockSpec((None, nm, xd), lambda i: (i // t_tiles, 0, 0)),
            pl.BlockSpec((xd, d), const),
            pl.BlockSpec((1, d), const),
        ],
        out_specs=[
            pl.BlockSpec((tm, d), lambda i: (i, 0)),
            pl.BlockSpec((tm, d), lambda i: (i, 0)),
        ],
        out_shape=[
            jax.ShapeDtypeStruct((n, d), F32),
            jax.ShapeDtypeStruct((n, d), BF16),
        ],
        compiler_params=_cparams("arbitrary"),
        name="mid",
    )(x2, o_moba, o_fox, w_om, w_of, gx, w_cq, kc, vc, w_co, gf)


def _conv_ffn_kernel(xn_ref, h_ref, wg_ref, wu_ref, cwg_ref, cwu_ref, cbg_ref, cbu_ref,
                     wd_ref, fg_ref, o_ref, hs_sc, tail_sc, *, tiles_per_seq):
    i = pl.program_id(0)
    f = pl.program_id(1)
    nf = pl.num_programs(1)
    tm = xn_ref.shape[0]
    tf = wg_ref.shape[1]
    halo = SUBLANES

    xn = xn_ref[...]
    first = (i % tiles_per_seq) == 0

    @pl.when(first)
    def _():
        hs_sc[0:halo, :] = jnp.zeros((halo, 2 * tf), F32)

    @pl.when(jnp.logical_not(first))
    def _():
        hs_sc[0:halo, :] = tail_sc[f]

    hs_sc[halo:, :tf] = jnp.dot(xn, wg_ref[...], preferred_element_type=F32)
    hs_sc[halo:, tf:] = jnp.dot(xn, wu_ref[...], preferred_element_type=F32)
    tail_sc[f] = hs_sc[tm:tm + halo, :]

    cw = jnp.concatenate([cwg_ref[...], cwu_ref[...]], axis=1)
    cb = jnp.concatenate([cbg_ref[...], cbu_ref[...]], axis=1)
    y = cb
    for k in range(CONV_WIDTH):
        lo = halo - (CONV_WIDTH - 1) + k
        y = y + hs_sc[lo:lo + tm, :] * cw[k:k + 1, :]
    gte = y[:, :tf]
    act = (gte * (1.0 / (1.0 + jnp.exp2(gte * (-LOG2E)))) * y[:, tf:]).astype(BF16)
    part = jnp.dot(act, wd_ref[...], preferred_element_type=F32)

    @pl.when(f == 0)
    def _():
        o_ref[...] = h_ref[...] + part

    @pl.when(f > 0)
    def _():
        o_ref[...] += part

    @pl.when(f == nf - 1)
    def _():
        o_ref[...] = _rmsnorm(o_ref[...], fg_ref[...])


def _conv_ffn(xn3, h2, w_up, conv_w, conv_b, w_down, fg, *, seq):
    n, d = h2.shape
    ff = w_down.shape[0]
    tm = min(512, seq)
    tf = 512
    assert ff % tf == 0
    nf = ff // tf
    return pl.pallas_call(
        functools.partial(_conv_ffn_kernel, tiles_per_seq=seq // tm),
        grid=(n // tm, nf),
        in_specs=[
            pl.BlockSpec((tm, d), lambda i, f: (i, 0)),
            pl.BlockSpec((tm, d), lambda i, f: (i, 0)),
            pl.BlockSpec((d, tf), lambda i, f: (0, f)),
            pl.BlockSpec((d, tf), lambda i, f: (0, f + nf)),
            pl.BlockSpec((CONV_WIDTH, tf), lambda i, f: (0, f)),
            pl.BlockSpec((CONV_WIDTH, tf), lambda i, f: (0, f + nf)),
            pl.BlockSpec((1, tf), lambda i, f: (0, f)),
            pl.BlockSpec((1, tf), lambda i, f: (0, f + nf)),
            pl.BlockSpec((tf, d), lambda i, f: (f, 0)),
            pl.BlockSpec((1, d), lambda i, f: (0, 0)),
        ],
        out_specs=pl.BlockSpec((tm, d), lambda i, f: (i, 0)),
        out_shape=jax.ShapeDtypeStruct((n, d), F32),
        scratch_shapes=[
            pltpu.VMEM((tm + SUBLANES, 2 * tf), F32),
            pltpu.VMEM((nf, SUBLANES, 2 * tf), F32),
        ],
        compiler_params=_cparams("arbitrary", "arbitrary"),
        name="conv_ffn",
    )(xn3, h2, w_up, w_up, conv_w, conv_w, conv_b, conv_b, w_down, fg)


def _rope_tables(t):
    half = HEAD_DIM // 2
    inv = ROPE_THETA ** (-jnp.arange(half, dtype=F32) / half)
    ang = jnp.arange(t, dtype=F32)[:, None] * inv[None, :]
    cos, sin = jnp.cos(ang), jnp.sin(ang)
    return jnp.concatenate([cos, cos], axis=1), jnp.concatenate([-sin, sin], axis=1)


def kernel(x, mem, attn_norm_g, w_in, b_f, w_o, xattn_norm_g, mem_norm_g, w_cq, w_ckv, w_co,
           ffn_norm_g, w_up, conv_w, conv_b, w_down, final_norm_g):
    b, t, d = x.shape
    assert w_in.shape[0] == 1, "one layer: the final rmsnorm is fused into the ffn kernel"
    n_heads = d // HEAD_DIM
    n_moba = n_heads // 2
    n_fox = n_heads - n_moba
    assert n_moba == n_fox and n_fox <= LANES
    wgrp = n_moba * HEAD_DIM
    main_cols = 6 * wgrp
    cos, sin = _rope_tables(t)
    row = lambda v: v.reshape(1, -1)

    w_in_t = w_in[0].T
    w_z_t = jnp.pad(w_in_t[main_cols:], ((0, LANES - n_fox), (0, 0))).astype(BF16)
    bf = row(jnp.pad(b_f[0].astype(F32), (0, LANES - n_fox)))
    xn, qx, kx = _norm_gate(x, row(attn_norm_g[0]), w_z_t, bf, n_fox)
    proj = _in_proj(xn.reshape(b * t, d), w_in_t, cos, sin, seq=t, group_width=wgrp)
    proj3 = proj.reshape(b, t, main_cols)
    o_moba = _attn(proj3, n_moba, 0)
    o_fox = _attn(proj3, n_fox, 3 * n_moba, (qx, kx))
    kc, vc = _mem_kv(mem, row(mem_norm_g[0]), w_ckv[0].astype(BF16))
    wo = w_o[0].astype(BF16)
    h2, xn3 = _mid(x.reshape(b * t, d), o_moba.reshape(b * t, wgrp), o_fox.reshape(b * t, wgrp),
                   wo[:wgrp], wo[wgrp:], row(xattn_norm_g[0]), w_cq[0].astype(BF16), kc, vc,
                   w_co[0].astype(BF16), row(ffn_norm_g[0]), seq=t)
    out = _conv_ffn(xn3, h2, w_up[0].astype(BF16), conv_w[0], row(conv_b[0]),
                    w_down[0].astype(BF16), row(final_norm_g), seq=t)
    return out.reshape(b, t, d)
```

```python
import functools
import math

import jax
import jax.numpy as jnp
from jax import lax
from jax.experimental import pallas as pl
from jax.experimental.pallas import tpu as pltpu

HEAD_DIM = 128
MOBA_BLOCK = 256
MOBA_TOPK = 3
ROPE_THETA = 10000.0
N_XATTN_HEADS = 4
CONV_WIDTH = 3
RMS_EPS = 1e-6
NEG = -1e30
SCALE = 1.0 / math.sqrt(HEAD_DIM)
LOG2E = math.log2(math.e)
QSCALE = SCALE * LOG2E

LANES = 128
SUBLANES = 8
VMEM_LIMIT_BYTES = 56 * 1024 * 1024

F32 = jnp.float32
BF16 = jnp.bfloat16

_NT = (((1,), (1,)), ((), ()))


def _cparams(*sem):
    return pltpu.CompilerParams(dimension_semantics=sem, vmem_limit_bytes=VMEM_LIMIT_BYTES)


def _rmsnorm(x, g):
    ms = jnp.mean(x * x, axis=-1, keepdims=True)
    return x * lax.rsqrt(ms + RMS_EPS) * g


def _split3(x):
    hi = x.astype(BF16)
    r = x - hi.astype(F32)
    mid = r.astype(BF16)
    lo = (r - mid.astype(F32)).astype(BF16)
    return hi, mid, lo


def _in_proj_kernel(xn_ref, w_ref, cos_ref, sin_ref, o_ref, w_sc, *, group_of_tile):
    j = pl.program_id(0)

    @pl.when(pl.program_id(1) == 0)
    def _():
        w_sc[...] = w_ref[...].astype(BF16)

    grp = group_of_tile(j)
    rot = (grp <= 1).astype(F32)
    scale = jnp.where((grp == 0) | (grp == 3), QSCALE, 1.0).astype(F32)
    a = (cos_ref[...] * rot + (1.0 - rot)) * scale
    b = sin_ref[...] * (rot * scale)
    tm, tn = o_ref.shape
    chunk = min(256, tm)
    for r0 in range(0, tm, chunk):
        rows = slice(r0, r0 + chunk)
        y = lax.dot_general(xn_ref[rows, :], w_sc[...], _NT, preferred_element_type=F32)
        for c0 in range(0, tn, HEAD_DIM):
            yh = y[:, c0:c0 + HEAD_DIM]
            out = yh * a[rows] + pltpu.roll(yh, HEAD_DIM // 2, 1) * b[rows]
            o_ref[rows, c0:c0 + HEAD_DIM] = out.astype(o_ref.dtype)


def _in_proj(xn, w_in_t, cos, sin, *, seq, group_width):
    n, d = xn.shape
    cols = 6 * group_width
    tm = min(1024, seq)
    tn = min(1024, group_width)
    tiles_per_group = group_width // tn
    t_tiles = seq // tm
    kern = functools.partial(_in_proj_kernel, group_of_tile=lambda j: j // tiles_per_group)
    return pl.pallas_call(
        kern,
        grid=(cols // tn, n // tm),
        in_specs=[
            pl.BlockSpec((tm, d), lambda j, i: (i, 0)),
            pl.BlockSpec((tn, d), lambda j, i: (j, 0)),
            pl.BlockSpec((tm, HEAD_DIM), lambda j, i: (i % t_tiles, 0)),
            pl.BlockSpec((tm, HEAD_DIM), lambda j, i: (i % t_tiles, 0)),
        ],
        out_specs=pl.BlockSpec((tm, tn), lambda j, i: (i, j)),
        out_shape=jax.ShapeDtypeStruct((n, cols), BF16),
        scratch_shapes=[pltpu.VMEM((tn, d), BF16)],
        compiler_params=_cparams("arbitrary", "arbitrary"),
        name="in_proj",
    )(xn, w_in_t, cos, sin)


def _norm_gate_kernel(x_ref, g_ref, wz_ref, bf_ref, route_ref, xn_ref, qx_ref, kx_ref, carry_sc,
                      *, n_heads):
    t = pl.program_id(1)

    @pl.when(t == 0)
    def _():
        carry_sc[...] = jnp.zeros_like(carry_sc)

    xn = _rmsnorm(x_ref[...], g_ref[...]).astype(BF16)
    xn_ref[...] = xn
    z = lax.dot_general(xn, wz_ref[...], _NT, preferred_element_type=F32) + bf_ref[...]
    lf = jnp.minimum(z, 0.0) - jnp.log(1.0 + jnp.exp(-jnp.abs(z)))

    tc = lf.shape[0]
    row = lax.broadcasted_iota(jnp.int32, (tc, tc), 0)
    col = lax.broadcasted_iota(jnp.int32, (tc, tc), 1)
    tri = (col <= row).astype(BF16)
    c = carry_sc[0:1, :]
    for part in _split3(lf):
        c = c + jnp.dot(tri, part, preferred_element_type=F32)
    carry_sc[0:1, :] = c[tc - 1:tc, :]

    parts = jnp.concatenate(_split3(c * LOG2E), axis=1)
    routed = jnp.dot(parts, route_ref[...], preferred_element_type=F32)
    lane = lax.broadcasted_iota(jnp.int32, (tc, LANES), 1)
    q_ones = ((lane >= 3) & (lane < 6)).astype(F32)
    k_ones = (lane < 3).astype(F32)
    for h in range(n_heads):
        qx_ref[h] = (routed[:, (2 * h) * LANES:(2 * h + 1) * LANES] + q_ones).astype(BF16)
        kx_ref[h] = (routed[:, (2 * h + 1) * LANES:(2 * h + 2) * LANES] + k_ones).astype(BF16)


def _gate_routing(n_heads):
    import numpy as np
    r = np.zeros((3 * LANES, n_heads * 2 * LANES), np.float32)
    for h in range(n_heads):
        for p in range(3):
            r[p * LANES + h, (2 * h) * LANES + p] = 1.0
            r[p * LANES + h, (2 * h + 1) * LANES + 3 + p] = -1.0
    return jnp.asarray(r, BF16)


def _norm_gate(x3, g, w_z, b_f, n_heads):
    b, t, d = x3.shape
    tc = min(512, t)
    ext = jax.ShapeDtypeStruct((b, n_heads, t, LANES), BF16)
    ext_spec = pl.BlockSpec((None, n_heads, tc, LANES), lambda bi, ti: (bi, 0, ti, 0))
    const = lambda bi, ti: (0, 0)
    return pl.pallas_call(
        functools.partial(_norm_gate_kernel, n_heads=n_heads),
        grid=(b, t // tc),
        in_specs=[
            pl.BlockSpec((None, tc, d), lambda bi, ti: (bi, ti, 0)),
            pl.BlockSpec((1, d), const),
            pl.BlockSpec((LANES, d), const),
            pl.BlockSpec((1, LANES), const),
            pl.BlockSpec((3 * LANES, n_heads * 2 * LANES), const),
        ],
        out_specs=[pl.BlockSpec((None, tc, d), lambda bi, ti: (bi, ti, 0)), ext_spec, ext_spec],
        out_shape=[jax.ShapeDtypeStruct((b, t, d), BF16), ext, ext],
        scratch_shapes=[pltpu.VMEM((SUBLANES, LANES), F32)],
        compiler_params=_cparams("arbitrary", "arbitrary"),
        name="norm_gate",
    )(x3, g, w_z, b_f, _gate_routing(n_heads))


ATT_TILE = 512
ATT_HEADS = 4
VT_ROWS = HEAD_DIM + 16


def _flash_init(s, v):
    m = jnp.max(s, axis=1, keepdims=True)
    p = jnp.exp(s - m)
    l = jnp.sum(p, axis=1, keepdims=True)
    acc = jnp.dot(p.astype(BF16), v, preferred_element_type=F32)
    return m, l, acc


def _moba_bias(q, kmean, iq):
    tq = q.shape[0]
    nb = kmean.shape[0]
    km_hi = kmean.astype(BF16)
    km_lo = (kmean - km_hi.astype(F32)).astype(BF16)
    gate = (lax.dot_general(km_hi, q, _NT, preferred_element_type=F32)
            + lax.dot_general(km_lo, q, _NT, preferred_element_type=F32))
    blk_i = lax.broadcasted_iota(jnp.int32, gate.shape, 0)
    qry_i = lax.broadcasted_iota(jnp.int32, gate.shape, 1)
    own = iq * (tq // MOBA_BLOCK) + qry_i // MOBA_BLOCK
    blk_f = blk_i.astype(F32)
    past = blk_i < own
    g = jnp.where(past, gate, -jnp.inf)
    sel = jnp.zeros(gate.shape, jnp.bool_)
    for _ in range(MOBA_TOPK):
        mx = jnp.max(g, axis=0, keepdims=True)
        idx = jnp.min(jnp.where(g == mx, blk_f, float(LANES)), axis=0, keepdims=True)
        pick = blk_f == idx
        sel = sel | pick
        g = jnp.where(pick, -jnp.inf, g)
    bias_t = jnp.where((sel & past) | (blk_i == own), 0.0, NEG)
    bias_t = jnp.concatenate([bias_t, jnp.zeros((LANES - nb, tq), F32)], axis=0)
    return bias_t.T.astype(BF16)


def _attn_kernel(*refs, moba):
    if moba:
        q_ref, k_ref, v_ref, o_ref = refs[:4]
        kmean_sc = refs[-1]
        refs = refs[:-1]
    else:
        q_ref, k_ref, v_ref, qx_ref, kx_ref, o_ref = refs[:6]
    kaug_sc, vt_sc, sa_sc, sb_sc, mxa_sc, mxb_sc, m_sc, acc_sc = refs[-8:]
    iq = pl.program_id(2)
    tq = q_ref.shape[0]
    t = k_ref.shape[0]
    n_heads = q_ref.shape[1] // HEAD_DIM
    heads = [slice(g * HEAD_DIM, (g + 1) * HEAD_DIM) for g in range(n_heads)]

    @pl.when(iq == 0)
    def _():
        for g, hs in enumerate(heads):
            kaug_sc[g, :, :HEAD_DIM] = k_ref[:, hs]
            if moba:
                row = lax.broadcasted_iota(jnp.int32, (t, LANES), 0)
                lane = lax.broadcasted_iota(jnp.int32, (t, LANES), 1)
                kaug_sc[g, :, HEAD_DIM:] = ((row // MOBA_BLOCK) == lane).astype(BF16)
                kmean_sc[g] = jnp.zeros((LANES, HEAD_DIM), F32)
                for n in range(t // MOBA_BLOCK):
                    kb = k_ref[n * MOBA_BLOCK:(n + 1) * MOBA_BLOCK, hs].astype(F32)
                    kmean_sc[g, n:n + 1, :] = jnp.mean(kb, axis=0, keepdims=True)
            else:
                kaug_sc[g, :, HEAD_DIM:] = kx_ref[g]
            pad_row = lax.broadcasted_iota(jnp.int32, (VT_ROWS - HEAD_DIM, tq), 0)
            for j in range(t // tq):
                vt_sc[g, j, :HEAD_DIM, :] = (
                    v_ref[j * tq:(j + 1) * tq, hs].astype(F32).T.astype(BF16))
                vt_sc[g, j, HEAD_DIM:, :] = (pad_row == 0).astype(BF16)

    qa = []
    for g, hs in enumerate(heads):
        q = q_ref[:, hs]
        if moba:
            nb = -(-(t // MOBA_BLOCK) // 16) * 16
            extra = _moba_bias(q, kmean_sc[g, :nb, :], iq)
        else:
            extra = qx_ref[g]
        qa.append(jnp.concatenate([q, extra], axis=1))

    def scores(g, j):
        off = pl.multiple_of(j * tq, tq)
        return lax.dot_general(kaug_sc[g, pl.ds(off, tq), :], qa[g], _NT,
                               preferred_element_type=F32)

    half = tq // 2
    halves = (slice(0, half), slice(half, tq))

    def produce(j, s_ref, mx_ref):
        for g in range(n_heads):
            s = scores(g, j)
            s_ref[g] = s
            mx_ref[g] = jnp.max(s, axis=0, keepdims=True)

    def consume(j, s_ref, mx_ref, diagonal=False):
        for g in range(n_heads):
            m = m_sc[g]
            m_new = jnp.maximum(m, mx_ref[g])
            alpha = jnp.exp2(m - m_new)
            m_sc[g] = m_new
            for qi, qs in enumerate(halves):
                pv = alpha[:, qs] * acc_sc[g, :, qs]
                for ki, ks in enumerate(halves):
                    if diagonal and ki > qi:
                        continue
                    p = jnp.exp2(s_ref[g, ks, qs] - m_new[:, qs])
                    pv = pv + jnp.dot(vt_sc[g, j, :, ks], p.astype(BF16),
                                      preferred_element_type=F32)
                acc_sc[g, :, qs] = pv

    for g in range(n_heads):
        m_sc[g] = jnp.full((1, tq), NEG, F32)
        acc_sc[g] = jnp.zeros((VT_ROWS, tq), F32)
    produce(0, sa_sc, mxa_sc)

    def body(jj, carry):
        j = 2 * jj
        produce(j + 1, sb_sc, mxb_sc)
        consume(j, sa_sc, mxa_sc)
        produce(j + 2, sa_sc, mxa_sc)
        consume(j + 1, sb_sc, mxb_sc)
        return carry

    lax.fori_loop(0, iq // 2, body, 0)

    def finish(s_ref, mx_ref):
        key = lax.broadcasted_iota(jnp.int32, (tq, tq), 0)
        qry = lax.broadcasted_iota(jnp.int32, (tq, tq), 1)
        for g in range(n_heads):
            s = jnp.where(key <= qry, s_ref[g], NEG)
            s_ref[g] = s
            mx_ref[g] = jnp.max(s, axis=0, keepdims=True)
        consume(iq, s_ref, mx_ref, diagonal=True)
        for g, hs in enumerate(heads):
            out = acc_sc[g, :HEAD_DIM, :] / acc_sc[g, HEAD_DIM:HEAD_DIM + 1, :]
            o_ref[:, hs] = out.T.astype(o_ref.dtype)

    @pl.when(iq % 2 == 0)
    def _():
        finish(sa_sc, mxa_sc)

    @pl.when(iq % 2 == 1)
    def _():
        produce(iq, sb_sc, mxb_sc)
        consume(iq - 1, sa_sc, mxa_sc)
        finish(sb_sc, mxb_sc)


def _attn(proj3, n_heads, head0, fox_extras=None):
    b, t, _ = proj3.shape
    moba = fox_extras is None
    tq = min(ATT_TILE, t)
    g = min(ATT_HEADS, n_heads)
    assert t % tq == 0 and tq % MOBA_BLOCK == 0 and t // MOBA_BLOCK <= LANES and n_heads % g == 0
    w = g * HEAD_DIM
    c0 = head0 // g
    in_specs = [
        pl.BlockSpec((None, tq, w), lambda bi, h, i: (bi, i, c0 + h)),
        pl.BlockSpec((None, t, w), lambda bi, h, i: (bi, 0, c0 + n_heads // g + h)),
        pl.BlockSpec((None, t, w), lambda bi, h, i: (bi, 0, c0 + 2 * (n_heads // g) + h)),
    ]
    args = [proj3, proj3, proj3]
    scratch = [
        pltpu.VMEM((g, t, 2 * HEAD_DIM), BF16),
        pltpu.VMEM((g, t // tq, VT_ROWS, tq), BF16),
        pltpu.VMEM((g, tq, tq), F32),
        pltpu.VMEM((g, tq, tq), F32),
        pltpu.VMEM((g, 1, tq), F32),
        pltpu.VMEM((g, 1, tq), F32),
        pltpu.VMEM((g, 1, tq), F32),
        pltpu.VMEM((g, VT_ROWS, tq), F32),
    ]
    if moba:
        scratch.append(pltpu.VMEM((g, LANES, HEAD_DIM), F32))
    else:
        in_specs += [
            pl.BlockSpec((None, g, tq, LANES), lambda bi, h, i: (bi, h, i, 0)),
            pl.BlockSpec((None, g, t, LANES), lambda bi, h, i: (bi, h, 0, 0)),
        ]
        args += list(fox_extras)
    return pl.pallas_call(
        functools.partial(_attn_kernel, moba=moba),
        grid=(b, n_heads // g, t // tq),
        in_specs=in_specs,
        out_specs=pl.BlockSpec((None, tq, w), lambda bi, h, i: (bi, i, h)),
        out_shape=jax.ShapeDtypeStruct((b, t, n_heads * HEAD_DIM), BF16),
        scratch_shapes=scratch,
        compiler_params=_cparams("arbitrary", "arbitrary", "arbitrary"),
        name="moba_attn" if moba else "fox_attn",
    )(*args)


def _mem_kv_kernel(mem_ref, g_ref, w_ref, k_ref, v_ref):
    xn = _rmsnorm(mem_ref[...], g_ref[...]).astype(BF16)
    kv = jnp.dot(xn, w_ref[...], preferred_element_type=F32)
    half = kv.shape[1] // 2
    k_ref[...] = kv[:, :half].astype(k_ref.dtype)
    v_ref[...] = kv[:, half:].astype(v_ref.dtype)


def _mem_kv(mem, g, w_ckv):
    b, nm, d = mem.shape
    xd = w_ckv.shape[1] // 2
    out = jax.ShapeDtypeStruct((b, nm, xd), BF16)
    o_spec = pl.BlockSpec((None, nm, xd), lambda bi: (bi, 0, 0))
    return pl.pallas_call(
        _mem_kv_kernel,
        grid=(b,),
        in_specs=[
            pl.BlockSpec((None, nm, d), lambda bi: (bi, 0, 0)),
            pl.BlockSpec((1, d), lambda bi: (0, 0)),
            pl.BlockSpec((d, 2 * xd), lambda bi: (0, 0)),
        ],
        out_specs=[o_spec, o_spec],
        out_shape=[out, out],
        compiler_params=_cparams("arbitrary"),
        name="mem_kv",
    )(mem, g, w_ckv)


def _mid_kernel(x_ref, om_ref, of_ref, wom_ref, wof_ref, gx_ref, wcq_ref, kc_ref, vc_ref,
                wco_ref, gf_ref, h_ref, xn_ref):
    h1 = (x_ref[...]
          + jnp.dot(om_ref[...], wom_ref[...], preferred_element_type=F32)
          + jnp.dot(of_ref[...], wof_ref[...], preferred_element_type=F32))
    xn2 = _rmsnorm(h1, gx_ref[...]).astype(BF16)
    q = (jnp.dot(xn2, wcq_ref[...], preferred_element_type=F32) * SCALE).astype(BF16)
    outs = []
    for h in range(N_XATTN_HEADS):
        sl = slice(h * HEAD_DIM, (h + 1) * HEAD_DIM)
        s = lax.dot_general(q[:, sl], kc_ref[:, sl], _NT, preferred_element_type=F32)
        m, l, acc = _flash_init(s, vc_ref[:, sl])
        outs.append((acc / l).astype(BF16))
    oc = jnp.concatenate(outs, axis=1)
    h2 = h1 + jnp.dot(oc, wco_ref[...], preferred_element_type=F32)
    h_ref[...] = h2
    xn_ref[...] = _rmsnorm(h2, gf_ref[...]).astype(xn_ref.dtype)


def _mid(x2, o_moba, o_fox, w_om, w_of, gx, w_cq, kc, vc, w_co, gf, *, seq):
    n, d = x2.shape
    wm, wf = o_moba.shape[1], o_fox.shape[1]
    nm, xd = kc.shape[1], kc.shape[2]
    tm = min(512, seq)
    t_tiles = seq // tm
    const = lambda i: (0, 0)
    return pl.pallas_call(
        _mid_kernel,
        grid=(n // tm,),
        in_specs=[
            pl.BlockSpec((tm, d), lambda i: (i, 0)),
            pl.BlockSpec((tm, wm), lambda i: (i, 0)),
            pl.BlockSpec((tm, wf), lambda i: (i, 0)),
            pl.BlockSpec((wm, d), const),
            pl.BlockSpec((wf, d), const),
            pl.BlockSpec((1, d), const),
            pl.BlockSpec((d, xd), const),
            pl.BlockSpec((None, nm, xd), lambda i: (i // t_tiles, 0, 0)),
            pl.BlockSpec((None, nm, xd), lambda i: (i // t_tiles, 0, 0)),
            pl.BlockSpec((xd, d), const),
            pl.BlockSpec((1, d), const),
        ],
        out_specs=[
            pl.BlockSpec((tm, d), lambda i: (i, 0)),
            pl.BlockSpec((tm, d), lambda i: (i, 0)),
        ],
        out_shape=[
            jax.ShapeDtypeStruct((n, d), F32),
            jax.ShapeDtypeStruct((n, d), BF16),
        ],
        compiler_params=_cparams("arbitrary"),
        name="mid",
    )(x2, o_moba, o_fox, w_om, w_of, gx, w_cq, kc, vc, w_co, gf)


def _conv_ffn_kernel(xn_ref, h_ref, wg_ref, wu_ref, cwg_ref, cwu_ref, cbg_ref, cbu_ref,
                     wd_ref, fg_ref, o_ref, hs_sc, act_sc, tail_sc, *, tiles_per_seq):
    i = pl.program_id(0)
    f = pl.program_id(1)
    nf = pl.num_programs(1)
    tm = xn_ref.shape[0]
    tf = wg_ref.shape[1]
    halo = SUBLANES
    first = (i % tiles_per_seq) == 0

    @pl.when((i == 0) & (f == 0))
    def _():
        tail_sc[...] = jnp.zeros_like(tail_sc)

    def up():
        xn = xn_ref[...]
        prev = jnp.where(first, 0.0, tail_sc[f])
        for cols, w_ref in ((slice(0, tf), wg_ref), (slice(tf, 2 * tf), wu_ref)):
            h = jnp.dot(xn, w_ref[...], preferred_element_type=F32)
            tail_sc[f, :, cols] = h[tm - halo:, :]
            for k in range(CONV_WIDTH):
                hs_sc[k, halo + k:halo + k + tm, cols] = h
                if k:
                    hs_sc[k, halo:halo + k, cols] = prev[halo - k:, cols]

    def gate():
        rows = 64
        for c0 in range(0, tf, LANES):
            cols = slice(c0, c0 + LANES)

            def conv(base, cw_ref, cb_ref, r0):
                y = cb_ref[:, cols]
                for k in range(CONV_WIDTH):
                    tap = cw_ref[CONV_WIDTH - 1 - k:CONV_WIDTH - k, cols]
                    y = y + hs_sc[k, halo + r0:halo + r0 + rows, base + c0:base + c0 + LANES] * tap
                return y

            for r0 in range(0, tm, rows):
                gte = conv(0, cwg_ref, cbg_ref, r0)
                upv = conv(tf, cwu_ref, cbu_ref, r0)
                act_sc[r0:r0 + rows, cols] = (
                    gte * (1.0 / (1.0 + jnp.exp2(gte * (-LOG2E)))) * upv).astype(BF16)

    @pl.when(f == 0)
    def _():
        o_ref[...] = h_ref[...]

    up()
    gate()
    o_ref[...] += jnp.dot(act_sc[...], wd_ref[...], preferred_element_type=F32)

    @pl.when(f == nf - 1)
    def _():
        o_ref[...] = _rmsnorm(o_ref[...], fg_ref[...])


def _conv_ffn(xn3, h2, w_up, conv_w, conv_b, w_down, fg, *, seq):
    n, d = h2.shape
    ff = w_down.shape[0]
    tm = min(512, seq)
    tf = 512
    assert ff % tf == 0
    nf = ff // tf
    return pl.pallas_call(
        functools.partial(_conv_ffn_kernel, tiles_per_seq=seq // tm),
        grid=(n // tm, nf),
        in_specs=[
            pl.BlockSpec((tm, d), lambda i, f: (i, 0)),
            pl.BlockSpec((tm, d), lambda i, f: (i, 0)),
            pl.BlockSpec((d, tf), lambda i, f: (0, f)),
            pl.BlockSpec((d, tf), lambda i, f: (0, f + nf)),
            pl.BlockSpec((CONV_WIDTH, tf), lambda i, f: (0, f)),
            pl.BlockSpec((CONV_WIDTH, tf), lambda i, f: (0, f + nf)),
            pl.BlockSpec((1, tf), lambda i, f: (0, f)),
            pl.BlockSpec((1, tf), lambda i, f: (0, f + nf)),
            pl.BlockSpec((tf, d), lambda i, f: (f, 0)),
            pl.BlockSpec((1, d), lambda i, f: (0, 0)),
        ],
        out_specs=pl.BlockSpec((tm, d), lambda i, f: (i, 0)),
        out_shape=jax.ShapeDtypeStruct((n, d), F32),
        scratch_shapes=[
            pltpu.VMEM((CONV_WIDTH, tm + 2 * SUBLANES, 2 * tf), F32),
            pltpu.VMEM((tm, tf), BF16),
            pltpu.VMEM((nf, SUBLANES, 2 * tf), F32),
        ],
        compiler_params=_cparams("arbitrary", "arbitrary"),
        name="conv_ffn",
    )(xn3, h2, w_up, w_up, conv_w, conv_w, conv_b, conv_b, w_down, fg)


def _rope_tables(t):
    half = HEAD_DIM // 2
    inv = ROPE_THETA ** (-jnp.arange(half, dtype=F32) / half)
    ang = jnp.arange(t, dtype=F32)[:, None] * inv[None, :]
    cos, sin = jnp.cos(ang), jnp.sin(ang)
    return jnp.concatenate([cos, cos], axis=1), jnp.concatenate([-sin, sin], axis=1)


def kernel(x, mem, attn_norm_g, w_in, b_f, w_o, xattn_norm_g, mem_norm_g, w_cq, w_ckv, w_co,
           ffn_norm_g, w_up, conv_w, conv_b, w_down, final_norm_g):
    b, t, d = x.shape
    assert w_in.shape[0] == 1, "one layer: the final rmsnorm is fused into the ffn kernel"
    n_heads = d // HEAD_DIM
    n_moba = n_heads // 2
    n_fox = n_heads - n_moba
    assert n_moba == n_fox and n_fox <= LANES
    wgrp = n_moba * HEAD_DIM
    main_cols = 6 * wgrp
    cos, sin = _rope_tables(t)
    row = lambda v: v.reshape(1, -1)

    w_in_t = w_in[0].T
    w_z_t = jnp.pad(w_in_t[main_cols:], ((0, LANES - n_fox), (0, 0))).astype(BF16)
    bf = row(jnp.pad(b_f[0].astype(F32), (0, LANES - n_fox)))
    xn, qx, kx = _norm_gate(x, row(attn_norm_g[0]), w_z_t, bf, n_fox)
    proj = _in_proj(xn.reshape(b * t, d), w_in_t, cos, sin, seq=t, group_width=wgrp)
    proj3 = proj.reshape(b, t, main_cols)
    o_moba = _attn(proj3, n_moba, 0)
    o_fox = _attn(proj3, n_fox, 3 * n_moba, (qx, kx))
    kc, vc = _mem_kv(mem, row(mem_norm_g[0]), w_ckv[0].astype(BF16))
    wo = w_o[0].astype(BF16)
    h2, xn3 = _mid(x.reshape(b * t, d), o_moba.reshape(b * t, wgrp), o_fox.reshape(b * t, wgrp),
                   wo[:wgrp], wo[wgrp:], row(xattn_norm_g[0]), w_cq[0].astype(BF16), kc, vc,
                   w_co[0].astype(BF16), row(ffn_norm_g[0]), seq=t)
    out = _conv_ffn(xn3, h2, w_up[0].astype(BF16), conv_w[0], row(conv_b[0]),
                    w_down[0].astype(BF16), row(final_norm_g), seq=t)
    return out.reshape(b, t, d)
```

```python
import functools
import math

import jax
import jax.numpy as jnp
from jax import lax
from jax.experimental import pallas as pl
from jax.experimental.pallas import tpu as pltpu

HEAD_DIM = 128
MOBA_BLOCK = 256
MOBA_TOPK = 3
ROPE_THETA = 10000.0
N_XATTN_HEADS = 4
CONV_WIDTH = 3
RMS_EPS = 1e-6
NEG = -1e30
SCALE = 1.0 / math.sqrt(HEAD_DIM)
LOG2E = math.log2(math.e)
QSCALE = SCALE * LOG2E

LANES = 128
SUBLANES = 8
VMEM_LIMIT_BYTES = 56 * 1024 * 1024

F32 = jnp.float32
BF16 = jnp.bfloat16

_NT = (((1,), (1,)), ((), ()))


def _cparams(*sem):
    return pltpu.CompilerParams(dimension_semantics=sem, vmem_limit_bytes=VMEM_LIMIT_BYTES)


def _rmsnorm(x, g):
    ms = jnp.mean(x * x, axis=-1, keepdims=True)
    return x * lax.rsqrt(ms + RMS_EPS) * g


def _split3(x):
    hi = x.astype(BF16)
    r = x - hi.astype(F32)
    mid = r.astype(BF16)
    lo = (r - mid.astype(F32)).astype(BF16)
    return hi, mid, lo


def _in_proj_kernel(xn_ref, w_ref, cos_ref, sin_ref, o_ref, w_sc, *, group_of_tile):
    j = pl.program_id(0)

    @pl.when(pl.program_id(1) == 0)
    def _():
        w_sc[...] = w_ref[...].astype(BF16)

    grp = group_of_tile(j)
    rot = (grp <= 1).astype(F32)
    scale = jnp.where((grp == 0) | (grp == 3), QSCALE, 1.0).astype(F32)
    a = (cos_ref[...] * rot + (1.0 - rot)) * scale
    b = sin_ref[...] * (rot * scale)
    tm, tn = o_ref.shape
    chunk = min(256, tm)
    for r0 in range(0, tm, chunk):
        rows = slice(r0, r0 + chunk)
        y = lax.dot_general(xn_ref[rows, :], w_sc[...], _NT, preferred_element_type=F32)
        for c0 in range(0, tn, HEAD_DIM):
            yh = y[:, c0:c0 + HEAD_DIM]
            out = yh * a[rows] + pltpu.roll(yh, HEAD_DIM // 2, 1) * b[rows]
            o_ref[rows, c0:c0 + HEAD_DIM] = out.astype(o_ref.dtype)


def _in_proj(xn, w_in_t, cos, sin, *, seq, group_width):
    n, d = xn.shape
    cols = 6 * group_width
    tm = min(1024, seq)
    tn = min(1024, group_width)
    tiles_per_group = group_width // tn
    t_tiles = seq // tm
    kern = functools.partial(_in_proj_kernel, group_of_tile=lambda j: j // tiles_per_group)
    return pl.pallas_call(
        kern,
        grid=(cols // tn, n // tm),
        in_specs=[
            pl.BlockSpec((tm, d), lambda j, i: (i, 0)),
            pl.BlockSpec((tn, d), lambda j, i: (j, 0)),
            pl.BlockSpec((tm, HEAD_DIM), lambda j, i: (i % t_tiles, 0)),
            pl.BlockSpec((tm, HEAD_DIM), lambda j, i: (i % t_tiles, 0)),
        ],
        out_specs=pl.BlockSpec((tm, tn), lambda j, i: (i, j)),
        out_shape=jax.ShapeDtypeStruct((n, cols), BF16),
        scratch_shapes=[pltpu.VMEM((tn, d), BF16)],
        compiler_params=_cparams("arbitrary", "arbitrary"),
        name="in_proj",
    )(xn, w_in_t, cos, sin)


def _norm_gate_kernel(x_ref, g_ref, wz_ref, bf_ref, route_ref, xn_ref, qx_ref, kx_ref, carry_sc,
                      *, n_heads):
    t = pl.program_id(1)

    @pl.when(t == 0)
    def _():
        carry_sc[...] = jnp.zeros_like(carry_sc)

    xn = _rmsnorm(x_ref[...], g_ref[...]).astype(BF16)
    xn_ref[...] = xn
    z = lax.dot_general(xn, wz_ref[...], _NT, preferred_element_type=F32) + bf_ref[...]
    lf = jnp.minimum(z, 0.0) - jnp.log(1.0 + jnp.exp(-jnp.abs(z)))

    tc = lf.shape[0]
    row = lax.broadcasted_iota(jnp.int32, (tc, tc), 0)
    col = lax.broadcasted_iota(jnp.int32, (tc, tc), 1)
    tri = (col <= row).astype(BF16)
    c = carry_sc[0:1, :]
    for part in _split3(lf):
        c = c + jnp.dot(tri, part, preferred_element_type=F32)
    carry_sc[0:1, :] = c[tc - 1:tc, :]

    parts = jnp.concatenate(_split3(c * LOG2E), axis=1)
    routed = jnp.dot(parts, route_ref[...], preferred_element_type=F32)
    lane = lax.broadcasted_iota(jnp.int32, (tc, LANES), 1)
    q_ones = ((lane >= 3) & (lane < 6)).astype(F32)
    k_ones = (lane < 3).astype(F32)
    for h in range(n_heads):
        qx_ref[h] = (routed[:, (2 * h) * LANES:(2 * h + 1) * LANES] + q_ones).astype(BF16)
        kx_ref[h] = (routed[:, (2 * h + 1) * LANES:(2 * h + 2) * LANES] + k_ones).astype(BF16)


def _gate_routing(n_heads):
    import numpy as np
    r = np.zeros((3 * LANES, n_heads * 2 * LANES), np.float32)
    for h in range(n_heads):
        for p in range(3):
            r[p * LANES + h, (2 * h) * LANES + p] = 1.0
            r[p * LANES + h, (2 * h + 1) * LANES + 3 + p] = -1.0
    return jnp.asarray(r, BF16)


def _norm_gate(x3, g, w_z, b_f, n_heads):
    b, t, d = x3.shape
    tc = min(512, t)
    ext = jax.ShapeDtypeStruct((b, n_heads, t, LANES), BF16)
    ext_spec = pl.BlockSpec((None, n_heads, tc, LANES), lambda bi, ti: (bi, 0, ti, 0))
    const = lambda bi, ti: (0, 0)
    return pl.pallas_call(
        functools.partial(_norm_gate_kernel, n_heads=n_heads),
        grid=(b, t // tc),
        in_specs=[
            pl.BlockSpec((None, tc, d), lambda bi, ti: (bi, ti, 0)),
            pl.BlockSpec((1, d), const),
            pl.BlockSpec((LANES, d), const),
            pl.BlockSpec((1, LANES), const),
            pl.BlockSpec((3 * LANES, n_heads * 2 * LANES), const),
        ],
        out_specs=[pl.BlockSpec((None, tc, d), lambda bi, ti: (bi, ti, 0)), ext_spec, ext_spec],
        out_shape=[jax.ShapeDtypeStruct((b, t, d), BF16), ext, ext],
        scratch_shapes=[pltpu.VMEM((SUBLANES, LANES), F32)],
        compiler_params=_cparams("arbitrary", "arbitrary"),
        name="norm_gate",
    )(x3, g, w_z, b_f, _gate_routing(n_heads))


ATT_TILE = 512
ATT_HEADS = 4
VT_ROWS = HEAD_DIM + 16


def _flash_init(s, v):
    m = jnp.max(s, axis=1, keepdims=True)
    p = jnp.exp(s - m)
    l = jnp.sum(p, axis=1, keepdims=True)
    acc = jnp.dot(p.astype(BF16), v, preferred_element_type=F32)
    return m, l, acc


def _moba_bias(q, kmean, iq):
    tq = q.shape[0]
    nb = kmean.shape[0]
    km_hi = kmean.astype(BF16)
    km_lo = (kmean - km_hi.astype(F32)).astype(BF16)
    gate = (lax.dot_general(km_hi, q, _NT, preferred_element_type=F32)
            + lax.dot_general(km_lo, q, _NT, preferred_element_type=F32))
    blk_i = lax.broadcasted_iota(jnp.int32, gate.shape, 0)
    qry_i = lax.broadcasted_iota(jnp.int32, gate.shape, 1)
    own = iq * (tq // MOBA_BLOCK) + qry_i // MOBA_BLOCK
    blk_f = blk_i.astype(F32)
    past = blk_i < own
    g = jnp.where(past, gate, -jnp.inf)
    sel = jnp.zeros(gate.shape, jnp.bool_)
    for _ in range(MOBA_TOPK):
        mx = jnp.max(g, axis=0, keepdims=True)
        idx = jnp.min(jnp.where(g == mx, blk_f, float(LANES)), axis=0, keepdims=True)
        pick = blk_f == idx
        sel = sel | pick
        g = jnp.where(pick, -jnp.inf, g)
    bias_t = jnp.where((sel & past) | (blk_i == own), 0.0, NEG)
    bias_t = jnp.concatenate([bias_t, jnp.zeros((LANES - nb, tq), F32)], axis=0)
    return bias_t.T.astype(BF16)


def _attn_kernel(*refs, moba):
    if moba:
        q_ref, k_ref, v_ref, o_ref = refs[:4]
        kmean_sc = refs[-1]
        refs = refs[:-1]
    else:
        q_ref, k_ref, v_ref, qx_ref, kx_ref, o_ref = refs[:6]
    kaug_sc, vt_sc, sa_sc, sb_sc, mxa_sc, mxb_sc, m_sc, acc_sc = refs[-8:]
    iq = pl.program_id(2)
    tq = q_ref.shape[0]
    t = k_ref.shape[0]
    n_heads = q_ref.shape[1] // HEAD_DIM
    heads = [slice(g * HEAD_DIM, (g + 1) * HEAD_DIM) for g in range(n_heads)]

    @pl.when(iq == 0)
    def _():
        for g, hs in enumerate(heads):
            kaug_sc[g, :, :HEAD_DIM] = k_ref[:, hs]
            if moba:
                row = lax.broadcasted_iota(jnp.int32, (t, LANES), 0)
                lane = lax.broadcasted_iota(jnp.int32, (t, LANES), 1)
                kaug_sc[g, :, HEAD_DIM:] = ((row // MOBA_BLOCK) == lane).astype(BF16)
                kmean_sc[g] = jnp.zeros((LANES, HEAD_DIM), F32)
                for n in range(t // MOBA_BLOCK):
                    kb = k_ref[n * MOBA_BLOCK:(n + 1) * MOBA_BLOCK, hs].astype(F32)
                    kmean_sc[g, n:n + 1, :] = jnp.mean(kb, axis=0, keepdims=True)
            else:
                kaug_sc[g, :, HEAD_DIM:] = kx_ref[g]
            pad_row = lax.broadcasted_iota(jnp.int32, (VT_ROWS - HEAD_DIM, tq), 0)
            for j in range(t // tq):
                vt_sc[g, j, :HEAD_DIM, :] = (
                    v_ref[j * tq:(j + 1) * tq, hs].astype(F32).T.astype(BF16))
                vt_sc[g, j, HEAD_DIM:, :] = (pad_row == 0).astype(BF16)

    qa = []
    for g, hs in enumerate(heads):
        q = q_ref[:, hs]
        if moba:
            nb = -(-(t // MOBA_BLOCK) // 16) * 16
            extra = _moba_bias(q, kmean_sc[g, :nb, :], iq)
        else:
            extra = qx_ref[g]
        qa.append(jnp.concatenate([q, extra], axis=1))

    def scores(g, j):
        off = pl.multiple_of(j * tq, tq)
        return lax.dot_general(kaug_sc[g, pl.ds(off, tq), :], qa[g], _NT,
                               preferred_element_type=F32)

    half = tq // 2
    halves = (slice(0, half), slice(half, tq))

    def produce(j, s_ref, mx_ref):
        for g in range(n_heads):
            s = scores(g, j)
            s_ref[g] = s
            mx_ref[g] = jnp.max(s, axis=0, keepdims=True)

    def consume(j, s_ref, mx_ref, diagonal=False):
        for g in range(n_heads):
            m = m_sc[g]
            m_new = jnp.maximum(m, mx_ref[g])
            alpha = jnp.exp2(m - m_new)
            m_sc[g] = m_new
            for qi, qs in enumerate(halves):
                pv = alpha[:, qs] * acc_sc[g, :, qs]
                for ki, ks in enumerate(halves):
                    if diagonal and ki > qi:
                        continue
                    p = jnp.exp2(s_ref[g, ks, qs] - m_new[:, qs])
                    pv = pv + jnp.dot(vt_sc[g, j, :, ks], p.astype(BF16),
                                      preferred_element_type=F32)
                acc_sc[g, :, qs] = pv

    for g in range(n_heads):
        m_sc[g] = jnp.full((1, tq), NEG, F32)
        acc_sc[g] = jnp.zeros((VT_ROWS, tq), F32)
    produce(0, sa_sc, mxa_sc)

    def body(jj, carry):
        j = 2 * jj
        produce(j + 1, sb_sc, mxb_sc)
        consume(j, sa_sc, mxa_sc)
        produce(j + 2, sa_sc, mxa_sc)
        consume(j + 1, sb_sc, mxb_sc)
        return carry

    lax.fori_loop(0, iq // 2, body, 0)

    def finish(s_ref, mx_ref):
        key = lax.broadcasted_iota(jnp.int32, (tq, tq), 0)
        qry = lax.broadcasted_iota(jnp.int32, (tq, tq), 1)
        for g in range(n_heads):
            s = jnp.where(key <= qry, s_ref[g], NEG)
            s_ref[g] = s
            mx_ref[g] = jnp.max(s, axis=0, keepdims=True)
        consume(iq, s_ref, mx_ref, diagonal=True)
        for g, hs in enumerate(heads):
            out = acc_sc[g, :HEAD_DIM, :] / acc_sc[g, HEAD_DIM:HEAD_DIM + 1, :]
            o_ref[:, hs] = out.T.astype(o_ref.dtype)

    @pl.when(iq % 2 == 0)
    def _():
        finish(sa_sc, mxa_sc)

    @pl.when(iq % 2 == 1)
    def _():
        produce(iq, sb_sc, mxb_sc)
        consume(iq - 1, sa_sc, mxa_sc)
        finish(sb_sc, mxb_sc)


def _attn(proj3, n_heads, head0, fox_extras=None):
    b, t, _ = proj3.shape
    moba = fox_extras is None
    tq = min(ATT_TILE, t)
    g = min(ATT_HEADS, n_heads)
    assert t % tq == 0 and tq % MOBA_BLOCK == 0 and t // MOBA_BLOCK <= LANES and n_heads % g == 0
    w = g * HEAD_DIM
    c0 = head0 // g
    in_specs = [
        pl.BlockSpec((None, tq, w), lambda bi, h, i: (bi, i, c0 + h)),
        pl.BlockSpec((None, t, w), lambda bi, h, i: (bi, 0, c0 + n_heads // g + h)),
        pl.BlockSpec((None, t, w), lambda bi, h, i: (bi, 0, c0 + 2 * (n_heads // g) + h)),
    ]
    args = [proj3, proj3, proj3]
    scratch = [
        pltpu.VMEM((g, t, 2 * HEAD_DIM), BF16),
        pltpu.VMEM((g, t // tq, VT_ROWS, tq), BF16),
        pltpu.VMEM((g, tq, tq), F32),
        pltpu.VMEM((g, tq, tq), F32),
        pltpu.VMEM((g, 1, tq), F32),
        pltpu.VMEM((g, 1, tq), F32),
        pltpu.VMEM((g, 1, tq), F32),
        pltpu.VMEM((g, VT_ROWS, tq), F32),
    ]
    if moba:
        scratch.append(pltpu.VMEM((g, LANES, HEAD_DIM), F32))
    else:
        in_specs += [
            pl.BlockSpec((None, g, tq, LANES), lambda bi, h, i: (bi, h, i, 0)),
            pl.BlockSpec((None, g, t, LANES), lambda bi, h, i: (bi, h, 0, 0)),
        ]
        args += list(fox_extras)
    return pl.pallas_call(
        functools.partial(_attn_kernel, moba=moba),
        grid=(b, n_heads // g, t // tq),
        in_specs=in_specs,
        out_specs=pl.BlockSpec((None, tq, w), lambda bi, h, i: (bi, i, h)),
        out_shape=jax.ShapeDtypeStruct((b, t, n_heads * HEAD_DIM), BF16),
        scratch_shapes=scratch,
        compiler_params=_cparams("arbitrary", "arbitrary", "arbitrary"),
        name="moba_attn" if moba else "fox_attn",
    )(*args)


def _mem_kv_kernel(mem_ref, g_ref, w_ref, k_ref, v_ref):
    xn = _rmsnorm(mem_ref[...], g_ref[...]).astype(BF16)
    kv = jnp.dot(xn, w_ref[...], preferred_element_type=F32)
    half = kv.shape[1] // 2
    k_ref[...] = kv[:, :half].astype(k_ref.dtype)
    v_ref[...] = kv[:, half:].astype(v_ref.dtype)


def _mem_kv(mem, g, w_ckv):
    b, nm, d = mem.shape
    xd = w_ckv.shape[1] // 2
    out = jax.ShapeDtypeStruct((b, nm, xd), BF16)
    o_spec = pl.BlockSpec((None, nm, xd), lambda bi: (bi, 0, 0))
    return pl.pallas_call(
        _mem_kv_kernel,
        grid=(b,),
        in_specs=[
            pl.BlockSpec((None, nm, d), lambda bi: (bi, 0, 0)),
            pl.BlockSpec((1, d), lambda bi: (0, 0)),
            pl.BlockSpec((d, 2 * xd), lambda bi: (0, 0)),
        ],
        out_specs=[o_spec, o_spec],
        out_shape=[out, out],
        compiler_params=_cparams("arbitrary"),
        name="mem_kv",
    )(mem, g, w_ckv)


def _mid_kernel(x_ref, om_ref, of_ref, wom_ref, wof_ref, gx_ref, wcq_ref, kc_ref, vc_ref,
                wco_ref, gf_ref, h_ref, xn_ref):
    h1 = (x_ref[...]
          + jnp.dot(om_ref[...], wom_ref[...], preferred_element_type=F32)
          + jnp.dot(of_ref[...], wof_ref[...], preferred_element_type=F32))
    xn2 = _rmsnorm(h1, gx_ref[...]).astype(BF16)
    q = (jnp.dot(xn2, wcq_ref[...], preferred_element_type=F32) * SCALE).astype(BF16)
    outs = []
    for h in range(N_XATTN_HEADS):
        sl = slice(h * HEAD_DIM, (h + 1) * HEAD_DIM)
        s = lax.dot_general(q[:, sl], kc_ref[:, sl], _NT, preferred_element_type=F32)
        m, l, acc = _flash_init(s, vc_ref[:, sl])
        outs.append((acc / l).astype(BF16))
    oc = jnp.concatenate(outs, axis=1)
    h2 = h1 + jnp.dot(oc, wco_ref[...], preferred_element_type=F32)
    h_ref[...] = h2
    xn_ref[...] = _rmsnorm(h2, gf_ref[...]).astype(xn_ref.dtype)


def _mid(x2, o_moba, o_fox, w_om, w_of, gx, w_cq, kc, vc, w_co, gf, *, seq):
    n, d = x2.shape
    wm, wf = o_moba.shape[1], o_fox.shape[1]
    nm, xd = kc.shape[1], kc.shape[2]
    tm = min(512, seq)
    t_tiles = seq // tm
    const = lambda i: (0, 0)
    return pl.pallas_call(
        _mid_kernel,
        grid=(n // tm,),
        in_specs=[
            pl.BlockSpec((tm, d), lambda i: (i, 0)),
            pl.BlockSpec((tm, wm), lambda i: (i, 0)),
            pl.BlockSpec((tm, wf), lambda i: (i, 0)),
            pl.BlockSpec((wm, d), const),
            pl.BlockSpec((wf, d), const),
            pl.BlockSpec((1, d), const),
            pl.BlockSpec((d, xd), const),
            pl.BlockSpec((None, nm, xd), lambda i: (i // t_tiles, 0, 0)),
            pl.BlockSpec((None, nm, xd), lambda i: (i // t_tiles, 0, 0)),
            pl.BlockSpec((xd, d), const),
            pl.BlockSpec((1, d), const),
        ],
        out_specs=[
            pl.BlockSpec((tm, d), lambda i: (i, 0)),
            pl.BlockSpec((tm, d), lambda i: (i, 0)),
        ],
        out_shape=[
            jax.ShapeDtypeStruct((n, d), F32),
            jax.ShapeDtypeStruct((n, d), BF16),
        ],
        compiler_params=_cparams("arbitrary"),
        name="mid",
    )(x2, o_moba, o_fox, w_om, w_of, gx, w_cq, kc, vc, w_co, gf)


def _conv_ffn_kernel(xn_ref, h_ref, wg_ref, wu_ref, cwg_ref, cwu_ref, cbg_ref, cbu_ref,
                     wd_ref, fg_ref, o_ref, hs_sc, act_sc, tail_sc, *, tiles_per_seq):
    i = pl.program_id(0)
    f = pl.program_id(1)
    nf = pl.num_programs(1)
    tm = xn_ref.shape[0]
    tf = wg_ref.shape[1]
    halo = SUBLANES
    first = (i % tiles_per_seq) == 0

    @pl.when((i == 0) & (f == 0))
    def _():
        tail_sc[...] = jnp.zeros_like(tail_sc)

    def up():
        xn = xn_ref[...]
        prev = jnp.where(first, 0.0, tail_sc[f])
        for cols, w_ref in ((slice(0, tf), wg_ref), (slice(tf, 2 * tf), wu_ref)):
            h = jnp.dot(xn, w_ref[...], preferred_element_type=F32)
            tail_sc[f, :, cols] = h[tm - halo:, :]
            for k in range(CONV_WIDTH):
                hs_sc[k, halo + k:halo + k + tm, cols] = h
                if k:
                    hs_sc[k, halo:halo + k, cols] = prev[halo - k:, cols]

    def gate():
        rows = 64
        for c0 in range(0, tf, LANES):
            cols = slice(c0, c0 + LANES)

            def conv(base, cw_ref, cb_ref, r0):
                y = cb_ref[:, cols]
                for k in range(CONV_WIDTH):
                    tap = cw_ref[CONV_WIDTH - 1 - k:CONV_WIDTH - k, cols]
                    y = y + hs_sc[k, halo + r0:halo + r0 + rows, base + c0:base + c0 + LANES] * tap
                return y

            for r0 in range(0, tm, rows):
                gte = conv(0, cwg_ref, cbg_ref, r0)
                upv = conv(tf, cwu_ref, cbu_ref, r0)
                act_sc[r0:r0 + rows, cols] = (
                    gte * (1.0 / (1.0 + jnp.exp2(gte * (-LOG2E)))) * upv).astype(BF16)

    @pl.when(f == 0)
    def _():
        o_ref[...] = h_ref[...]

    up()
    gate()
    o_ref[...] += jnp.dot(act_sc[...], wd_ref[...], preferred_element_type=F32)

    @pl.when(f == nf - 1)
    def _():
        o_ref[...] = _rmsnorm(o_ref[...], fg_ref[...])


def _conv_ffn(xn3, h2, w_up, conv_w, conv_b, w_down, fg, *, seq):
    n, d = h2.shape
    ff = w_down.shape[0]
    tm = min(512, seq)
    tf = 512
    assert ff % tf == 0
    nf = ff // tf
    w_up = w_up.reshape(d, 2, nf, tf).transpose(2, 1, 0, 3)
    return pl.pallas_call(
        functools.partial(_conv_ffn_kernel, tiles_per_seq=seq // tm),
        grid=(n // tm, nf),
        in_specs=[
            pl.BlockSpec((tm, d), lambda i, f: (i, 0)),
            pl.BlockSpec((tm, d), lambda i, f: (i, 0)),
            pl.BlockSpec((None, None, d, tf), lambda i, f: (f, 0, 0, 0)),
            pl.BlockSpec((None, None, d, tf), lambda i, f: (f, 1, 0, 0)),
            pl.BlockSpec((CONV_WIDTH, tf), lambda i, f: (0, f)),
            pl.BlockSpec((CONV_WIDTH, tf), lambda i, f: (0, f + nf)),
            pl.BlockSpec((1, tf), lambda i, f: (0, f)),
            pl.BlockSpec((1, tf), lambda i, f: (0, f + nf)),
            pl.BlockSpec((tf, d), lambda i, f: (f, 0)),
            pl.BlockSpec((1, d), lambda i, f: (0, 0)),
        ],
        out_specs=pl.BlockSpec((tm, d), lambda i, f: (i, 0)),
        out_shape=jax.ShapeDtypeStruct((n, d), F32),
        scratch_shapes=[
            pltpu.VMEM((CONV_WIDTH, tm + 2 * SUBLANES, 2 * tf), F32),
            pltpu.VMEM((tm, tf), BF16),
            pltpu.VMEM((nf, SUBLANES, 2 * tf), F32),
        ],
        compiler_params=_cparams("arbitrary", "arbitrary"),
        name="conv_ffn",
    )(xn3, h2, w_up, w_up, conv_w, conv_w, conv_b, conv_b, w_down, fg)


def _rope_tables(t):
    half = HEAD_DIM // 2
    inv = ROPE_THETA ** (-jnp.arange(half, dtype=F32) / half)
    ang = jnp.arange(t, dtype=F32)[:, None] * inv[None, :]
    cos, sin = jnp.cos(ang), jnp.sin(ang)
    return jnp.concatenate([cos, cos], axis=1), jnp.concatenate([-sin, sin], axis=1)


def kernel(x, mem, attn_norm_g, w_in, b_f, w_o, xattn_norm_g, mem_norm_g, w_cq, w_ckv, w_co,
           ffn_norm_g, w_up, conv_w, conv_b, w_down, final_norm_g):
    b, t, d = x.shape
    assert w_in.shape[0] == 1, "one layer: the final rmsnorm is fused into the ffn kernel"
    n_heads = d // HEAD_DIM
    n_moba = n_heads // 2
    n_fox = n_heads - n_moba
    assert n_moba == n_fox and n_fox <= LANES
    wgrp = n_moba * HEAD_DIM
    main_cols = 6 * wgrp
    cos, sin = _rope_tables(t)
    row = lambda v: v.reshape(1, -1)

    w_in_t = w_in[0].T
    w_z_t = jnp.pad(w_in_t[main_cols:], ((0, LANES - n_fox), (0, 0))).astype(BF16)
    bf = row(jnp.pad(b_f[0].astype(F32), (0, LANES - n_fox)))
    xn, qx, kx = _norm_gate(x, row(attn_norm_g[0]), w_z_t, bf, n_fox)
    proj = _in_proj(xn.reshape(b * t, d), w_in_t, cos, sin, seq=t, group_width=wgrp)
    proj3 = proj.reshape(b, t, main_cols)
    o_moba = _attn(proj3, n_moba, 0)
    o_fox = _attn(proj3, n_fox, 3 * n_moba, (qx, kx))
    kc, vc = _mem_kv(mem, row(mem_norm_g[0]), w_ckv[0].astype(BF16))
    wo = w_o[0].astype(BF16)
    h2, xn3 = _mid(x.reshape(b * t, d), o_moba.reshape(b * t, wgrp), o_fox.reshape(b * t, wgrp),
                   wo[:wgrp], wo[wgrp:], row(xattn_norm_g[0]), w_cq[0].astype(BF16), kc, vc,
                   w_co[0].astype(BF16), row(ffn_norm_g[0]), seq=t)
    out = _conv_ffn(xn3, h2, w_up[0].astype(BF16), conv_w[0], row(conv_b[0]),
                    w_down[0].astype(BF16), row(final_norm_g), seq=t)
    return out.reshape(b, t, d)
```

```python
import functools
import math

import jax
import jax.numpy as jnp
from jax import lax
from jax.experimental import pallas as pl
from jax.experimental.pallas import tpu as pltpu

HEAD_DIM = 128
MOBA_BLOCK = 256
MOBA_TOPK = 3
ROPE_THETA = 10000.0
N_XATTN_HEADS = 4
CONV_WIDTH = 3
RMS_EPS = 1e-6
NEG = -1e30
SCALE = 1.0 / math.sqrt(HEAD_DIM)
LOG2E = math.log2(math.e)
QSCALE = SCALE * LOG2E

LANES = 128
SUBLANES = 8
VMEM_LIMIT_BYTES = 56 * 1024 * 1024

F32 = jnp.float32
BF16 = jnp.bfloat16

_NT = (((1,), (1,)), ((), ()))


def _cparams(*sem):
    return pltpu.CompilerParams(dimension_semantics=sem, vmem_limit_bytes=VMEM_LIMIT_BYTES)


def _rmsnorm(x, g):
    ms = jnp.mean(x * x, axis=-1, keepdims=True)
    return x * lax.rsqrt(ms + RMS_EPS) * g


def _split3(x):
    hi = x.astype(BF16)
    r = x - hi.astype(F32)
    mid = r.astype(BF16)
    lo = (r - mid.astype(F32)).astype(BF16)
    return hi, mid, lo


def _in_proj_kernel(xn_ref, w_ref, cos_ref, sin_ref, o_ref, w_sc, *, group_of_tile):
    j = pl.program_id(0)

    @pl.when(pl.program_id(1) == 0)
    def _():
        w_sc[...] = w_ref[...].astype(BF16)

    grp = group_of_tile(j)
    rot = (grp <= 1).astype(F32)
    scale = jnp.where((grp == 0) | (grp == 3), QSCALE, 1.0).astype(F32)
    a = (cos_ref[...] * rot + (1.0 - rot)) * scale
    b = sin_ref[...] * (rot * scale)
    tm, tn = o_ref.shape
    chunk = min(256, tm)
    for r0 in range(0, tm, chunk):
        rows = slice(r0, r0 + chunk)
        y = lax.dot_general(xn_ref[rows, :], w_sc[...], _NT, preferred_element_type=F32)
        for c0 in range(0, tn, HEAD_DIM):
            yh = y[:, c0:c0 + HEAD_DIM]
            out = yh * a[rows] + pltpu.roll(yh, HEAD_DIM // 2, 1) * b[rows]
            o_ref[rows, c0:c0 + HEAD_DIM] = out.astype(o_ref.dtype)


def _in_proj(xn, w_in_t, cos, sin, *, seq, group_width):
    n, d = xn.shape
    cols = 6 * group_width
    tm = min(1024, seq)
    tn = min(1024, group_width)
    tiles_per_group = group_width // tn
    t_tiles = seq // tm
    kern = functools.partial(_in_proj_kernel, group_of_tile=lambda j: j // tiles_per_group)
    return pl.pallas_call(
        kern,
        grid=(cols // tn, n // tm),
        in_specs=[
            pl.BlockSpec((tm, d), lambda j, i: (i, 0)),
            pl.BlockSpec((tn, d), lambda j, i: (j, 0)),
            pl.BlockSpec((tm, HEAD_DIM), lambda j, i: (i % t_tiles, 0)),
            pl.BlockSpec((tm, HEAD_DIM), lambda j, i: (i % t_tiles, 0)),
        ],
        out_specs=pl.BlockSpec((tm, tn), lambda j, i: (i, j)),
        out_shape=jax.ShapeDtypeStruct((n, cols), BF16),
        scratch_shapes=[pltpu.VMEM((tn, d), BF16)],
        compiler_params=_cparams("arbitrary", "arbitrary"),
        name="in_proj",
    )(xn, w_in_t, cos, sin)


def _norm_gate_kernel(x_ref, g_ref, wz_ref, bf_ref, route_ref, xn_ref, qx_ref, kx_ref, carry_sc,
                      *, n_heads):
    t = pl.program_id(1)

    @pl.when(t == 0)
    def _():
        carry_sc[...] = jnp.zeros_like(carry_sc)

    xn = _rmsnorm(x_ref[...], g_ref[...]).astype(BF16)
    xn_ref[...] = xn
    z = lax.dot_general(xn, wz_ref[...], _NT, preferred_element_type=F32) + bf_ref[...]
    lf = jnp.minimum(z, 0.0) - jnp.log(1.0 + jnp.exp(-jnp.abs(z)))

    tc = lf.shape[0]
    row = lax.broadcasted_iota(jnp.int32, (tc, tc), 0)
    col = lax.broadcasted_iota(jnp.int32, (tc, tc), 1)
    tri = (col <= row).astype(BF16)
    c = carry_sc[0:1, :]
    for part in _split3(lf):
        c = c + jnp.dot(tri, part, preferred_element_type=F32)
    carry_sc[0:1, :] = c[tc - 1:tc, :]

    parts = jnp.concatenate(_split3(c * LOG2E), axis=1)
    routed = jnp.dot(parts, route_ref[...], preferred_element_type=F32)
    lane = lax.broadcasted_iota(jnp.int32, (tc, LANES), 1)
    q_ones = ((lane >= 3) & (lane < 6)).astype(F32)
    k_ones = (lane < 3).astype(F32)
    for h in range(n_heads):
        qx_ref[h] = (routed[:, (2 * h) * LANES:(2 * h + 1) * LANES] + q_ones).astype(BF16)
        kx_ref[h] = (routed[:, (2 * h + 1) * LANES:(2 * h + 2) * LANES] + k_ones).astype(BF16)


def _gate_routing(n_heads):
    import numpy as np
    r = np.zeros((3 * LANES, n_heads * 2 * LANES), np.float32)
    for h in range(n_heads):
        for p in range(3):
            r[p * LANES + h, (2 * h) * LANES + p] = 1.0
            r[p * LANES + h, (2 * h + 1) * LANES + 3 + p] = -1.0
    return jnp.asarray(r, BF16)


def _norm_gate(x3, g, w_z, b_f, n_heads):
    b, t, d = x3.shape
    tc = min(512, t)
    ext = jax.ShapeDtypeStruct((b, n_heads, t, LANES), BF16)
    ext_spec = pl.BlockSpec((None, n_heads, tc, LANES), lambda bi, ti: (bi, 0, ti, 0))
    const = lambda bi, ti: (0, 0)
    return pl.pallas_call(
        functools.partial(_norm_gate_kernel, n_heads=n_heads),
        grid=(b, t // tc),
        in_specs=[
            pl.BlockSpec((None, tc, d), lambda bi, ti: (bi, ti, 0)),
            pl.BlockSpec((1, d), const),
            pl.BlockSpec((LANES, d), const),
            pl.BlockSpec((1, LANES), const),
            pl.BlockSpec((3 * LANES, n_heads * 2 * LANES), const),
        ],
        out_specs=[pl.BlockSpec((None, tc, d), lambda bi, ti: (bi, ti, 0)), ext_spec, ext_spec],
        out_shape=[jax.ShapeDtypeStruct((b, t, d), BF16), ext, ext],
        scratch_shapes=[pltpu.VMEM((SUBLANES, LANES), F32)],
        compiler_params=_cparams("arbitrary", "arbitrary"),
        name="norm_gate",
    )(x3, g, w_z, b_f, _gate_routing(n_heads))


ATT_TILE = 512
ATT_HEADS = 4
VT_ROWS = HEAD_DIM + 16


def _flash_init(s, v):
    m = jnp.max(s, axis=1, keepdims=True)
    p = jnp.exp(s - m)
    l = jnp.sum(p, axis=1, keepdims=True)
    acc = jnp.dot(p.astype(BF16), v, preferred_element_type=F32)
    return m, l, acc


def _moba_bias(q, kmean, iq):
    tq = q.shape[0]
    nb = kmean.shape[0]
    km_hi = kmean.astype(BF16)
    km_lo = (kmean - km_hi.astype(F32)).astype(BF16)
    gate = (lax.dot_general(km_hi, q, _NT, preferred_element_type=F32)
            + lax.dot_general(km_lo, q, _NT, preferred_element_type=F32))
    blk_i = lax.broadcasted_iota(jnp.int32, gate.shape, 0)
    qry_i = lax.broadcasted_iota(jnp.int32, gate.shape, 1)
    own = iq * (tq // MOBA_BLOCK) + qry_i // MOBA_BLOCK
    blk_f = blk_i.astype(F32)
    past = blk_i < own
    g = jnp.where(past, gate, -jnp.inf)
    sel = jnp.zeros(gate.shape, jnp.bool_)
    for _ in range(MOBA_TOPK):
        mx = jnp.max(g, axis=0, keepdims=True)
        idx = jnp.min(jnp.where(g == mx, blk_f, float(LANES)), axis=0, keepdims=True)
        pick = blk_f == idx
        sel = sel | pick
        g = jnp.where(pick, -jnp.inf, g)
    bias_t = jnp.where((sel & past) | (blk_i == own), 0.0, NEG)
    bias_t = jnp.concatenate([bias_t, jnp.zeros((LANES - nb, tq), F32)], axis=0)
    return bias_t.T.astype(BF16)


def _attn_kernel(*refs, moba):
    if moba:
        q_ref, k_ref, v_ref, o_ref = refs[:4]
        kmean_sc = refs[-1]
        refs = refs[:-1]
    else:
        q_ref, k_ref, v_ref, qx_ref, kx_ref, o_ref = refs[:6]
    kaug_sc, vt_sc, sa_sc, sb_sc, mxa_sc, mxb_sc, m_sc, acc_sc = refs[-8:]
    iq = pl.program_id(2)
    tq = q_ref.shape[0]
    t = k_ref.shape[0]
    n_heads = q_ref.shape[1] // HEAD_DIM
    heads = [slice(g * HEAD_DIM, (g + 1) * HEAD_DIM) for g in range(n_heads)]

    @pl.when(iq == 0)
    def _():
        for g, hs in enumerate(heads):
            kaug_sc[g, :, :HEAD_DIM] = k_ref[:, hs]
            if moba:
                row = lax.broadcasted_iota(jnp.int32, (t, LANES), 0)
                lane = lax.broadcasted_iota(jnp.int32, (t, LANES), 1)
                kaug_sc[g, :, HEAD_DIM:] = ((row // MOBA_BLOCK) == lane).astype(BF16)
                kmean_sc[g] = jnp.zeros((LANES, HEAD_DIM), F32)
                for n in range(t // MOBA_BLOCK):
                    kb = k_ref[n * MOBA_BLOCK:(n + 1) * MOBA_BLOCK, hs].astype(F32)
                    kmean_sc[g, n:n + 1, :] = jnp.mean(kb, axis=0, keepdims=True)
            else:
                kaug_sc[g, :, HEAD_DIM:] = kx_ref[g]
            pad_row = lax.broadcasted_iota(jnp.int32, (VT_ROWS - HEAD_DIM, tq), 0)
            for j in range(t // tq):
                vt_sc[g, j, :HEAD_DIM, :] = (
                    v_ref[j * tq:(j + 1) * tq, hs].astype(F32).T.astype(BF16))
                vt_sc[g, j, HEAD_DIM:, :] = (pad_row == 0).astype(BF16)

    qa = []
    for g, hs in enumerate(heads):
        q = q_ref[:, hs]
        if moba:
            nb = -(-(t // MOBA_BLOCK) // 16) * 16
            extra = _moba_bias(q, kmean_sc[g, :nb, :], iq)
        else:
            extra = qx_ref[g]
        qa.append(jnp.concatenate([q, extra], axis=1))

    def scores(g, j):
        off = pl.multiple_of(j * tq, tq)
        return lax.dot_general(kaug_sc[g, pl.ds(off, tq), :], qa[g], _NT,
                               preferred_element_type=F32)

    half = tq // 2
    halves = (slice(0, half), slice(half, tq))

    def produce(j, s_ref, mx_ref):
        for g in range(n_heads):
            s = scores(g, j)
            s_ref[g] = s
            mx_ref[g] = jnp.max(s, axis=0, keepdims=True)

    def consume(j, s_ref, mx_ref, diagonal=False):
        for g in range(n_heads):
            m = m_sc[g]
            m_new = jnp.maximum(m, mx_ref[g])
            alpha = jnp.exp2(m - m_new)
            m_sc[g] = m_new
            for qi, qs in enumerate(halves):
                pv = alpha[:, qs] * acc_sc[g, :, qs]
                for ki, ks in enumerate(halves):
                    if diagonal and ki > qi:
                        continue
                    p = jnp.exp2(s_ref[g, ks, qs] - m_new[:, qs])
                    pv = pv + jnp.dot(vt_sc[g, j, :, ks], p.astype(BF16),
                                      preferred_element_type=F32)
                acc_sc[g, :, qs] = pv

    for g in range(n_heads):
        m_sc[g] = jnp.full((1, tq), NEG, F32)
        acc_sc[g] = jnp.zeros((VT_ROWS, tq), F32)
    produce(0, sa_sc, mxa_sc)

    def body(jj, carry):
        j = 2 * jj
        produce(j + 1, sb_sc, mxb_sc)
        consume(j, sa_sc, mxa_sc)
        produce(j + 2, sa_sc, mxa_sc)
        consume(j + 1, sb_sc, mxb_sc)
        return carry

    lax.fori_loop(0, iq // 2, body, 0)

    def finish(s_ref, mx_ref):
        key = lax.broadcasted_iota(jnp.int32, (tq, tq), 0)
        qry = lax.broadcasted_iota(jnp.int32, (tq, tq), 1)
        for g in range(n_heads):
            s = jnp.where(key <= qry, s_ref[g], NEG)
            s_ref[g] = s
            mx_ref[g] = jnp.max(s, axis=0, keepdims=True)
        consume(iq, s_ref, mx_ref, diagonal=True)
        for g, hs in enumerate(heads):
            out = acc_sc[g, :HEAD_DIM, :] / acc_sc[g, HEAD_DIM:HEAD_DIM + 1, :]
            o_ref[:, hs] = out.T.astype(o_ref.dtype)

    @pl.when(iq % 2 == 0)
    def _():
        finish(sa_sc, mxa_sc)

    @pl.when(iq % 2 == 1)
    def _():
        produce(iq, sb_sc, mxb_sc)
        consume(iq - 1, sa_sc, mxa_sc)
        finish(sb_sc, mxb_sc)


def _attn(proj3, n_heads, head0, fox_extras=None):
    b, t, _ = proj3.shape
    moba = fox_extras is None
    tq = min(ATT_TILE, t)
    g = min(ATT_HEADS, n_heads)
    assert t % tq == 0 and tq % MOBA_BLOCK == 0 and t // MOBA_BLOCK <= LANES and n_heads % g == 0
    w = g * HEAD_DIM
    c0 = head0 // g
    in_specs = [
        pl.BlockSpec((None, tq, w), lambda bi, h, i: (bi, i, c0 + h)),
        pl.BlockSpec((None, t, w), lambda bi, h, i: (bi, 0, c0 + n_heads // g + h)),
        pl.BlockSpec((None, t, w), lambda bi, h, i: (bi, 0, c0 + 2 * (n_heads // g) + h)),
    ]
    args = [proj3, proj3, proj3]
    scratch = [
        pltpu.VMEM((g, t, 2 * HEAD_DIM), BF16),
        pltpu.VMEM((g, t // tq, VT_ROWS, tq), BF16),
        pltpu.VMEM((g, tq, tq), F32),
        pltpu.VMEM((g, tq, tq), F32),
        pltpu.VMEM((g, 1, tq), F32),
        pltpu.VMEM((g, 1, tq), F32),
        pltpu.VMEM((g, 1, tq), F32),
        pltpu.VMEM((g, VT_ROWS, tq), F32),
    ]
    if moba:
        scratch.append(pltpu.VMEM((g, LANES, HEAD_DIM), F32))
    else:
        in_specs += [
            pl.BlockSpec((None, g, tq, LANES), lambda bi, h, i: (bi, h, i, 0)),
            pl.BlockSpec((None, g, t, LANES), lambda bi, h, i: (bi, h, 0, 0)),
        ]
        args += list(fox_extras)
    return pl.pallas_call(
        functools.partial(_attn_kernel, moba=moba),
        grid=(b, n_heads // g, t // tq),
        in_specs=in_specs,
        out_specs=pl.BlockSpec((None, tq, w), lambda bi, h, i: (bi, i, h)),
        out_shape=jax.ShapeDtypeStruct((b, t, n_heads * HEAD_DIM), BF16),
        scratch_shapes=scratch,
        compiler_params=_cparams("arbitrary", "arbitrary", "arbitrary"),
        name="moba_attn" if moba else "fox_attn",
    )(*args)


def _mem_kv_kernel(mem_ref, g_ref, w_ref, k_ref, v_ref):
    xn = _rmsnorm(mem_ref[...], g_ref[...]).astype(BF16)
    kv = jnp.dot(xn, w_ref[...], preferred_element_type=F32)
    half = kv.shape[1] // 2
    k_ref[...] = kv[:, :half].astype(k_ref.dtype)
    v_ref[...] = kv[:, half:].astype(v_ref.dtype)


def _mem_kv(mem, g, w_ckv):
    b, nm, d = mem.shape
    xd = w_ckv.shape[1] // 2
    out = jax.ShapeDtypeStruct((b, nm, xd), BF16)
    o_spec = pl.BlockSpec((None, nm, xd), lambda bi: (bi, 0, 0))
    return pl.pallas_call(
        _mem_kv_kernel,
        grid=(b,),
        in_specs=[
            pl.BlockSpec((None, nm, d), lambda bi: (bi, 0, 0)),
            pl.BlockSpec((1, d), lambda bi: (0, 0)),
            pl.BlockSpec((d, 2 * xd), lambda bi: (0, 0)),
        ],
        out_specs=[o_spec, o_spec],
        out_shape=[out, out],
        compiler_params=_cparams("arbitrary"),
        name="mem_kv",
    )(mem, g, w_ckv)


def _mid_kernel(x_ref, om_ref, of_ref, wom_ref, wof_ref, gx_ref, wcq_ref, kc_ref, vc_ref,
                wco_ref, gf_ref, h_ref, xn_ref):
    h1 = (x_ref[...]
          + jnp.dot(om_ref[...], wom_ref[...], preferred_element_type=F32)
          + jnp.dot(of_ref[...], wof_ref[...], preferred_element_type=F32))
    xn2 = _rmsnorm(h1, gx_ref[...]).astype(BF16)
    q = (jnp.dot(xn2, wcq_ref[...], preferred_element_type=F32) * SCALE).astype(BF16)
    outs = []
    for h in range(N_XATTN_HEADS):
        sl = slice(h * HEAD_DIM, (h + 1) * HEAD_DIM)
        s = lax.dot_general(q[:, sl], kc_ref[:, sl], _NT, preferred_element_type=F32)
        m, l, acc = _flash_init(s, vc_ref[:, sl])
        outs.append((acc / l).astype(BF16))
    oc = jnp.concatenate(outs, axis=1)
    h2 = h1 + jnp.dot(oc, wco_ref[...], preferred_element_type=F32)
    h_ref[...] = h2
    xn_ref[...] = _rmsnorm(h2, gf_ref[...]).astype(xn_ref.dtype)


def _mid(x2, o_moba, o_fox, w_om, w_of, gx, w_cq, kc, vc, w_co, gf, *, seq):
    n, d = x2.shape
    wm, wf = o_moba.shape[1], o_fox.shape[1]
    nm, xd = kc.shape[1], kc.shape[2]
    tm = min(512, seq)
    t_tiles = seq // tm
    const = lambda i: (0, 0)
    return pl.pallas_call(
        _mid_kernel,
        grid=(n // tm,),
        in_specs=[
            pl.BlockSpec((tm, d), lambda i: (i, 0)),
            pl.BlockSpec((tm, wm), lambda i: (i, 0)),
            pl.BlockSpec((tm, wf), lambda i: (i, 0)),
            pl.BlockSpec((wm, d), const),
            pl.BlockSpec((wf, d), const),
            pl.BlockSpec((1, d), const),
            pl.BlockSpec((d, xd), const),
            pl.BlockSpec((None, nm, xd), lambda i: (i // t_tiles, 0, 0)),
            pl.BlockSpec((None, nm, xd), lambda i: (i // t_tiles, 0, 0)),
            pl.BlockSpec((xd, d), const),
            pl.BlockSpec((1, d), const),
        ],
        out_specs=[
            pl.BlockSpec((tm, d), lambda i: (i, 0)),
            pl.BlockSpec((tm, d), lambda i: (i, 0)),
        ],
        out_shape=[
            jax.ShapeDtypeStruct((n, d), F32),
            jax.ShapeDtypeStruct((n, d), BF16),
        ],
        compiler_params=_cparams("arbitrary"),
        name="mid",
    )(x2, o_moba, o_fox, w_om, w_of, gx, w_cq, kc, vc, w_co, gf)


def _conv_ffn_kernel(xn_ref, h_ref, wg_ref, wu_ref, cwg_ref, cwu_ref, cbg_ref, cbu_ref,
                     wd_ref, fg_ref, o_ref, hs_sc, tail_sc, *, tiles_per_seq):
    i = pl.program_id(0)
    f = pl.program_id(1)
    nf = pl.num_programs(1)
    tm = xn_ref.shape[0]
    tf = wg_ref.shape[1]
    halo = SUBLANES
    first = (i % tiles_per_seq) == 0

    @pl.when((i == 0) & (f == 0))
    def _():
        tail_sc[...] = jnp.zeros_like(tail_sc)

    def up():
        xn = xn_ref[...]
        hs_sc[0:halo, :] = jnp.where(first, 0.0, tail_sc[f])
        hs_sc[halo:, :tf] = jnp.dot(xn, wg_ref[...], preferred_element_type=F32)
        hs_sc[halo:, tf:] = jnp.dot(xn, wu_ref[...], preferred_element_type=F32)
        tail_sc[f] = hs_sc[tm:tm + halo, :]

    def gate():
        cw = jnp.concatenate([cwg_ref[...], cwu_ref[...]], axis=1)
        cb = jnp.concatenate([cbg_ref[...], cbu_ref[...]], axis=1)
        y = cb
        for k in range(CONV_WIDTH):
            lo = halo - (CONV_WIDTH - 1) + k
            y = y + hs_sc[lo:lo + tm, :] * cw[k:k + 1, :]
        gte = y[:, :tf]
        return (gte * (1.0 / (1.0 + jnp.exp2(gte * (-LOG2E)))) * y[:, tf:]).astype(BF16)

    @pl.when(f == 0)
    def _():
        o_ref[...] = h_ref[...]

    up()
    o_ref[...] += jnp.dot(gate(), wd_ref[...], preferred_element_type=F32)

    @pl.when(f == nf - 1)
    def _():
        o_ref[...] = _rmsnorm(o_ref[...], fg_ref[...])


def _conv_ffn(xn3, h2, w_up, conv_w, conv_b, w_down, fg, *, seq):
    n, d = h2.shape
    ff = w_down.shape[0]
    tm = min(512, seq)
    tf = 512
    assert ff % tf == 0
    nf = ff // tf
    return pl.pallas_call(
        functools.partial(_conv_ffn_kernel, tiles_per_seq=seq // tm),
        grid=(n // tm, nf),
        in_specs=[
            pl.BlockSpec((tm, d), lambda i, f: (i, 0)),
            pl.BlockSpec((tm, d), lambda i, f: (i, 0)),
            pl.BlockSpec((d, tf), lambda i, f: (0, f)),
            pl.BlockSpec((d, tf), lambda i, f: (0, f + nf)),
            pl.BlockSpec((CONV_WIDTH, tf), lambda i, f: (0, f)),
            pl.BlockSpec((CONV_WIDTH, tf), lambda i, f: (0, f + nf)),
            pl.BlockSpec((1, tf), lambda i, f: (0, f)),
            pl.BlockSpec((1, tf), lambda i, f: (0, f + nf)),
            pl.BlockSpec((tf, d), lambda i, f: (f, 0)),
            pl.BlockSpec((1, d), lambda i, f: (0, 0)),
        ],
        out_specs=pl.BlockSpec((tm, d), lambda i, f: (i, 0)),
        out_shape=jax.ShapeDtypeStruct((n, d), F32),
        scratch_shapes=[
            pltpu.VMEM((tm + SUBLANES, 2 * tf), F32),
            pltpu.VMEM((nf, SUBLANES, 2 * tf), F32),
        ],
        compiler_params=_cparams("arbitrary", "arbitrary"),
        name="conv_ffn",
    )(xn3, h2, w_up, w_up, conv_w, conv_w, conv_b, conv_b, w_down, fg)


def _rope_tables(t):
    half = HEAD_DIM // 2
    inv = ROPE_THETA ** (-jnp.arange(half, dtype=F32) / half)
    ang = jnp.arange(t, dtype=F32)[:, None] * inv[None, :]
    cos, sin = jnp.cos(ang), jnp.sin(ang)
    return jnp.concatenate([cos, cos], axis=1), jnp.concatenate([-sin, sin], axis=1)


def kernel(x, mem, attn_norm_g, w_in, b_f, w_o, xattn_norm_g, mem_norm_g, w_cq, w_ckv, w_co,
           ffn_norm_g, w_up, conv_w, conv_b, w_down, final_norm_g):
    b, t, d = x.shape
    assert w_in.shape[0] == 1, "one layer: the final rmsnorm is fused into the ffn kernel"
    n_heads = d // HEAD_DIM
    n_moba = n_heads // 2
    n_fox = n_heads - n_moba
    assert n_moba == n_fox and n_fox <= LANES
    wgrp = n_moba * HEAD_DIM
    main_cols = 6 * wgrp
    cos, sin = _rope_tables(t)
    row = lambda v: v.reshape(1, -1)

    w_in_t = w_in[0].T
    w_z_t = jnp.pad(w_in_t[main_cols:], ((0, LANES - n_fox), (0, 0))).astype(BF16)
    bf = row(jnp.pad(b_f[0].astype(F32), (0, LANES - n_fox)))
    xn, qx, kx = _norm_gate(x, row(attn_norm_g[0]), w_z_t, bf, n_fox)
    proj = _in_proj(xn.reshape(b * t, d), w_in_t, cos, sin, seq=t, group_width=wgrp)
    proj3 = proj.reshape(b, t, main_cols)
    o_moba = _attn(proj3, n_moba, 0)
    o_fox = _attn(proj3, n_fox, 3 * n_moba, (qx, kx))
    kc, vc = _mem_kv(mem, row(mem_norm_g[0]), w_ckv[0].astype(BF16))
    wo = w_o[0].astype(BF16)
    h2, xn3 = _mid(x.reshape(b * t, d), o_moba.reshape(b * t, wgrp), o_fox.reshape(b * t, wgrp),
                   wo[:wgrp], wo[wgrp:], row(xattn_norm_g[0]), w_cq[0].astype(BF16), kc, vc,
                   w_co[0].astype(BF16), row(ffn_norm_g[0]), seq=t)
    out = _conv_ffn(xn3, h2, w_up[0].astype(BF16), conv_w[0], row(conv_b[0]),
                    w_down[0].astype(BF16), row(final_norm_g), seq=t)
    return out.reshape(b, t, d)
```

```python
import functools
import math

import jax
import jax.numpy as jnp
from jax import lax
from jax.experimental import pallas as pl
from jax.experimental.pallas import tpu as pltpu

HEAD_DIM = 128
MOBA_BLOCK = 256
MOBA_TOPK = 3
ROPE_THETA = 10000.0
N_XATTN_HEADS = 4
CONV_WIDTH = 3
RMS_EPS = 1e-6
NEG = -1e30
SCALE = 1.0 / math.sqrt(HEAD_DIM)
LOG2E = math.log2(math.e)
QSCALE = SCALE * LOG2E

LANES = 128
SUBLANES = 8
VMEM_LIMIT_BYTES = 56 * 1024 * 1024

F32 = jnp.float32
BF16 = jnp.bfloat16

_NT = (((1,), (1,)), ((), ()))


def _cparams(*sem):
    return pltpu.CompilerParams(dimension_semantics=sem, vmem_limit_bytes=VMEM_LIMIT_BYTES)


def _rmsnorm(x, g):
    ms = jnp.mean(x * x, axis=-1, keepdims=True)
    return x * lax.rsqrt(ms + RMS_EPS) * g


def _split3(x):
    hi = x.astype(BF16)
    r = x - hi.astype(F32)
    mid = r.astype(BF16)
    lo = (r - mid.astype(F32)).astype(BF16)
    return hi, mid, lo


def _in_proj_kernel(xn_ref, w_ref, cos_ref, sin_ref, o_ref, w_sc, *, group_of_tile):
    j = pl.program_id(0)

    @pl.when(pl.program_id(1) == 0)
    def _():
        w_sc[...] = w_ref[...].astype(BF16)

    grp = group_of_tile(j)
    rot = (grp <= 1).astype(F32)
    scale = jnp.where((grp == 0) | (grp == 3), QSCALE, 1.0).astype(F32)
    a = (cos_ref[...] * rot + (1.0 - rot)) * scale
    b = sin_ref[...] * (rot * scale)
    tm, tn = o_ref.shape
    chunk = min(256, tm)
    for r0 in range(0, tm, chunk):
        rows = slice(r0, r0 + chunk)
        y = lax.dot_general(xn_ref[rows, :], w_sc[...], _NT, preferred_element_type=F32)
        for c0 in range(0, tn, HEAD_DIM):
            yh = y[:, c0:c0 + HEAD_DIM]
            out = yh * a[rows] + pltpu.roll(yh, HEAD_DIM // 2, 1) * b[rows]
            o_ref[rows, c0:c0 + HEAD_DIM] = out.astype(o_ref.dtype)


def _in_proj(xn, w_in_t, cos, sin, *, seq, group_width):
    n, d = xn.shape
    cols = 6 * group_width
    tm = min(1024, seq)
    tn = min(1024, group_width)
    tiles_per_group = group_width // tn
    t_tiles = seq // tm
    kern = functools.partial(_in_proj_kernel, group_of_tile=lambda j: j // tiles_per_group)
    return pl.pallas_call(
        kern,
        grid=(cols // tn, n // tm),
        in_specs=[
            pl.BlockSpec((tm, d), lambda j, i: (i, 0)),
            pl.BlockSpec((tn, d), lambda j, i: (j, 0)),
            pl.BlockSpec((tm, HEAD_DIM), lambda j, i: (i % t_tiles, 0)),
            pl.BlockSpec((tm, HEAD_DIM), lambda j, i: (i % t_tiles, 0)),
        ],
        out_specs=pl.BlockSpec((tm, tn), lambda j, i: (i, j)),
        out_shape=jax.ShapeDtypeStruct((n, cols), BF16),
        scratch_shapes=[pltpu.VMEM((tn, d), BF16)],
        compiler_params=_cparams("arbitrary", "arbitrary"),
        name="in_proj",
    )(xn, w_in_t, cos, sin)


def _norm_gate_kernel(x_ref, g_ref, wz_ref, bf_ref, route_ref, xn_ref, qx_ref, kx_ref, carry_sc,
                      *, n_heads):
    t = pl.program_id(1)

    @pl.when(t == 0)
    def _():
        carry_sc[...] = jnp.zeros_like(carry_sc)

    xn = _rmsnorm(x_ref[...], g_ref[...]).astype(BF16)
    xn_ref[...] = xn
    z = lax.dot_general(xn, wz_ref[...], _NT, preferred_element_type=F32) + bf_ref[...]
    lf = jnp.minimum(z, 0.0) - jnp.log(1.0 + jnp.exp(-jnp.abs(z)))

    tc = lf.shape[0]
    row = lax.broadcasted_iota(jnp.int32, (tc, tc), 0)
    col = lax.broadcasted_iota(jnp.int32, (tc, tc), 1)
    tri = (col <= row).astype(BF16)
    c = carry_sc[0:1, :]
    for part in _split3(lf):
        c = c + jnp.dot(tri, part, preferred_element_type=F32)
    carry_sc[0:1, :] = c[tc - 1:tc, :]

    parts = jnp.concatenate(_split3(c * LOG2E), axis=1)
    routed = jnp.dot(parts, route_ref[...], preferred_element_type=F32)
    lane = lax.broadcasted_iota(jnp.int32, (tc, LANES), 1)
    q_ones = ((lane >= 3) & (lane < 6)).astype(F32)
    k_ones = (lane < 3).astype(F32)
    for h in range(n_heads):
        qx_ref[h] = (routed[:, (2 * h) * LANES:(2 * h + 1) * LANES] + q_ones).astype(BF16)
        kx_ref[h] = (routed[:, (2 * h + 1) * LANES:(2 * h + 2) * LANES] + k_ones).astype(BF16)


def _gate_routing(n_heads):
    import numpy as np
    r = np.zeros((3 * LANES, n_heads * 2 * LANES), np.float32)
    for h in range(n_heads):
        for p in range(3):
            r[p * LANES + h, (2 * h) * LANES + p] = 1.0
            r[p * LANES + h, (2 * h + 1) * LANES + 3 + p] = -1.0
    return jnp.asarray(r, BF16)


def _norm_gate(x3, g, w_z, b_f, n_heads):
    b, t, d = x3.shape
    tc = min(512, t)
    ext = jax.ShapeDtypeStruct((b, n_heads, t, LANES), BF16)
    ext_spec = pl.BlockSpec((None, n_heads, tc, LANES), lambda bi, ti: (bi, 0, ti, 0))
    const = lambda bi, ti: (0, 0)
    return pl.pallas_call(
        functools.partial(_norm_gate_kernel, n_heads=n_heads),
        grid=(b, t // tc),
        in_specs=[
            pl.BlockSpec((None, tc, d), lambda bi, ti: (bi, ti, 0)),
            pl.BlockSpec((1, d), const),
            pl.BlockSpec((LANES, d), const),
            pl.BlockSpec((1, LANES), const),
            pl.BlockSpec((3 * LANES, n_heads * 2 * LANES), const),
        ],
        out_specs=[pl.BlockSpec((None, tc, d), lambda bi, ti: (bi, ti, 0)), ext_spec, ext_spec],
        out_shape=[jax.ShapeDtypeStruct((b, t, d), BF16), ext, ext],
        scratch_shapes=[pltpu.VMEM((SUBLANES, LANES), F32)],
        compiler_params=_cparams("arbitrary", "arbitrary"),
        name="norm_gate",
    )(x3, g, w_z, b_f, _gate_routing(n_heads))


ATT_TILE = 512
ATT_HEADS = 4
VT_ROWS = HEAD_DIM + 16


def _flash_init(s, v):
    m = jnp.max(s, axis=1, keepdims=True)
    p = jnp.exp(s - m)
    l = jnp.sum(p, axis=1, keepdims=True)
    acc = jnp.dot(p.astype(BF16), v, preferred_element_type=F32)
    return m, l, acc


def _moba_bias(q, kmean, iq):
    tq = q.shape[0]
    nb = kmean.shape[0]
    km_hi = kmean.astype(BF16)
    km_lo = (kmean - km_hi.astype(F32)).astype(BF16)
    gate = (lax.dot_general(km_hi, q, _NT, preferred_element_type=F32)
            + lax.dot_general(km_lo, q, _NT, preferred_element_type=F32))
    blk_i = lax.broadcasted_iota(jnp.int32, gate.shape, 0)
    qry_i = lax.broadcasted_iota(jnp.int32, gate.shape, 1)
    own = iq * (tq // MOBA_BLOCK) + qry_i // MOBA_BLOCK
    blk_f = blk_i.astype(F32)
    past = blk_i < own
    g = jnp.where(past, gate, -jnp.inf)
    sel = jnp.zeros(gate.shape, jnp.bool_)
    for _ in range(MOBA_TOPK):
        mx = jnp.max(g, axis=0, keepdims=True)
        idx = jnp.min(jnp.where(g == mx, blk_f, float(LANES)), axis=0, keepdims=True)
        pick = blk_f == idx
        sel = sel | pick
        g = jnp.where(pick, -jnp.inf, g)
    bias_t = jnp.where((sel & past) | (blk_i == own), 0.0, NEG)
    bias_t = jnp.concatenate([bias_t, jnp.zeros((LANES - nb, tq), F32)], axis=0)
    return bias_t.T.astype(BF16)


def _attn_kernel(*refs, moba):
    if moba:
        q_ref, k_ref, v_ref, o_ref = refs[:4]
        kmean_sc = refs[-1]
        refs = refs[:-1]
    else:
        q_ref, k_ref, v_ref, qx_ref, kx_ref, o_ref = refs[:6]
    kaug_sc, vt_sc, sa_sc, sb_sc, mxa_sc, mxb_sc, m_sc, acc_sc = refs[-8:]
    iq = pl.program_id(2)
    tq = q_ref.shape[0]
    t = k_ref.shape[0]
    n_heads = q_ref.shape[1] // HEAD_DIM
    heads = [slice(g * HEAD_DIM, (g + 1) * HEAD_DIM) for g in range(n_heads)]

    @pl.when(iq == 0)
    def _():
        for g, hs in enumerate(heads):
            kaug_sc[g, :, :HEAD_DIM] = k_ref[:, hs]
            if moba:
                row = lax.broadcasted_iota(jnp.int32, (t, LANES), 0)
                lane = lax.broadcasted_iota(jnp.int32, (t, LANES), 1)
                kaug_sc[g, :, HEAD_DIM:] = ((row // MOBA_BLOCK) == lane).astype(BF16)
                kmean_sc[g] = jnp.zeros((LANES, HEAD_DIM), F32)
                for n in range(t // MOBA_BLOCK):
                    kb = k_ref[n * MOBA_BLOCK:(n + 1) * MOBA_BLOCK, hs].astype(F32)
                    kmean_sc[g, n:n + 1, :] = jnp.mean(kb, axis=0, keepdims=True)
            else:
                kaug_sc[g, :, HEAD_DIM:] = kx_ref[g]
            pad_row = lax.broadcasted_iota(jnp.int32, (VT_ROWS - HEAD_DIM, tq), 0)
            for j in range(t // tq):
                vt_sc[g, j, :HEAD_DIM, :] = (
                    v_ref[j * tq:(j + 1) * tq, hs].astype(F32).T.astype(BF16))
                vt_sc[g, j, HEAD_DIM:, :] = (pad_row == 0).astype(BF16)

    qa = []
    for g, hs in enumerate(heads):
        q = q_ref[:, hs]
        if moba:
            nb = -(-(t // MOBA_BLOCK) // 16) * 16
            extra = _moba_bias(q, kmean_sc[g, :nb, :], iq)
        else:
            extra = qx_ref[g]
        qa.append(jnp.concatenate([q, extra], axis=1))

    def scores(g, j):
        off = pl.multiple_of(j * tq, tq)
        return lax.dot_general(kaug_sc[g, pl.ds(off, tq), :], qa[g], _NT,
                               preferred_element_type=F32)

    half = tq // 2
    halves = (slice(0, half), slice(half, tq))

    def produce(j, s_ref, mx_ref):
        for g in range(n_heads):
            s = scores(g, j)
            s_ref[g] = s
            mx_ref[g] = jnp.max(s, axis=0, keepdims=True)

    def consume(j, s_ref, mx_ref, diagonal=False):
        for g in range(n_heads):
            m = m_sc[g]
            m_new = jnp.maximum(m, mx_ref[g])
            alpha = jnp.exp2(m - m_new)
            m_sc[g] = m_new
            for qi, qs in enumerate(halves):
                pv = alpha[:, qs] * acc_sc[g, :, qs]
                for ki, ks in enumerate(halves):
                    if diagonal and ki > qi:
                        continue
                    p = jnp.exp2(s_ref[g, ks, qs] - m_new[:, qs])
                    pv = pv + jnp.dot(vt_sc[g, j, :, ks], p.astype(BF16),
                                      preferred_element_type=F32)
                acc_sc[g, :, qs] = pv

    for g in range(n_heads):
        m_sc[g] = jnp.full((1, tq), NEG, F32)
        acc_sc[g] = jnp.zeros((VT_ROWS, tq), F32)
    produce(0, sa_sc, mxa_sc)

    def body(jj, carry):
        j = 2 * jj
        produce(j + 1, sb_sc, mxb_sc)
        consume(j, sa_sc, mxa_sc)
        produce(j + 2, sa_sc, mxa_sc)
        consume(j + 1, sb_sc, mxb_sc)
        return carry

    lax.fori_loop(0, iq // 2, body, 0)

    def finish(s_ref, mx_ref):
        key = lax.broadcasted_iota(jnp.int32, (tq, tq), 0)
        qry = lax.broadcasted_iota(jnp.int32, (tq, tq), 1)
        for g in range(n_heads):
            s = jnp.where(key <= qry, s_ref[g], NEG)
            s_ref[g] = s
            mx_ref[g] = jnp.max(s, axis=0, keepdims=True)
        consume(iq, s_ref, mx_ref, diagonal=True)
        for g, hs in enumerate(heads):
            out = acc_sc[g, :HEAD_DIM, :] / acc_sc[g, HEAD_DIM:HEAD_DIM + 1, :]
            o_ref[:, hs] = out.T.astype(o_ref.dtype)

    @pl.when(iq % 2 == 0)
    def _():
        finish(sa_sc, mxa_sc)

    @pl.when(iq % 2 == 1)
    def _():
        produce(iq, sb_sc, mxb_sc)
        consume(iq - 1, sa_sc, mxa_sc)
        finish(sb_sc, mxb_sc)


def _attn(proj3, n_heads, head0, fox_extras=None):
    b, t, _ = proj3.shape
    moba = fox_extras is None
    tq = min(ATT_TILE, t)
    g = min(ATT_HEADS, n_heads)
    assert t % tq == 0 and tq % MOBA_BLOCK == 0 and t // MOBA_BLOCK <= LANES and n_heads % g == 0
    w = g * HEAD_DIM
    c0 = head0 // g
    in_specs = [
        pl.BlockSpec((None, tq, w), lambda bi, h, i: (bi, i, c0 + h)),
        pl.BlockSpec((None, t, w), lambda bi, h, i: (bi, 0, c0 + n_heads // g + h)),
        pl.BlockSpec((None, t, w), lambda bi, h, i: (bi, 0, c0 + 2 * (n_heads // g) + h)),
    ]
    args = [proj3, proj3, proj3]
    scratch = [
        pltpu.VMEM((g, t, 2 * HEAD_DIM), BF16),
        pltpu.VMEM((g, t // tq, VT_ROWS, tq), BF16),
        pltpu.VMEM((g, tq, tq), F32),
        pltpu.VMEM((g, tq, tq), F32),
        pltpu.VMEM((g, 1, tq), F32),
        pltpu.VMEM((g, 1, tq), F32),
        pltpu.VMEM((g, 1, tq), F32),
        pltpu.VMEM((g, VT_ROWS, tq), F32),
    ]
    if moba:
        scratch.append(pltpu.VMEM((g, LANES, HEAD_DIM), F32))
    else:
        in_specs += [
            pl.BlockSpec((None, g, tq, LANES), lambda bi, h, i: (bi, h, i, 0)),
            pl.BlockSpec((None, g, t, LANES), lambda bi, h, i: (bi, h, 0, 0)),
        ]
        args += list(fox_extras)
    return pl.pallas_call(
        functools.partial(_attn_kernel, moba=moba),
        grid=(b, n_heads // g, t // tq),
        in_specs=in_specs,
        out_specs=pl.BlockSpec((None, tq, w), lambda bi, h, i: (bi, i, h)),
        out_shape=jax.ShapeDtypeStruct((b, t, n_heads * HEAD_DIM), BF16),
        scratch_shapes=scratch,
        compiler_params=_cparams("arbitrary", "arbitrary", "arbitrary"),
        name="moba_attn" if moba else "fox_attn",
    )(*args)


def _mem_kv_kernel(mem_ref, g_ref, w_ref, k_ref, v_ref):
    xn = _rmsnorm(mem_ref[...], g_ref[...]).astype(BF16)
    kv = jnp.dot(xn, w_ref[...], preferred_element_type=F32)
    half = kv.shape[1] // 2
    k_ref[...] = kv[:, :half].astype(k_ref.dtype)
    v_ref[...] = kv[:, half:].astype(v_ref.dtype)


def _mem_kv(mem, g, w_ckv):
    b, nm, d = mem.shape
    xd = w_ckv.shape[1] // 2
    out = jax.ShapeDtypeStruct((b, nm, xd), BF16)
    o_spec = pl.BlockSpec((None, nm, xd), lambda bi: (bi, 0, 0))
    return pl.pallas_call(
        _mem_kv_kernel,
        grid=(b,),
        in_specs=[
            pl.BlockSpec((None, nm, d), lambda bi: (bi, 0, 0)),
            pl.BlockSpec((1, d), lambda bi: (0, 0)),
            pl.BlockSpec((d, 2 * xd), lambda bi: (0, 0)),
        ],
        out_specs=[o_spec, o_spec],
        out_shape=[out, out],
        compiler_params=_cparams("arbitrary"),
        name="mem_kv",
    )(mem, g, w_ckv)


def _mid_kernel(x_ref, om_ref, of_ref, wom_ref, wof_ref, gx_ref, wcq_ref, kc_ref, vc_ref,
                wco_ref, gf_ref, h_ref, xn_ref):
    h1 = (x_ref[...]
          + jnp.dot(om_ref[...], wom_ref[...], preferred_element_type=F32)
          + jnp.dot(of_ref[...], wof_ref[...], preferred_element_type=F32))
    xn2 = _rmsnorm(h1, gx_ref[...]).astype(BF16)
    q = (jnp.dot(xn2, wcq_ref[...], preferred_element_type=F32) * SCALE).astype(BF16)
    outs = []
    for h in range(N_XATTN_HEADS):
        sl = slice(h * HEAD_DIM, (h + 1) * HEAD_DIM)
        s = lax.dot_general(q[:, sl], kc_ref[:, sl], _NT, preferred_element_type=F32)
        m, l, acc = _flash_init(s, vc_ref[:, sl])
        outs.append((acc / l).astype(BF16))
    oc = jnp.concatenate(outs, axis=1)
    h2 = h1 + jnp.dot(oc, wco_ref[...], preferred_element_type=F32)
    h_ref[...] = h2
    xn_ref[...] = _rmsnorm(h2, gf_ref[...]).astype(xn_ref.dtype)


def _mid(x2, o_moba, o_fox, w_om, w_of, gx, w_cq, kc, vc, w_co, gf, *, seq):
    n, d = x2.shape
    wm, wf = o_moba.shape[1], o_fox.shape[1]
    nm, xd = kc.shape[1], kc.shape[2]
    tm = min(512, seq)
    t_tiles = seq // tm
    const = lambda i: (0, 0)
    return pl.pallas_call(
        _mid_kernel,
        grid=(n // tm,),
        in_specs=[
            pl.BlockSpec((tm, d), lambda i: (i, 0)),
            pl.BlockSpec((tm, wm), lambda i: (i, 0)),
            pl.BlockSpec((tm, wf), lambda i: (i, 0)),
            pl.BlockSpec((wm, d), const),
            pl.BlockSpec((wf, d), const),
            pl.BlockSpec((1, d), const),
            pl.BlockSpec((d, xd), const),
            pl.BlockSpec((None, nm, xd), lambda i: (i // t_tiles, 0, 0)),
            pl.BlockSpec((None, nm, xd), lambda i: (i // t_tiles, 0, 0)),
            pl.BlockSpec((xd, d), const),
            pl.BlockSpec((1, d), const),
        ],
        out_specs=[
            pl.BlockSpec((tm, d), lambda i: (i, 0)),
            pl.BlockSpec((tm, d), lambda i: (i, 0)),
        ],
        out_shape=[
            jax.ShapeDtypeStruct((n, d), F32),
            jax.ShapeDtypeStruct((n, d), BF16),
        ],
        compiler_params=_cparams("arbitrary"),
        name="mid",
    )(x2, o_moba, o_fox, w_om, w_of, gx, w_cq, kc, vc, w_co, gf)


def _conv_ffn_kernel(xn_ref, h_ref, wg_ref, wu_ref, cwg_ref, cwu_ref, cbg_ref, cbu_ref,
                     wd_ref, fg_ref, o_ref, hs_sc, tail_sc, *, tiles_per_seq):
    i = pl.program_id(0)
    f = pl.program_id(1)
    nf = pl.num_programs(1)
    tm = xn_ref.shape[0]
    tf = wg_ref.shape[1]
    halo = SUBLANES
    first = (i % tiles_per_seq) == 0

    @pl.when((i == 0) & (f == 0))
    def _():
        tail_sc[...] = jnp.zeros_like(tail_sc)

    def up():
        xn = xn_ref[...]
        hs_sc[0:halo, :] = jnp.where(first, 0.0, tail_sc[f])
        hs_sc[halo:, :tf] = jnp.dot(xn, wg_ref[...], preferred_element_type=F32)
        hs_sc[halo:, tf:] = jnp.dot(xn, wu_ref[...], preferred_element_type=F32)
        tail_sc[f] = hs_sc[tm:tm + halo, :]

    def gate(cols):
        def conv(base, cw_ref, cb_ref):
            y = cb_ref[:, cols]
            for k in range(CONV_WIDTH):
                lo = halo - (CONV_WIDTH - 1) + k
                y = y + hs_sc[lo:lo + tm, base + cols.start:base + cols.stop] * cw_ref[k:k + 1, cols]
            return y
        gte = conv(0, cwg_ref, cbg_ref)
        return (gte * (1.0 / (1.0 + jnp.exp2(gte * (-LOG2E)))) * conv(tf, cwu_ref, cbu_ref)
                ).astype(BF16)

    @pl.when(f == 0)
    def _():
        o_ref[...] = h_ref[...]

    up()
    hk = tf // 2
    o_ref[...] += (
        jnp.dot(gate(slice(0, hk)), wd_ref[:hk, :], preferred_element_type=F32)
        + jnp.dot(gate(slice(hk, tf)), wd_ref[hk:, :], preferred_element_type=F32))

    @pl.when(f == nf - 1)
    def _():
        o_ref[...] = _rmsnorm(o_ref[...], fg_ref[...])


def _conv_ffn(xn3, h2, w_up, conv_w, conv_b, w_down, fg, *, seq):
    n, d = h2.shape
    ff = w_down.shape[0]
    tm = min(512, seq)
    tf = 512
    assert ff % tf == 0
    nf = ff // tf
    return pl.pallas_call(
        functools.partial(_conv_ffn_kernel, tiles_per_seq=seq // tm),
        grid=(n // tm, nf),
        in_specs=[
            pl.BlockSpec((tm, d), lambda i, f: (i, 0)),
            pl.BlockSpec((tm, d), lambda i, f: (i, 0)),
            pl.BlockSpec((d, tf), lambda i, f: (0, f)),
            pl.BlockSpec((d, tf), lambda i, f: (0, f + nf)),
            pl.BlockSpec((CONV_WIDTH, tf), lambda i, f: (0, f)),
            pl.BlockSpec((CONV_WIDTH, tf), lambda i, f: (0, f + nf)),
            pl.BlockSpec((1, tf), lambda i, f: (0, f)),
            pl.BlockSpec((1, tf), lambda i, f: (0, f + nf)),
            pl.BlockSpec((tf, d), lambda i, f: (f, 0)),
            pl.BlockSpec((1, d), lambda i, f: (0, 0)),
        ],
        out_specs=pl.BlockSpec((tm, d), lambda i, f: (i, 0)),
        out_shape=jax.ShapeDtypeStruct((n, d), F32),
        scratch_shapes=[
            pltpu.VMEM((tm + SUBLANES, 2 * tf), F32),
            pltpu.VMEM((nf, SUBLANES, 2 * tf), F32),
        ],
        compiler_params=_cparams("arbitrary", "arbitrary"),
        name="conv_ffn",
    )(xn3, h2, w_up, w_up, conv_w, conv_w, conv_b, conv_b, w_down, fg)


def _rope_tables(t):
    half = HEAD_DIM // 2
    inv = ROPE_THETA ** (-jnp.arange(half, dtype=F32) / half)
    ang = jnp.arange(t, dtype=F32)[:, None] * inv[None, :]
    cos, sin = jnp.cos(ang), jnp.sin(ang)
    return jnp.concatenate([cos, cos], axis=1), jnp.concatenate([-sin, sin], axis=1)


def kernel(x, mem, attn_norm_g, w_in, b_f, w_o, xattn_norm_g, mem_norm_g, w_cq, w_ckv, w_co,
           ffn_norm_g, w_up, conv_w, conv_b, w_down, final_norm_g):
    b, t, d = x.shape
    assert w_in.shape[0] == 1, "one layer: the final rmsnorm is fused into the ffn kernel"
    n_heads = d // HEAD_DIM
    n_moba = n_heads // 2
    n_fox = n_heads - n_moba
    assert n_moba == n_fox and n_fox <= LANES
    wgrp = n_moba * HEAD_DIM
    main_cols = 6 * wgrp
    cos, sin = _rope_tables(t)
    row = lambda v: v.reshape(1, -1)

    w_in_t = w_in[0].T
    w_z_t = jnp.pad(w_in_t[main_cols:], ((0, LANES - n_fox), (0, 0))).astype(BF16)
    bf = row(jnp.pad(b_f[0].astype(F32), (0, LANES - n_fox)))
    xn, qx, kx = _norm_gate(x, row(attn_norm_g[0]), w_z_t, bf, n_fox)
    proj = _in_proj(xn.reshape(b * t, d), w_in_t, cos, sin, seq=t, group_width=wgrp)
    proj3 = proj.reshape(b, t, main_cols)
    o_moba = _attn(proj3, n_moba, 0)
    o_fox = _attn(proj3, n_fox, 3 * n_moba, (qx, kx))
    kc, vc = _mem_kv(mem, row(mem_norm_g[0]), w_ckv[0].astype(BF16))
    wo = w_o[0].astype(BF16)
    h2, xn3 = _mid(x.reshape(b * t, d), o_moba.reshape(b * t, wgrp), o_fox.reshape(b * t, wgrp),
                   wo[:wgrp], wo[wgrp:], row(xattn_norm_g[0]), w_cq[0].astype(BF16), kc, vc,
                   w_co[0].astype(BF16), row(ffn_norm_g[0]), seq=t)
    out = _conv_ffn(xn3, h2, w_up[0].astype(BF16), conv_w[0], row(conv_b[0]),
                    w_down[0].astype(BF16), row(final_norm_g), seq=t)
    return out.reshape(b, t, d)
```

```python
import functools
import math

import jax
import jax.numpy as jnp
from jax import lax
from jax.experimental import pallas as pl
from jax.experimental.pallas import tpu as pltpu

HEAD_DIM = 128
MOBA_BLOCK = 256
MOBA_TOPK = 3
ROPE_THETA = 10000.0
N_XATTN_HEADS = 4
CONV_WIDTH = 3
RMS_EPS = 1e-6
NEG = -1e30
SCALE = 1.0 / math.sqrt(HEAD_DIM)
LOG2E = math.log2(math.e)
QSCALE = SCALE * LOG2E

LANES = 128
SUBLANES = 8
VMEM_LIMIT_BYTES = 56 * 1024 * 1024

F32 = jnp.float32
BF16 = jnp.bfloat16

_NT = (((1,), (1,)), ((), ()))


def _cparams(*sem):
    return pltpu.CompilerParams(dimension_semantics=sem, vmem_limit_bytes=VMEM_LIMIT_BYTES)


def _rmsnorm(x, g):
    ms = jnp.mean(x * x, axis=-1, keepdims=True)
    return x * lax.rsqrt(ms + RMS_EPS) * g


def _split3(x):
    hi = x.astype(BF16)
    r = x - hi.astype(F32)
    mid = r.astype(BF16)
    lo = (r - mid.astype(F32)).astype(BF16)
    return hi, mid, lo


def _in_proj_kernel(xn_ref, w_ref, cos_ref, sin_ref, o_ref, w_sc, *, group_of_tile):
    j = pl.program_id(0)

    @pl.when(pl.program_id(1) == 0)
    def _():
        w_sc[...] = w_ref[...].astype(BF16)

    grp = group_of_tile(j)
    rot = (grp <= 1).astype(F32)
    scale = jnp.where((grp == 0) | (grp == 3), QSCALE, 1.0).astype(F32)
    a = (cos_ref[...] * rot + (1.0 - rot)) * scale
    b = sin_ref[...] * (rot * scale)
    tm, tn = o_ref.shape
    chunk = min(256, tm)
    for r0 in range(0, tm, chunk):
        rows = slice(r0, r0 + chunk)
        y = lax.dot_general(xn_ref[rows, :], w_sc[...], _NT, preferred_element_type=F32)
        for c0 in range(0, tn, HEAD_DIM):
            yh = y[:, c0:c0 + HEAD_DIM]
            out = yh * a[rows] + pltpu.roll(yh, HEAD_DIM // 2, 1) * b[rows]
            o_ref[rows, c0:c0 + HEAD_DIM] = out.astype(o_ref.dtype)


def _in_proj(xn, w_in_t, cos, sin, *, seq, group_width):
    n, d = xn.shape
    cols = 6 * group_width
    tm = min(1024, seq)
    tn = min(1024, group_width)
    tiles_per_group = group_width // tn
    t_tiles = seq // tm
    kern = functools.partial(_in_proj_kernel, group_of_tile=lambda j: j // tiles_per_group)
    return pl.pallas_call(
        kern,
        grid=(cols // tn, n // tm),
        in_specs=[
            pl.BlockSpec((tm, d), lambda j, i: (i, 0)),
            pl.BlockSpec((tn, d), lambda j, i: (j, 0)),
            pl.BlockSpec((tm, HEAD_DIM), lambda j, i: (i % t_tiles, 0)),
            pl.BlockSpec((tm, HEAD_DIM), lambda j, i: (i % t_tiles, 0)),
        ],
        out_specs=pl.BlockSpec((tm, tn), lambda j, i: (i, j)),
        out_shape=jax.ShapeDtypeStruct((n, cols), BF16),
        scratch_shapes=[pltpu.VMEM((tn, d), BF16)],
        compiler_params=_cparams("arbitrary", "arbitrary"),
        name="in_proj",
    )(xn, w_in_t, cos, sin)


def _norm_gate_kernel(x_ref, g_ref, wz_ref, bf_ref, route_ref, xn_ref, qx_ref, kx_ref, carry_sc,
                      *, n_heads):
    t = pl.program_id(1)

    @pl.when(t == 0)
    def _():
        carry_sc[...] = jnp.zeros_like(carry_sc)

    xn = _rmsnorm(x_ref[...], g_ref[...]).astype(BF16)
    xn_ref[...] = xn
    z = lax.dot_general(xn, wz_ref[...], _NT, preferred_element_type=F32) + bf_ref[...]
    lf = jnp.minimum(z, 0.0) - jnp.log(1.0 + jnp.exp(-jnp.abs(z)))

    tc = lf.shape[0]
    row = lax.broadcasted_iota(jnp.int32, (tc, tc), 0)
    col = lax.broadcasted_iota(jnp.int32, (tc, tc), 1)
    tri = (col <= row).astype(BF16)
    c = carry_sc[0:1, :]
    for part in _split3(lf):
        c = c + jnp.dot(tri, part, preferred_element_type=F32)
    carry_sc[0:1, :] = c[tc - 1:tc, :]

    parts = jnp.concatenate(_split3(c * LOG2E), axis=1)
    routed = jnp.dot(parts, route_ref[...], preferred_element_type=F32)
    lane = lax.broadcasted_iota(jnp.int32, (tc, LANES), 1)
    q_ones = ((lane >= 3) & (lane < 6)).astype(F32)
    k_ones = (lane < 3).astype(F32)
    for h in range(n_heads):
        qx_ref[h] = (routed[:, (2 * h) * LANES:(2 * h + 1) * LANES] + q_ones).astype(BF16)
        kx_ref[h] = (routed[:, (2 * h + 1) * LANES:(2 * h + 2) * LANES] + k_ones).astype(BF16)


def _gate_routing(n_heads):
    import numpy as np
    r = np.zeros((3 * LANES, n_heads * 2 * LANES), np.float32)
    for h in range(n_heads):
        for p in range(3):
            r[p * LANES + h, (2 * h) * LANES + p] = 1.0
            r[p * LANES + h, (2 * h + 1) * LANES + 3 + p] = -1.0
    return jnp.asarray(r, BF16)


def _norm_gate(x3, g, w_z, b_f, n_heads):
    b, t, d = x3.shape
    tc = min(512, t)
    ext = jax.ShapeDtypeStruct((b, n_heads, t, LANES), BF16)
    ext_spec = pl.BlockSpec((None, n_heads, tc, LANES), lambda bi, ti: (bi, 0, ti, 0))
    const = lambda bi, ti: (0, 0)
    return pl.pallas_call(
        functools.partial(_norm_gate_kernel, n_heads=n_heads),
        grid=(b, t // tc),
        in_specs=[
            pl.BlockSpec((None, tc, d), lambda bi, ti: (bi, ti, 0)),
            pl.BlockSpec((1, d), const),
            pl.BlockSpec((LANES, d), const),
            pl.BlockSpec((1, LANES), const),
            pl.BlockSpec((3 * LANES, n_heads * 2 * LANES), const),
        ],
        out_specs=[pl.BlockSpec((None, tc, d), lambda bi, ti: (bi, ti, 0)), ext_spec, ext_spec],
        out_shape=[jax.ShapeDtypeStruct((b, t, d), BF16), ext, ext],
        scratch_shapes=[pltpu.VMEM((SUBLANES, LANES), F32)],
        compiler_params=_cparams("arbitrary", "arbitrary"),
        name="norm_gate",
    )(x3, g, w_z, b_f, _gate_routing(n_heads))


ATT_TILE = 512
ATT_HEADS = 4
VT_ROWS = HEAD_DIM + 16


def _flash_init(s, v):
    m = jnp.max(s, axis=1, keepdims=True)
    p = jnp.exp(s - m)
    l = jnp.sum(p, axis=1, keepdims=True)
    acc = jnp.dot(p.astype(BF16), v, preferred_element_type=F32)
    return m, l, acc


def _moba_bias(q, kmean, iq):
    tq = q.shape[0]
    nb = kmean.shape[0]
    km_hi = kmean.astype(BF16)
    km_lo = (kmean - km_hi.astype(F32)).astype(BF16)
    gate = (lax.dot_general(km_hi, q, _NT, preferred_element_type=F32)
            + lax.dot_general(km_lo, q, _NT, preferred_element_type=F32))
    blk_i = lax.broadcasted_iota(jnp.int32, gate.shape, 0)
    qry_i = lax.broadcasted_iota(jnp.int32, gate.shape, 1)
    own = iq * (tq // MOBA_BLOCK) + qry_i // MOBA_BLOCK
    blk_f = blk_i.astype(F32)
    past = blk_i < own
    g = jnp.where(past, gate, -jnp.inf)
    sel = jnp.zeros(gate.shape, jnp.bool_)
    for _ in range(MOBA_TOPK):
        mx = jnp.max(g, axis=0, keepdims=True)
        idx = jnp.min(jnp.where(g == mx, blk_f, float(LANES)), axis=0, keepdims=True)
        pick = blk_f == idx
        sel = sel | pick
        g = jnp.where(pick, -jnp.inf, g)
    bias_t = jnp.where((sel & past) | (blk_i == own), 0.0, NEG)
    bias_t = jnp.concatenate([bias_t, jnp.zeros((LANES - nb, tq), F32)], axis=0)
    return bias_t.T.astype(BF16)


def _attn_kernel(*refs, moba):
    if moba:
        q_ref, k_ref, v_ref, o_ref = refs[:4]
        kmean_sc = refs[-1]
        refs = refs[:-1]
    else:
        q_ref, k_ref, v_ref, qx_ref, kx_ref, o_ref = refs[:6]
    kaug_sc, vt_sc, sa_sc, sb_sc, mxa_sc, mxb_sc, m_sc, acc_sc = refs[-8:]
    iq = pl.program_id(2)
    tq = q_ref.shape[0]
    t = k_ref.shape[0]
    n_heads = q_ref.shape[1] // HEAD_DIM
    heads = [slice(g * HEAD_DIM, (g + 1) * HEAD_DIM) for g in range(n_heads)]

    @pl.when(iq == 0)
    def _():
        for g, hs in enumerate(heads):
            kaug_sc[g, :, :HEAD_DIM] = k_ref[:, hs]
            if moba:
                row = lax.broadcasted_iota(jnp.int32, (t, LANES), 0)
                lane = lax.broadcasted_iota(jnp.int32, (t, LANES), 1)
                kaug_sc[g, :, HEAD_DIM:] = ((row // MOBA_BLOCK) == lane).astype(BF16)
                kmean_sc[g] = jnp.zeros((LANES, HEAD_DIM), F32)
                for n in range(t // MOBA_BLOCK):
                    kb = k_ref[n * MOBA_BLOCK:(n + 1) * MOBA_BLOCK, hs].astype(F32)
                    kmean_sc[g, n:n + 1, :] = jnp.mean(kb, axis=0, keepdims=True)
            else:
                kaug_sc[g, :, HEAD_DIM:] = kx_ref[g]
            pad_row = lax.broadcasted_iota(jnp.int32, (VT_ROWS - HEAD_DIM, tq), 0)
            for j in range(t // tq):
                vt_sc[g, j, :HEAD_DIM, :] = (
                    v_ref[j * tq:(j + 1) * tq, hs].astype(F32).T.astype(BF16))
                vt_sc[g, j, HEAD_DIM:, :] = (pad_row == 0).astype(BF16)

    qa = []
    for g, hs in enumerate(heads):
        q = q_ref[:, hs]
        if moba:
            nb = -(-(t // MOBA_BLOCK) // 16) * 16
            extra = _moba_bias(q, kmean_sc[g, :nb, :], iq)
        else:
            extra = qx_ref[g]
        qa.append(jnp.concatenate([q, extra], axis=1))

    def scores(g, j):
        off = pl.multiple_of(j * tq, tq)
        return lax.dot_general(kaug_sc[g, pl.ds(off, tq), :], qa[g], _NT,
                               preferred_element_type=F32)

    half = tq // 2
    halves = (slice(0, half), slice(half, tq))

    def produce(j, s_ref, mx_ref):
        for g in range(n_heads):
            s = scores(g, j)
            s_ref[g] = s
            mx_ref[g] = jnp.max(s, axis=0, keepdims=True)

    def consume(j, s_ref, mx_ref, diagonal=False):
        for g in range(n_heads):
            m = m_sc[g]
            m_new = jnp.maximum(m, mx_ref[g])
            alpha = jnp.exp2(m - m_new)
            m_sc[g] = m_new
            for qi, qs in enumerate(halves):
                pv = alpha[:, qs] * acc_sc[g, :, qs]
                for ki, ks in enumerate(halves):
                    if diagonal and ki > qi:
                        continue
                    p = jnp.exp2(s_ref[g, ks, qs] - m_new[:, qs])
                    pv = pv + jnp.dot(vt_sc[g, j, :, ks], p.astype(BF16),
                                      preferred_element_type=F32)
                acc_sc[g, :, qs] = pv

    for g in range(n_heads):
        m_sc[g] = jnp.full((1, tq), NEG, F32)
        acc_sc[g] = jnp.zeros((VT_ROWS, tq), F32)
    produce(0, sa_sc, mxa_sc)

    def body(jj, carry):
        j = 2 * jj
        produce(j + 1, sb_sc, mxb_sc)
        consume(j, sa_sc, mxa_sc)
        produce(j + 2, sa_sc, mxa_sc)
        consume(j + 1, sb_sc, mxb_sc)
        return carry

    lax.fori_loop(0, iq // 2, body, 0)

    def finish(s_ref, mx_ref):
        key = lax.broadcasted_iota(jnp.int32, (tq, tq), 0)
        qry = lax.broadcasted_iota(jnp.int32, (tq, tq), 1)
        for g in range(n_heads):
            s = jnp.where(key <= qry, s_ref[g], NEG)
            s_ref[g] = s
            mx_ref[g] = jnp.max(s, axis=0, keepdims=True)
        consume(iq, s_ref, mx_ref, diagonal=True)
        for g, hs in enumerate(heads):
            out = acc_sc[g, :HEAD_DIM, :] / acc_sc[g, HEAD_DIM:HEAD_DIM + 1, :]
            o_ref[:, hs] = out.T.astype(o_ref.dtype)

    @pl.when(iq % 2 == 0)
    def _():
        finish(sa_sc, mxa_sc)

    @pl.when(iq % 2 == 1)
    def _():
        produce(iq, sb_sc, mxb_sc)
        consume(iq - 1, sa_sc, mxa_sc)
        finish(sb_sc, mxb_sc)


def _attn(proj3, n_heads, head0, fox_extras=None):
    b, t, _ = proj3.shape
    moba = fox_extras is None
    tq = min(ATT_TILE, t)
    g = min(ATT_HEADS, n_heads)
    assert t % tq == 0 and tq % MOBA_BLOCK == 0 and t // MOBA_BLOCK <= LANES and n_heads % g == 0
    w = g * HEAD_DIM
    c0 = head0 // g
    in_specs = [
        pl.BlockSpec((None, tq, w), lambda bi, h, i: (bi, i, c0 + h)),
        pl.BlockSpec((None, t, w), lambda bi, h, i: (bi, 0, c0 + n_heads // g + h)),
        pl.BlockSpec((None, t, w), lambda bi, h, i: (bi, 0, c0 + 2 * (n_heads // g) + h)),
    ]
    args = [proj3, proj3, proj3]
    scratch = [
        pltpu.VMEM((g, t, 2 * HEAD_DIM), BF16),
        pltpu.VMEM((g, t // tq, VT_ROWS, tq), BF16),
        pltpu.VMEM((g, tq, tq), F32),
        pltpu.VMEM((g, tq, tq), F32),
        pltpu.VMEM((g, 1, tq), F32),
        pltpu.VMEM((g, 1, tq), F32),
        pltpu.VMEM((g, 1, tq), F32),
        pltpu.VMEM((g, VT_ROWS, tq), F32),
    ]
    if moba:
        scratch.append(pltpu.VMEM((g, LANES, HEAD_DIM), F32))
    else:
        in_specs += [
            pl.BlockSpec((None, g, tq, LANES), lambda bi, h, i: (bi, h, i, 0)),
            pl.BlockSpec((None, g, t, LANES), lambda bi, h, i: (bi, h, 0, 0)),
        ]
        args += list(fox_extras)
    return pl.pallas_call(
        functools.partial(_attn_kernel, moba=moba),
        grid=(b, n_heads // g, t // tq),
        in_specs=in_specs,
        out_specs=pl.BlockSpec((None, tq, w), lambda bi, h, i: (bi, i, h)),
        out_shape=jax.ShapeDtypeStruct((b, t, n_heads * HEAD_DIM), BF16),
        scratch_shapes=scratch,
        compiler_params=_cparams("arbitrary", "arbitrary", "arbitrary"),
        name="moba_attn" if moba else "fox_attn",
    )(*args)


def _mem_kv_kernel(mem_ref, g_ref, w_ref, k_ref, v_ref):
    xn = _rmsnorm(mem_ref[...], g_ref[...]).astype(BF16)
    kv = jnp.dot(xn, w_ref[...], preferred_element_type=F32)
    half = kv.shape[1] // 2
    k_ref[...] = kv[:, :half].astype(k_ref.dtype)
    v_ref[...] = kv[:, half:].astype(v_ref.dtype)


def _mem_kv(mem, g, w_ckv):
    b, nm, d = mem.shape
    xd = w_ckv.shape[1] // 2
    out = jax.ShapeDtypeStruct((b, nm, xd), BF16)
    o_spec = pl.BlockSpec((None, nm, xd), lambda bi: (bi, 0, 0))
    return pl.pallas_call(
        _mem_kv_kernel,
        grid=(b,),
        in_specs=[
            pl.BlockSpec((None, nm, d), lambda bi: (bi, 0, 0)),
            pl.BlockSpec((1, d), lambda bi: (0, 0)),
            pl.BlockSpec((d, 2 * xd), lambda bi: (0, 0)),
        ],
        out_specs=[o_spec, o_spec],
        out_shape=[out, out],
        compiler_params=_cparams("arbitrary"),
        name="mem_kv",
    )(mem, g, w_ckv)


def _mid_kernel(x_ref, om_ref, of_ref, wom_ref, wof_ref, gx_ref, wcq_ref, kc_ref, vc_ref,
                wco_ref, gf_ref, h_ref, xn_ref):
    h1 = (x_ref[...]
          + jnp.dot(om_ref[...], wom_ref[...], preferred_element_type=F32)
          + jnp.dot(of_ref[...], wof_ref[...], preferred_element_type=F32))
    xn2 = _rmsnorm(h1, gx_ref[...]).astype(BF16)
    q = (jnp.dot(xn2, wcq_ref[...], preferred_element_type=F32) * SCALE).astype(BF16)
    outs = []
    for h in range(N_XATTN_HEADS):
        sl = slice(h * HEAD_DIM, (h + 1) * HEAD_DIM)
        s = lax.dot_general(q[:, sl], kc_ref[:, sl], _NT, preferred_element_type=F32)
        m, l, acc = _flash_init(s, vc_ref[:, sl])
        outs.append((acc / l).astype(BF16))
    oc = jnp.concatenate(outs, axis=1)
    h2 = h1 + jnp.dot(oc, wco_ref[...], preferred_element_type=F32)
    h_ref[...] = h2
    xn_ref[...] = _rmsnorm(h2, gf_ref[...]).astype(xn_ref.dtype)


def _mid(x2, o_moba, o_fox, w_om, w_of, gx, w_cq, kc, vc, w_co, gf, *, seq):
    n, d = x2.shape
    wm, wf = o_moba.shape[1], o_fox.shape[1]
    nm, xd = kc.shape[1], kc.shape[2]
    tm = min(512, seq)
    t_tiles = seq // tm
    const = lambda i: (0, 0)
    return pl.pallas_call(
        _mid_kernel,
        grid=(n // tm,),
        in_specs=[
            pl.BlockSpec((tm, d), lambda i: (i, 0)),
            pl.BlockSpec((tm, wm), lambda i: (i, 0)),
            pl.BlockSpec((tm, wf), lambda i: (i, 0)),
            pl.BlockSpec((wm, d), const),
            pl.BlockSpec((wf, d), const),
            pl.BlockSpec((1, d), const),
            pl.BlockSpec((d, xd), const),
            pl.BlockSpec((None, nm, xd), lambda i: (i // t_tiles, 0, 0)),
            pl.BlockSpec((None, nm, xd), lambda i: (i // t_tiles, 0, 0)),
            pl.BlockSpec((xd, d), const),
            pl.BlockSpec((1, d), const),
        ],
        out_specs=[
            pl.BlockSpec((tm, d), lambda i: (i, 0)),
            pl.BlockSpec((tm, d), lambda i: (i, 0)),
        ],
        out_shape=[
            jax.ShapeDtypeStruct((n, d), F32),
            jax.ShapeDtypeStruct((n, d), BF16),
        ],
        compiler_params=_cparams("arbitrary"),
        name="mid",
    )(x2, o_moba, o_fox, w_om, w_of, gx, w_cq, kc, vc, w_co, gf)


def _conv_ffn_kernel(xn_ref, h_ref, wg_ref, wu_ref, cwg_ref, cwu_ref, cbg_ref, cbu_ref,
                     wd_ref, fg_ref, o_ref, hs_sc, tail_sc, *, tiles_per_seq, n_res):
    i = pl.program_id(0)
    f = pl.program_id(1)
    nf = pl.num_programs(1)
    tm = xn_ref.shape[0]
    tf = wg_ref.shape[1]
    halo = SUBLANES
    first = (i % tiles_per_seq) == 0

    @pl.when((i == 0) & (f == 0))
    def _():
        tail_sc[...] = jnp.zeros_like(tail_sc)

    def up():
        xn = xn_ref[...]
        hs_sc[0:halo, :] = jnp.where(first, 0.0, tail_sc[f])
        hs_sc[halo:, :tf] = jnp.dot(xn, wg_ref[...], preferred_element_type=F32)
        hs_sc[halo:, tf:] = jnp.dot(xn, wu_ref[...], preferred_element_type=F32)
        tail_sc[f] = hs_sc[tm:tm + halo, :]

    def gate(cols):
        def conv(base, cw_ref, cb_ref):
            y = cb_ref[:, cols]
            for k in range(CONV_WIDTH):
                lo = halo - (CONV_WIDTH - 1) + k
                y = y + hs_sc[lo:lo + tm, base + cols.start:base + cols.stop] * cw_ref[k:k + 1, cols]
            return y
        gte = conv(0, cwg_ref, cbg_ref)
        return (gte * (1.0 / (1.0 + jnp.exp2(gte * (-LOG2E)))) * conv(tf, cwu_ref, cbu_ref)
                ).astype(BF16)

    res_rows = h_ref.shape[0]

    @pl.when(f == 0)
    def _():
        o_ref[...] = jnp.zeros_like(o_ref)

    @pl.when(f < n_res)
    def _():
        r0 = pl.multiple_of(f * res_rows, res_rows)
        o_ref[pl.ds(r0, res_rows), :] += h_ref[...]

    up()
    hk = tf // 2
    o_ref[...] += (
        jnp.dot(gate(slice(0, hk)), wd_ref[:hk, :], preferred_element_type=F32)
        + jnp.dot(gate(slice(hk, tf)), wd_ref[hk:, :], preferred_element_type=F32))

    @pl.when(f == nf - 1)
    def _():
        o_ref[...] = _rmsnorm(o_ref[...], fg_ref[...])


def _conv_ffn(xn3, h2, w_up, conv_w, conv_b, w_down, fg, *, seq):
    n, d = h2.shape
    ff = w_down.shape[0]
    tm = min(1024, seq)
    tf = 512
    assert ff % tf == 0
    nf = ff // tf
    n_res = 8 if nf >= 8 else 4
    assert nf >= n_res and tm % (n_res * SUBLANES) == 0
    return pl.pallas_call(
        functools.partial(_conv_ffn_kernel, tiles_per_seq=seq // tm, n_res=n_res),
        grid=(n // tm, nf),
        in_specs=[
            pl.BlockSpec((tm, d), lambda i, f: (i, 0)),
            pl.BlockSpec((tm // n_res, d), lambda i, f: (i * n_res + jnp.minimum(f, n_res - 1), 0)),
            pl.BlockSpec((d, tf), lambda i, f: (0, f)),
            pl.BlockSpec((d, tf), lambda i, f: (0, f + nf)),
            pl.BlockSpec((CONV_WIDTH, tf), lambda i, f: (0, f)),
            pl.BlockSpec((CONV_WIDTH, tf), lambda i, f: (0, f + nf)),
            pl.BlockSpec((1, tf), lambda i, f: (0, f)),
            pl.BlockSpec((1, tf), lambda i, f: (0, f + nf)),
            pl.BlockSpec((tf, d), lambda i, f: (f, 0)),
            pl.BlockSpec((1, d), lambda i, f: (0, 0)),
        ],
        out_specs=pl.BlockSpec((tm, d), lambda i, f: (i, 0)),
        out_shape=jax.ShapeDtypeStruct((n, d), F32),
        scratch_shapes=[
            pltpu.VMEM((tm + SUBLANES, 2 * tf), F32),
            pltpu.VMEM((nf, SUBLANES, 2 * tf), F32),
        ],
        compiler_params=_cparams("arbitrary", "arbitrary"),
        name="conv_ffn",
    )(xn3, h2, w_up, w_up, conv_w, conv_w, conv_b, conv_b, w_down, fg)


def _rope_tables(t):
    half = HEAD_DIM // 2
    inv = ROPE_THETA ** (-jnp.arange(half, dtype=F32) / half)
    ang = jnp.arange(t, dtype=F32)[:, None] * inv[None, :]
    cos, sin = jnp.cos(ang), jnp.sin(ang)
    return jnp.concatenate([cos, cos], axis=1), jnp.concatenate([-sin, sin], axis=1)


def kernel(x, mem, attn_norm_g, w_in, b_f, w_o, xattn_norm_g, mem_norm_g, w_cq, w_ckv, w_co,
           ffn_norm_g, w_up, conv_w, conv_b, w_down, final_norm_g):
    b, t, d = x.shape
    assert w_in.shape[0] == 1, "one layer: the final rmsnorm is fused into the ffn kernel"
    n_heads = d // HEAD_DIM
    n_moba = n_heads // 2
    n_fox = n_heads - n_moba
    assert n_moba == n_fox and n_fox <= LANES
    wgrp = n_moba * HEAD_DIM
    main_cols = 6 * wgrp
    cos, sin = _rope_tables(t)
    row = lambda v: v.reshape(1, -1)

    w_in_t = w_in[0].T
    w_z_t = jnp.pad(w_in_t[main_cols:], ((0, LANES - n_fox), (0, 0))).astype(BF16)
    bf = row(jnp.pad(b_f[0].astype(F32), (0, LANES - n_fox)))
    xn, qx, kx = _norm_gate(x, row(attn_norm_g[0]), w_z_t, bf, n_fox)
    proj = _in_proj(xn.reshape(b * t, d), w_in_t, cos, sin, seq=t, group_width=wgrp)
    proj3 = proj.reshape(b, t, main_cols)
    o_moba = _attn(proj3, n_moba, 0)
    o_fox = _attn(proj3, n_fox, 3 * n_moba, (qx, kx))
    kc, vc = _mem_kv(mem, row(mem_norm_g[0]), w_ckv[0].astype(BF16))
    wo = w_o[0].astype(BF16)
    h2, xn3 = _mid(x.reshape(b * t, d), o_moba.reshape(b * t, wgrp), o_fox.reshape(b * t, wgrp),
                   wo[:wgrp], wo[wgrp:], row(xattn_norm_g[0]), w_cq[0].astype(BF16), kc, vc,
                   w_co[0].astype(BF16), row(ffn_norm_g[0]), seq=t)
    out = _conv_ffn(xn3, h2, w_up[0].astype(BF16), conv_w[0], row(conv_b[0]),
                    w_down[0].astype(BF16), row(final_norm_g), seq=t)
    return out.reshape(b, t, d)
```

```python
import functools
import math

import jax
import jax.numpy as jnp
from jax import lax
from jax.experimental import pallas as pl
from jax.experimental.pallas import tpu as pltpu

HEAD_DIM = 128
MOBA_BLOCK = 256
MOBA_TOPK = 3
ROPE_THETA = 10000.0
N_XATTN_HEADS = 4
CONV_WIDTH = 3
RMS_EPS = 1e-6
NEG = -1e30
SCALE = 1.0 / math.sqrt(HEAD_DIM)
LOG2E = math.log2(math.e)
QSCALE = SCALE * LOG2E

LANES = 128
SUBLANES = 8
VMEM_LIMIT_BYTES = 56 * 1024 * 1024

F32 = jnp.float32
BF16 = jnp.bfloat16

_NT = (((1,), (1,)), ((), ()))


def _cparams(*sem):
    return pltpu.CompilerParams(dimension_semantics=sem, vmem_limit_bytes=VMEM_LIMIT_BYTES)


def _rmsnorm(x, g):
    ms = jnp.mean(x * x, axis=-1, keepdims=True)
    return x * lax.rsqrt(ms + RMS_EPS) * g


def _split3(x):
    hi = x.astype(BF16)
    r = x - hi.astype(F32)
    mid = r.astype(BF16)
    lo = (r - mid.astype(F32)).astype(BF16)
    return hi, mid, lo


def _in_proj_kernel(xn_ref, w_ref, cos_ref, sin_ref, o_ref, w_sc, *, group_of_tile):
    j = pl.program_id(0)

    @pl.when(pl.program_id(1) == 0)
    def _():
        w_sc[...] = w_ref[...].astype(BF16)

    grp = group_of_tile(j)
    rot = (grp <= 1).astype(F32)
    scale = jnp.where((grp == 0) | (grp == 3), QSCALE, 1.0).astype(F32)
    a = (cos_ref[...] * rot + (1.0 - rot)) * scale
    b = sin_ref[...] * (rot * scale)
    tm, tn = o_ref.shape
    chunk = min(256, tm)
    for r0 in range(0, tm, chunk):
        rows = slice(r0, r0 + chunk)
        y = lax.dot_general(xn_ref[rows, :], w_sc[...], _NT, preferred_element_type=F32)
        for c0 in range(0, tn, HEAD_DIM):
            yh = y[:, c0:c0 + HEAD_DIM]
            out = yh * a[rows] + pltpu.roll(yh, HEAD_DIM // 2, 1) * b[rows]
            o_ref[rows, c0:c0 + HEAD_DIM] = out.astype(o_ref.dtype)


def _in_proj(xn, w_in_t, cos, sin, *, seq, group_width):
    n, d = xn.shape
    cols = 6 * group_width
    tm = min(2048, seq)
    tn = min(1024, group_width)
    tiles_per_group = group_width // tn
    t_tiles = seq // tm
    kern = functools.partial(_in_proj_kernel, group_of_tile=lambda j: j // tiles_per_group)
    return pl.pallas_call(
        kern,
        grid=(cols // tn, n // tm),
        in_specs=[
            pl.BlockSpec((tm, d), lambda j, i: (i, 0)),
            pl.BlockSpec((tn, d), lambda j, i: (j, 0)),
            pl.BlockSpec((tm, HEAD_DIM), lambda j, i: (i % t_tiles, 0)),
            pl.BlockSpec((tm, HEAD_DIM), lambda j, i: (i % t_tiles, 0)),
        ],
        out_specs=pl.BlockSpec((tm, tn), lambda j, i: (i, j)),
        out_shape=jax.ShapeDtypeStruct((n, cols), BF16),
        scratch_shapes=[pltpu.VMEM((tn, d), BF16)],
        compiler_params=_cparams("arbitrary", "arbitrary"),
        name="in_proj",
    )(xn, w_in_t, cos, sin)


def _norm_gate_kernel(x_ref, g_ref, wz_ref, bf_ref, route_ref, xn_ref, qx_ref, kx_ref, carry_sc,
                      *, n_heads):
    t = pl.program_id(1)

    @pl.when(t == 0)
    def _():
        carry_sc[...] = jnp.zeros_like(carry_sc)

    xn = _rmsnorm(x_ref[...], g_ref[...]).astype(BF16)
    xn_ref[...] = xn
    z = lax.dot_general(xn, wz_ref[...], _NT, preferred_element_type=F32) + bf_ref[...]
    lf = jnp.minimum(z, 0.0) - jnp.log(1.0 + jnp.exp(-jnp.abs(z)))

    tc = lf.shape[0]
    row = lax.broadcasted_iota(jnp.int32, (tc, tc), 0)
    col = lax.broadcasted_iota(jnp.int32, (tc, tc), 1)
    tri = (col <= row).astype(BF16)
    c = carry_sc[0:1, :]
    for part in _split3(lf):
        c = c + jnp.dot(tri, part, preferred_element_type=F32)
    carry_sc[0:1, :] = c[tc - 1:tc, :]

    parts = jnp.concatenate(_split3(c * LOG2E), axis=1)
    routed = jnp.dot(parts, route_ref[...], preferred_element_type=F32)
    lane = lax.broadcasted_iota(jnp.int32, (tc, LANES), 1)
    q_ones = ((lane >= 3) & (lane < 6)).astype(F32)
    k_ones = (lane < 3).astype(F32)
    for h in range(n_heads):
        qx_ref[h] = (routed[:, (2 * h) * LANES:(2 * h + 1) * LANES] + q_ones).astype(BF16)
        kx_ref[h] = (routed[:, (2 * h + 1) * LANES:(2 * h + 2) * LANES] + k_ones).astype(BF16)


def _gate_routing(n_heads):
    import numpy as np
    r = np.zeros((3 * LANES, n_heads * 2 * LANES), np.float32)
    for h in range(n_heads):
        for p in range(3):
            r[p * LANES + h, (2 * h) * LANES + p] = 1.0
            r[p * LANES + h, (2 * h + 1) * LANES + 3 + p] = -1.0
    return jnp.asarray(r, BF16)


def _norm_gate(x3, g, w_z, b_f, n_heads):
    b, t, d = x3.shape
    tc = min(512, t)
    ext = jax.ShapeDtypeStruct((b, n_heads, t, LANES), BF16)
    ext_spec = pl.BlockSpec((None, n_heads, tc, LANES), lambda bi, ti: (bi, 0, ti, 0))
    const = lambda bi, ti: (0, 0)
    return pl.pallas_call(
        functools.partial(_norm_gate_kernel, n_heads=n_heads),
        grid=(b, t // tc),
        in_specs=[
            pl.BlockSpec((None, tc, d), lambda bi, ti: (bi, ti, 0)),
            pl.BlockSpec((1, d), const),
            pl.BlockSpec((LANES, d), const),
            pl.BlockSpec((1, LANES), const),
            pl.BlockSpec((3 * LANES, n_heads * 2 * LANES), const),
        ],
        out_specs=[pl.BlockSpec((None, tc, d), lambda bi, ti: (bi, ti, 0)), ext_spec, ext_spec],
        out_shape=[jax.ShapeDtypeStruct((b, t, d), BF16), ext, ext],
        scratch_shapes=[pltpu.VMEM((SUBLANES, LANES), F32)],
        compiler_params=_cparams("arbitrary", "arbitrary"),
        name="norm_gate",
    )(x3, g, w_z, b_f, _gate_routing(n_heads))


ATT_TILE = 512
ATT_HEADS = 4
VT_ROWS = HEAD_DIM + 16


def _flash_init(s, v):
    m = jnp.max(s, axis=1, keepdims=True)
    p = jnp.exp(s - m)
    l = jnp.sum(p, axis=1, keepdims=True)
    acc = jnp.dot(p.astype(BF16), v, preferred_element_type=F32)
    return m, l, acc


def _moba_bias(q, kmean, iq):
    tq = q.shape[0]
    nb = kmean.shape[0]
    km_hi = kmean.astype(BF16)
    km_lo = (kmean - km_hi.astype(F32)).astype(BF16)
    gate = (lax.dot_general(km_hi, q, _NT, preferred_element_type=F32)
            + lax.dot_general(km_lo, q, _NT, preferred_element_type=F32))
    blk_i = lax.broadcasted_iota(jnp.int32, gate.shape, 0)
    qry_i = lax.broadcasted_iota(jnp.int32, gate.shape, 1)
    own = iq * (tq // MOBA_BLOCK) + qry_i // MOBA_BLOCK
    blk_f = blk_i.astype(F32)
    past = blk_i < own
    g = jnp.where(past, gate, -jnp.inf)
    sel = jnp.zeros(gate.shape, jnp.bool_)
    for _ in range(MOBA_TOPK):
        mx = jnp.max(g, axis=0, keepdims=True)
        idx = jnp.min(jnp.where(g == mx, blk_f, float(LANES)), axis=0, keepdims=True)
        pick = blk_f == idx
        sel = sel | pick
        g = jnp.where(pick, -jnp.inf, g)
    bias_t = jnp.where((sel & past) | (blk_i == own), 0.0, NEG)
    bias_t = jnp.concatenate([bias_t, jnp.zeros((LANES - nb, tq), F32)], axis=0)
    return bias_t.T.astype(BF16)


def _attn_kernel(*refs, moba):
    if moba:
        q_ref, k_ref, v_ref, o_ref = refs[:4]
        kmean_sc = refs[-1]
        refs = refs[:-1]
    else:
        q_ref, k_ref, v_ref, qx_ref, kx_ref, o_ref = refs[:6]
    kaug_sc, vt_sc, sa_sc, sb_sc, mxa_sc, mxb_sc, m_sc, acc_sc = refs[-8:]
    iq = pl.program_id(2)
    tq = q_ref.shape[0]
    t = k_ref.shape[0]
    n_heads = q_ref.shape[1] // HEAD_DIM
    heads = [slice(g * HEAD_DIM, (g + 1) * HEAD_DIM) for g in range(n_heads)]

    @pl.when(iq == 0)
    def _():
        for g, hs in enumerate(heads):
            kaug_sc[g, :, :HEAD_DIM] = k_ref[:, hs]
            if moba:
                row = lax.broadcasted_iota(jnp.int32, (t, LANES), 0)
                lane = lax.broadcasted_iota(jnp.int32, (t, LANES), 1)
                kaug_sc[g, :, HEAD_DIM:] = ((row // MOBA_BLOCK) == lane).astype(BF16)
                kmean_sc[g] = jnp.zeros((LANES, HEAD_DIM), F32)
                for n in range(t // MOBA_BLOCK):
                    kb = k_ref[n * MOBA_BLOCK:(n + 1) * MOBA_BLOCK, hs].astype(F32)
                    kmean_sc[g, n:n + 1, :] = jnp.mean(kb, axis=0, keepdims=True)
            else:
                kaug_sc[g, :, HEAD_DIM:] = kx_ref[g]
            pad_row = lax.broadcasted_iota(jnp.int32, (VT_ROWS - HEAD_DIM, tq), 0)
            for j in range(t // tq):
                vt_sc[g, j, :HEAD_DIM, :] = (
                    v_ref[j * tq:(j + 1) * tq, hs].astype(F32).T.astype(BF16))
                vt_sc[g, j, HEAD_DIM:, :] = (pad_row == 0).astype(BF16)

    qa = []
    for g, hs in enumerate(heads):
        q = q_ref[:, hs]
        if moba:
            nb = -(-(t // MOBA_BLOCK) // 16) * 16
            extra = _moba_bias(q, kmean_sc[g, :nb, :], iq)
        else:
            extra = qx_ref[g]
        qa.append(jnp.concatenate([q, extra], axis=1))

    def scores(g, j):
        off = pl.multiple_of(j * tq, tq)
        return lax.dot_general(kaug_sc[g, pl.ds(off, tq), :], qa[g], _NT,
                               preferred_element_type=F32)

    half = tq // 2
    halves = (slice(0, half), slice(half, tq))

    def produce(j, s_ref, mx_ref):
        for g in range(n_heads):
            s = scores(g, j)
            s_ref[g] = s
            mx_ref[g] = jnp.max(s, axis=0, keepdims=True)

    def consume(j, s_ref, mx_ref, diagonal=False):
        for g in range(n_heads):
            m = m_sc[g]
            m_new = jnp.maximum(m, mx_ref[g])
            alpha = jnp.exp2(m - m_new)
            m_sc[g] = m_new
            for qi, qs in enumerate(halves):
                pv = alpha[:, qs] * acc_sc[g, :, qs]
                for ki, ks in enumerate(halves):
                    if diagonal and ki > qi:
                        continue
                    p = jnp.exp2(s_ref[g, ks, qs] - m_new[:, qs])
                    pv = pv + jnp.dot(vt_sc[g, j, :, ks], p.astype(BF16),
                                      preferred_element_type=F32)
                acc_sc[g, :, qs] = pv

    for g in range(n_heads):
        m_sc[g] = jnp.full((1, tq), NEG, F32)
        acc_sc[g] = jnp.zeros((VT_ROWS, tq), F32)
    produce(0, sa_sc, mxa_sc)

    def body(jj, carry):
        j = 2 * jj
        produce(j + 1, sb_sc, mxb_sc)
        consume(j, sa_sc, mxa_sc)
        produce(j + 2, sa_sc, mxa_sc)
        consume(j + 1, sb_sc, mxb_sc)
        return carry

    lax.fori_loop(0, iq // 2, body, 0)

    def finish(s_ref, mx_ref):
        key = lax.broadcasted_iota(jnp.int32, (half, half), 0)
        qry = lax.broadcasted_iota(jnp.int32, (half, half), 1)
        lo, hi = halves
        for g in range(n_heads):
            d_lo = jnp.where(key <= qry, s_ref[g, lo, lo], NEG)
            d_hi = jnp.where(key <= qry, s_ref[g, hi, hi], NEG)
            s_ref[g, lo, lo] = d_lo
            s_ref[g, hi, hi] = d_hi
            mx_ref[g, :, lo] = jnp.max(d_lo, axis=0, keepdims=True)
            mx_ref[g, :, hi] = jnp.maximum(jnp.max(s_ref[g, lo, hi], axis=0, keepdims=True),
                                           jnp.max(d_hi, axis=0, keepdims=True))
        consume(iq, s_ref, mx_ref, diagonal=True)
        for g, hs in enumerate(heads):
            out = acc_sc[g, :HEAD_DIM, :] / acc_sc[g, HEAD_DIM:HEAD_DIM + 1, :]
            o_ref[:, hs] = out.T.astype(o_ref.dtype)

    @pl.when(iq % 2 == 0)
    def _():
        finish(sa_sc, mxa_sc)

    @pl.when(iq % 2 == 1)
    def _():
        produce(iq, sb_sc, mxb_sc)
        consume(iq - 1, sa_sc, mxa_sc)
        finish(sb_sc, mxb_sc)


def _attn(proj3, n_heads, head0, fox_extras=None):
    b, t, _ = proj3.shape
    moba = fox_extras is None
    tq = min(ATT_TILE, t)
    g = min(ATT_HEADS, n_heads)
    assert t % tq == 0 and tq % MOBA_BLOCK == 0 and t // MOBA_BLOCK <= LANES and n_heads % g == 0
    w = g * HEAD_DIM
    c0 = head0 // g
    in_specs = [
        pl.BlockSpec((None, tq, w), lambda bi, h, i: (bi, i, c0 + h)),
        pl.BlockSpec((None, t, w), lambda bi, h, i: (bi, 0, c0 + n_heads // g + h)),
        pl.BlockSpec((None, t, w), lambda bi, h, i: (bi, 0, c0 + 2 * (n_heads // g) + h)),
    ]
    args = [proj3, proj3, proj3]
    scratch = [
        pltpu.VMEM((g, t, 2 * HEAD_DIM), BF16),
        pltpu.VMEM((g, t // tq, VT_ROWS, tq), BF16),
        pltpu.VMEM((g, tq, tq), F32),
        pltpu.VMEM((g, tq, tq), F32),
        pltpu.VMEM((g, 1, tq), F32),
        pltpu.VMEM((g, 1, tq), F32),
        pltpu.VMEM((g, 1, tq), F32),
        pltpu.VMEM((g, VT_ROWS, tq), F32),
    ]
    if moba:
        scratch.append(pltpu.VMEM((g, LANES, HEAD_DIM), F32))
    else:
        in_specs += [
            pl.BlockSpec((None, g, tq, LANES), lambda bi, h, i: (bi, h, i, 0)),
            pl.BlockSpec((None, g, t, LANES), lambda bi, h, i: (bi, h, 0, 0)),
        ]
        args += list(fox_extras)
    return pl.pallas_call(
        functools.partial(_attn_kernel, moba=moba),
        grid=(b, n_heads // g, t // tq),
        in_specs=in_specs,
        out_specs=pl.BlockSpec((None, tq, w), lambda bi, h, i: (bi, i, h)),
        out_shape=jax.ShapeDtypeStruct((b, t, n_heads * HEAD_DIM), BF16),
        scratch_shapes=scratch,
        compiler_params=_cparams("arbitrary", "arbitrary", "arbitrary"),
        name="moba_attn" if moba else "fox_attn",
    )(*args)


def _mem_kv_kernel(mem_ref, g_ref, w_ref, k_ref, v_ref):
    xn = _rmsnorm(mem_ref[...], g_ref[...]).astype(BF16)
    kv = jnp.dot(xn, w_ref[...], preferred_element_type=F32)
    half = kv.shape[1] // 2
    k_ref[...] = kv[:, :half].astype(k_ref.dtype)
    v_ref[...] = kv[:, half:].astype(v_ref.dtype)


def _mem_kv(mem, g, w_ckv):
    b, nm, d = mem.shape
    xd = w_ckv.shape[1] // 2
    out = jax.ShapeDtypeStruct((b, nm, xd), BF16)
    o_spec = pl.BlockSpec((None, nm, xd), lambda bi: (bi, 0, 0))
    return pl.pallas_call(
        _mem_kv_kernel,
        grid=(b,),
        in_specs=[
            pl.BlockSpec((None, nm, d), lambda bi: (bi, 0, 0)),
            pl.BlockSpec((1, d), lambda bi: (0, 0)),
            pl.BlockSpec((d, 2 * xd), lambda bi: (0, 0)),
        ],
        out_specs=[o_spec, o_spec],
        out_shape=[out, out],
        compiler_params=_cparams("arbitrary"),
        name="mem_kv",
    )(mem, g, w_ckv)


def _mid_kernel(x_ref, om_ref, of_ref, wom_ref, wof_ref, gx_ref, wcq_ref, kc_ref, vc_ref,
                wco_ref, gf_ref, h_ref, xn_ref):
    h1 = (x_ref[...]
          + jnp.dot(om_ref[...], wom_ref[...], preferred_element_type=F32)
          + jnp.dot(of_ref[...], wof_ref[...], preferred_element_type=F32))
    xn2 = _rmsnorm(h1, gx_ref[...]).astype(BF16)
    q = (jnp.dot(xn2, wcq_ref[...], preferred_element_type=F32) * SCALE).astype(BF16)
    outs = []
    for h in range(N_XATTN_HEADS):
        sl = slice(h * HEAD_DIM, (h + 1) * HEAD_DIM)
        s = lax.dot_general(q[:, sl], kc_ref[:, sl], _NT, preferred_element_type=F32)
        m, l, acc = _flash_init(s, vc_ref[:, sl])
        outs.append((acc / l).astype(BF16))
    oc = jnp.concatenate(outs, axis=1)
    h2 = h1 + jnp.dot(oc, wco_ref[...], preferred_element_type=F32)
    h_ref[...] = h2
    xn_ref[...] = _rmsnorm(h2, gf_ref[...]).astype(xn_ref.dtype)


def _mid(x2, o_moba, o_fox, w_om, w_of, gx, w_cq, kc, vc, w_co, gf, *, seq):
    n, d = x2.shape
    wm, wf = o_moba.shape[1], o_fox.shape[1]
    nm, xd = kc.shape[1], kc.shape[2]
    tm = min(512, seq)
    t_tiles = seq // tm
    const = lambda i: (0, 0)
    return pl.pallas_call(
        _mid_kernel,
        grid=(n // tm,),
        in_specs=[
            pl.BlockSpec((tm, d), lambda i: (i, 0)),
            pl.BlockSpec((tm, wm), lambda i: (i, 0)),
            pl.BlockSpec((tm, wf), lambda i: (i, 0)),
            pl.BlockSpec((wm, d), const),
            pl.BlockSpec((wf, d), const),
            pl.BlockSpec((1, d), const),
            pl.BlockSpec((d, xd), const),
            pl.BlockSpec((None, nm, xd), lambda i: (i // t_tiles, 0, 0)),
            pl.BlockSpec((None, nm, xd), lambda i: (i // t_tiles, 0, 0)),
            pl.BlockSpec((xd, d), const),
            pl.BlockSpec((1, d), const),
        ],
        out_specs=[
            pl.BlockSpec((tm, d), lambda i: (i, 0)),
            pl.BlockSpec((tm, d), lambda i: (i, 0)),
        ],
        out_shape=[
            jax.ShapeDtypeStruct((n, d), F32),
            jax.ShapeDtypeStruct((n, d), BF16),
        ],
        compiler_params=_cparams("arbitrary"),
        name="mid",
    )(x2, o_moba, o_fox, w_om, w_of, gx, w_cq, kc, vc, w_co, gf)


def _conv_ffn_kernel(xn_ref, h_ref, wg_ref, wu_ref, cwg_ref, cwu_ref, cbg_ref, cbu_ref,
                     wd_ref, fg_ref, o_ref, hs_sc, tail_sc, *, tiles_per_seq, n_res):
    i = pl.program_id(0)
    f = pl.program_id(1)
    nf = pl.num_programs(1)
    tm = xn_ref.shape[0]
    tf = wg_ref.shape[1]
    halo = SUBLANES
    first = (i % tiles_per_seq) == 0

    @pl.when((i == 0) & (f == 0))
    def _():
        tail_sc[...] = jnp.zeros_like(tail_sc)

    def up():
        xn = xn_ref[...]
        hs_sc[0:halo, :] = jnp.where(first, 0.0, tail_sc[f])
        hs_sc[halo:, :tf] = jnp.dot(xn, wg_ref[...], preferred_element_type=F32)
        hs_sc[halo:, tf:] = jnp.dot(xn, wu_ref[...], preferred_element_type=F32)
        tail_sc[f] = hs_sc[tm:tm + halo, :]

    def gate(cols):
        def conv(base, cw_ref, cb_ref):
            y = cb_ref[:, cols]
            for k in range(CONV_WIDTH):
                lo = halo - (CONV_WIDTH - 1) + k
                y = y + hs_sc[lo:lo + tm, base + cols.start:base + cols.stop] * cw_ref[k:k + 1, cols]
            return y
        gte = conv(0, cwg_ref, cbg_ref)
        return (gte * (1.0 / (1.0 + jnp.exp2(gte * (-LOG2E)))) * conv(tf, cwu_ref, cbu_ref)
                ).astype(BF16)

    res_rows = h_ref.shape[0]

    @pl.when(f == 0)
    def _():
        o_ref[...] = jnp.zeros_like(o_ref)

    @pl.when(f < n_res)
    def _():
        r0 = pl.multiple_of(f * res_rows, res_rows)
        o_ref[pl.ds(r0, res_rows), :] += h_ref[...]

    up()
    hk = tf // 2
    o_ref[...] += (
        jnp.dot(gate(slice(0, hk)), wd_ref[:hk, :], preferred_element_type=F32)
        + jnp.dot(gate(slice(hk, tf)), wd_ref[hk:, :], preferred_element_type=F32))

    @pl.when(f == nf - 1)
    def _():
        o_ref[...] = _rmsnorm(o_ref[...], fg_ref[...])


def _conv_ffn(xn3, h2, w_up, conv_w, conv_b, w_down, fg, *, seq):
    n, d = h2.shape
    ff = w_down.shape[0]
    tm = min(1024, seq)
    tf = 512
    assert ff % tf == 0
    nf = ff // tf
    n_res = 8 if nf >= 8 else 4
    assert nf >= n_res and tm % (n_res * SUBLANES) == 0
    return pl.pallas_call(
        functools.partial(_conv_ffn_kernel, tiles_per_seq=seq // tm, n_res=n_res),
        grid=(n // tm, nf),
        in_specs=[
            pl.BlockSpec((tm, d), lambda i, f: (i, 0)),
            pl.BlockSpec((tm // n_res, d), lambda i, f: (i * n_res + jnp.minimum(f, n_res - 1), 0)),
            pl.BlockSpec((d, tf), lambda i, f: (0, f)),
            pl.BlockSpec((d, tf), lambda i, f: (0, f + nf)),
            pl.BlockSpec((CONV_WIDTH, tf), lambda i, f: (0, f)),
            pl.BlockSpec((CONV_WIDTH, tf), lambda i, f: (0, f + nf)),
            pl.BlockSpec((1, tf), lambda i, f: (0, f)),
            pl.BlockSpec((1, tf), lambda i, f: (0, f + nf)),
            pl.BlockSpec((tf, d), lambda i, f: (f, 0)),
            pl.BlockSpec((1, d), lambda i, f: (0, 0)),
        ],
        out_specs=pl.BlockSpec((tm, d), lambda i, f: (i, 0)),
        out_shape=jax.ShapeDtypeStruct((n, d), F32),
        scratch_shapes=[
            pltpu.VMEM((tm + SUBLANES, 2 * tf), F32),
            pltpu.VMEM((nf, SUBLANES, 2 * tf), F32),
        ],
        compiler_params=_cparams("arbitrary", "arbitrary"),
        name="conv_ffn",
    )(xn3, h2, w_up, w_up, conv_w, conv_w, conv_b, conv_b, w_down, fg)


def _rope_tables(t):
    half = HEAD_DIM // 2
    inv = ROPE_THETA ** (-jnp.arange(half, dtype=F32) / half)
    ang = jnp.arange(t, dtype=F32)[:, None] * inv[None, :]
    cos, sin = jnp.cos(ang), jnp.sin(ang)
    return jnp.concatenate([cos, cos], axis=1), jnp.concatenate([-sin, sin], axis=1)


def kernel(x, mem, attn_norm_g, w_in, b_f, w_o, xattn_norm_g, mem_norm_g, w_cq, w_ckv, w_co,
           ffn_norm_g, w_up, conv_w, conv_b, w_down, final_norm_g):
    b, t, d = x.shape
    assert w_in.shape[0] == 1, "one layer: the final rmsnorm is fused into the ffn kernel"
    n_heads = d // HEAD_DIM
    n_moba = n_heads // 2
    n_fox = n_heads - n_moba
    assert n_moba == n_fox and n_fox <= LANES
    wgrp = n_moba * HEAD_DIM
    main_cols = 6 * wgrp
    cos, sin = _rope_tables(t)
    row = lambda v: v.reshape(1, -1)

    w_in_t = w_in[0].T
    w_z_t = jnp.pad(w_in_t[main_cols:], ((0, LANES - n_fox), (0, 0))).astype(BF16)
    bf = row(jnp.pad(b_f[0].astype(F32), (0, LANES - n_fox)))
    xn, qx, kx = _norm_gate(x, row(attn_norm_g[0]), w_z_t, bf, n_fox)
    proj = _in_proj(xn.reshape(b * t, d), w_in_t, cos, sin, seq=t, group_width=wgrp)
    proj3 = proj.reshape(b, t, main_cols)
    o_moba = _attn(proj3, n_moba, 0)
    o_fox = _attn(proj3, n_fox, 3 * n_moba, (qx, kx))
    kc, vc = _mem_kv(mem, row(mem_norm_g[0]), w_ckv[0].astype(BF16))
    wo = w_o[0].astype(BF16)
    h2, xn3 = _mid(x.reshape(b * t, d), o_moba.reshape(b * t, wgrp), o_fox.reshape(b * t, wgrp),
                   wo[:wgrp], wo[wgrp:], row(xattn_norm_g[0]), w_cq[0].astype(BF16), kc, vc,
                   w_co[0].astype(BF16), row(ffn_norm_g[0]), seq=t)
    out = _conv_ffn(xn3, h2, w_up[0].astype(BF16), conv_w[0], row(conv_b[0]),
                    w_down[0].astype(BF16), row(final_norm_g), seq=t)
    return out.reshape(b, t, d)
```

```python
import functools
import math

import jax
import jax.numpy as jnp
import numpy as np
from jax import lax
from jax.experimental import pallas as pl
from jax.experimental.pallas import tpu as pltpu

HEAD_DIM = 128
MOBA_BLOCK = 256
MOBA_TOPK = 3
ROPE_THETA = 10000.0
N_XATTN_HEADS = 4
CONV_WIDTH = 3
RMS_EPS = 1e-6
NEG = -1e30
SCALE = 1.0 / math.sqrt(HEAD_DIM)
LOG2E = math.log2(math.e)
QSCALE = SCALE * LOG2E

LANES = 128
SUBLANES = 8
VMEM_LIMIT_BYTES = 56 * 1024 * 1024
FFN_VMEM_LIMIT_BYTES = 62 * 1024 * 1024

F32 = jnp.float32
BF16 = jnp.bfloat16

_NT = (((1,), (1,)), ((), ()))


def _cparams(*sem, vmem_limit_bytes=VMEM_LIMIT_BYTES):
    return pltpu.CompilerParams(dimension_semantics=sem, vmem_limit_bytes=vmem_limit_bytes)


def _rmsnorm(x, g):
    ms = jnp.mean(x * x, axis=-1, keepdims=True)
    return x * lax.rsqrt(ms + RMS_EPS) * g


def _split3(x):
    hi = x.astype(BF16)
    r = x - hi.astype(F32)
    mid = r.astype(BF16)
    lo = (r - mid.astype(F32)).astype(BF16)
    return hi, mid, lo


def _in_proj_kernel(xn_ref, w_ref, cos_ref, sin_ref, o_ref, w_sc, *, group_of_tile):
    j = pl.program_id(0)

    @pl.when(pl.program_id(1) == 0)
    def _():
        w_sc[...] = w_ref[...].astype(BF16)

    grp = group_of_tile(j)
    rot = (grp <= 1).astype(F32)
    scale = jnp.where((grp == 0) | (grp == 3), QSCALE, 1.0).astype(F32)
    a = (cos_ref[...] * rot + (1.0 - rot)) * scale
    b = sin_ref[...] * (rot * scale)
    tm, tn = o_ref.shape
    chunk = min(256, tm)
    for r0 in range(0, tm, chunk):
        rows = slice(r0, r0 + chunk)
        y = lax.dot_general(xn_ref[rows, :], w_sc[...], _NT, preferred_element_type=F32)
        for c0 in range(0, tn, HEAD_DIM):
            yh = y[:, c0:c0 + HEAD_DIM]
            out = yh * a[rows] + pltpu.roll(yh, HEAD_DIM // 2, 1) * b[rows]
            o_ref[rows, c0:c0 + HEAD_DIM] = out.astype(o_ref.dtype)


def _in_proj(xn, w_in_t, cos, sin, *, seq, group_width):
    n, d = xn.shape
    cols = 6 * group_width
    tm = min(2048, seq)
    tn = min(1024, group_width)
    tiles_per_group = group_width // tn
    t_tiles = seq // tm
    kern = functools.partial(_in_proj_kernel, group_of_tile=lambda j: j // tiles_per_group)
    return pl.pallas_call(
        kern,
        grid=(cols // tn, n // tm),
        in_specs=[
            pl.BlockSpec((tm, d), lambda j, i: (i, 0)),
            pl.BlockSpec((tn, d), lambda j, i: (j, 0)),
            pl.BlockSpec((tm, HEAD_DIM), lambda j, i: (i % t_tiles, 0)),
            pl.BlockSpec((tm, HEAD_DIM), lambda j, i: (i % t_tiles, 0)),
        ],
        out_specs=pl.BlockSpec((tm, tn), lambda j, i: (i, j)),
        out_shape=jax.ShapeDtypeStruct((n, cols), BF16),
        scratch_shapes=[pltpu.VMEM((tn, d), BF16)],
        compiler_params=_cparams("arbitrary", "arbitrary"),
        name="in_proj",
    )(xn, w_in_t, cos, sin)


def _norm_gate_kernel(x_ref, g_ref, wz_ref, bf_ref, route_ref, xn_ref, qx_ref, kx_ref, carry_sc,
                      *, n_heads):
    t = pl.program_id(1)

    @pl.when(t == 0)
    def _():
        carry_sc[...] = jnp.zeros_like(carry_sc)

    xn = _rmsnorm(x_ref[...], g_ref[...]).astype(BF16)
    xn_ref[...] = xn
    z = lax.dot_general(xn, wz_ref[...], _NT, preferred_element_type=F32) + bf_ref[...]
    lf = jnp.minimum(z, 0.0) - jnp.log(1.0 + jnp.exp(-jnp.abs(z)))

    tc = lf.shape[0]
    row = lax.broadcasted_iota(jnp.int32, (tc, tc), 0)
    col = lax.broadcasted_iota(jnp.int32, (tc, tc), 1)
    tri = (col <= row).astype(BF16)
    c = carry_sc[0:1, :]
    for part in _split3(lf):
        c = c + jnp.dot(tri, part, preferred_element_type=F32)
    carry_sc[0:1, :] = c[tc - 1:tc, :]

    parts = jnp.concatenate(_split3(c * LOG2E), axis=1)
    routed = jnp.dot(parts, route_ref[...], preferred_element_type=F32)
    lane = lax.broadcasted_iota(jnp.int32, (tc, LANES), 1)
    q_ones = ((lane >= 3) & (lane < 6)).astype(F32)
    k_ones = (lane < 3).astype(F32)
    for h in range(n_heads):
        qx_ref[h] = (routed[:, (2 * h) * LANES:(2 * h + 1) * LANES] + q_ones).astype(BF16)
        kx_ref[h] = (routed[:, (2 * h + 1) * LANES:(2 * h + 2) * LANES] + k_ones).astype(BF16)


def _gate_routing(n_heads):
    r = np.zeros((3 * LANES, n_heads * 2 * LANES), np.float32)
    for h in range(n_heads):
        for p in range(3):
            r[p * LANES + h, (2 * h) * LANES + p] = 1.0
            r[p * LANES + h, (2 * h + 1) * LANES + 3 + p] = -1.0
    return jnp.asarray(r, BF16)


def _norm_gate(x3, g, w_z, b_f, n_heads):
    b, t, d = x3.shape
    tc = min(512, t)
    ext = jax.ShapeDtypeStruct((b, n_heads, t, LANES), BF16)
    ext_spec = pl.BlockSpec((None, n_heads, tc, LANES), lambda bi, ti: (bi, 0, ti, 0))
    const = lambda bi, ti: (0, 0)
    return pl.pallas_call(
        functools.partial(_norm_gate_kernel, n_heads=n_heads),
        grid=(b, t // tc),
        in_specs=[
            pl.BlockSpec((None, tc, d), lambda bi, ti: (bi, ti, 0)),
            pl.BlockSpec((1, d), const),
            pl.BlockSpec((LANES, d), const),
            pl.BlockSpec((1, LANES), const),
            pl.BlockSpec((3 * LANES, n_heads * 2 * LANES), const),
        ],
        out_specs=[pl.BlockSpec((None, tc, d), lambda bi, ti: (bi, ti, 0)), ext_spec, ext_spec],
        out_shape=[jax.ShapeDtypeStruct((b, t, d), BF16), ext, ext],
        scratch_shapes=[pltpu.VMEM((SUBLANES, LANES), F32)],
        compiler_params=_cparams("arbitrary", "arbitrary"),
        name="norm_gate",
    )(x3, g, w_z, b_f, _gate_routing(n_heads))


ATT_TILE = 512
ATT_HEADS = 4
VT_ROWS = HEAD_DIM + 16


def _flash_init(s, v):
    m = jnp.max(s, axis=1, keepdims=True)
    p = jnp.exp(s - m)
    l = jnp.sum(p, axis=1, keepdims=True)
    acc = jnp.dot(p.astype(BF16), v, preferred_element_type=F32)
    return m, l, acc


def _moba_bias(q, kmean, iq):
    tq = q.shape[0]
    nb = kmean.shape[0]
    km_hi = kmean.astype(BF16)
    km_lo = (kmean - km_hi.astype(F32)).astype(BF16)
    gate = (lax.dot_general(km_hi, q, _NT, preferred_element_type=F32)
            + lax.dot_general(km_lo, q, _NT, preferred_element_type=F32))
    blk_i = lax.broadcasted_iota(jnp.int32, gate.shape, 0)
    qry_i = lax.broadcasted_iota(jnp.int32, gate.shape, 1)
    own = iq * (tq // MOBA_BLOCK) + qry_i // MOBA_BLOCK
    blk_f = blk_i.astype(F32)
    past = blk_i < own
    g = jnp.where(past, gate, -jnp.inf)
    sel = jnp.zeros(gate.shape, jnp.bool_)
    for _ in range(MOBA_TOPK):
        mx = jnp.max(g, axis=0, keepdims=True)
        idx = jnp.min(jnp.where(g == mx, blk_f, float(LANES)), axis=0, keepdims=True)
        pick = blk_f == idx
        sel = sel | pick
        g = jnp.where(pick, -jnp.inf, g)
    bias_t = jnp.where((sel & past) | (blk_i == own), 0.0, NEG)
    bias_t = jnp.concatenate([bias_t, jnp.zeros((LANES - nb, tq), F32)], axis=0)
    return bias_t.T.astype(BF16)


def _attn_kernel(*refs, moba):
    if moba:
        q_ref, k_ref, v_ref, o_ref = refs[:4]
        kmean_sc = refs[-1]
        refs = refs[:-1]
    else:
        q_ref, k_ref, v_ref, qx_ref, kx_ref, o_ref = refs[:6]
    kaug_sc, vt_sc, sa_sc, sb_sc, mxa_sc, mxb_sc, m_sc, acc_sc = refs[-8:]
    iq = pl.program_id(2)
    tq = q_ref.shape[0]
    t = k_ref.shape[0]
    n_heads = q_ref.shape[1] // HEAD_DIM
    heads = [slice(g * HEAD_DIM, (g + 1) * HEAD_DIM) for g in range(n_heads)]

    @pl.when(iq == 0)
    def _():
        for g, hs in enumerate(heads):
            kaug_sc[g, :, :HEAD_DIM] = k_ref[:, hs]
            if moba:
                row = lax.broadcasted_iota(jnp.int32, (t, LANES), 0)
                lane = lax.broadcasted_iota(jnp.int32, (t, LANES), 1)
                kaug_sc[g, :, HEAD_DIM:] = ((row // MOBA_BLOCK) == lane).astype(BF16)
                kmean_sc[g] = jnp.zeros((LANES, HEAD_DIM), F32)
                for n in range(t // MOBA_BLOCK):
                    kb = k_ref[n * MOBA_BLOCK:(n + 1) * MOBA_BLOCK, hs].astype(F32)
                    kmean_sc[g, n:n + 1, :] = jnp.mean(kb, axis=0, keepdims=True)
            else:
                kaug_sc[g, :, HEAD_DIM:] = kx_ref[g]
            pad_row = lax.broadcasted_iota(jnp.int32, (VT_ROWS - HEAD_DIM, tq), 0)
            for j in range(t // tq):
                vt_sc[g, j, :HEAD_DIM, :] = (
                    v_ref[j * tq:(j + 1) * tq, hs].astype(F32).T.astype(BF16))
                vt_sc[g, j, HEAD_DIM:, :] = (pad_row == 0).astype(BF16)

    qa = []
    for g, hs in enumerate(heads):
        q = q_ref[:, hs]
        if moba:
            nb = -(-(t // MOBA_BLOCK) // 16) * 16
            extra = _moba_bias(q, kmean_sc[g, :nb, :], iq)
        else:
            extra = qx_ref[g]
        qa.append(jnp.concatenate([q, extra], axis=1))

    def scores(g, j):
        off = pl.multiple_of(j * tq, tq)
        return lax.dot_general(kaug_sc[g, pl.ds(off, tq), :], qa[g], _NT,
                               preferred_element_type=F32)

    half = tq // 2
    halves = (slice(0, half), slice(half, tq))

    def produce(j, s_ref, mx_ref):
        for g in range(n_heads):
            s = scores(g, j)
            s_ref[g] = s
            mx_ref[g] = jnp.max(s, axis=0, keepdims=True)

    def consume(j, s_ref, mx_ref, diagonal=False):
        for g in range(n_heads):
            m = m_sc[g]
            m_new = jnp.maximum(m, mx_ref[g])
            alpha = jnp.exp2(m - m_new)
            m_sc[g] = m_new
            for qi, qs in enumerate(halves):
                pv = alpha[:, qs] * acc_sc[g, :, qs]
                for ki, ks in enumerate(halves):
                    if diagonal and ki > qi:
                        continue
                    p = jnp.exp2(s_ref[g, ks, qs] - m_new[:, qs])
                    pv = pv + jnp.dot(vt_sc[g, j, :, ks], p.astype(BF16),
                                      preferred_element_type=F32)
                acc_sc[g, :, qs] = pv

    for g in range(n_heads):
        m_sc[g] = jnp.full((1, tq), NEG, F32)
        acc_sc[g] = jnp.zeros((VT_ROWS, tq), F32)
    produce(0, sa_sc, mxa_sc)

    def body(jj, carry):
        j = 2 * jj
        produce(j + 1, sb_sc, mxb_sc)
        consume(j, sa_sc, mxa_sc)
        produce(j + 2, sa_sc, mxa_sc)
        consume(j + 1, sb_sc, mxb_sc)
        return carry

    lax.fori_loop(0, iq // 2, body, 0)

    def finish(s_ref, mx_ref):
        key = lax.broadcasted_iota(jnp.int32, (half, half), 0)
        qry = lax.broadcasted_iota(jnp.int32, (half, half), 1)
        lo, hi = halves
        for g in range(n_heads):
            d_lo = jnp.where(key <= qry, s_ref[g, lo, lo], NEG)
            d_hi = jnp.where(key <= qry, s_ref[g, hi, hi], NEG)
            s_ref[g, lo, lo] = d_lo
            s_ref[g, hi, hi] = d_hi
            mx_ref[g, :, lo] = jnp.max(d_lo, axis=0, keepdims=True)
            mx_ref[g, :, hi] = jnp.maximum(jnp.max(s_ref[g, lo, hi], axis=0, keepdims=True),
                                           jnp.max(d_hi, axis=0, keepdims=True))
        consume(iq, s_ref, mx_ref, diagonal=True)
        for g, hs in enumerate(heads):
            out = acc_sc[g, :HEAD_DIM, :] / acc_sc[g, HEAD_DIM:HEAD_DIM + 1, :]
            o_ref[:, hs] = out.T.astype(o_ref.dtype)

    @pl.when(iq % 2 == 0)
    def _():
        finish(sa_sc, mxa_sc)

    @pl.when(iq % 2 == 1)
    def _():
        produce(iq, sb_sc, mxb_sc)
        consume(iq - 1, sa_sc, mxa_sc)
        finish(sb_sc, mxb_sc)


def _attn(proj3, n_heads, head0, fox_extras=None):
    b, t, _ = proj3.shape
    moba = fox_extras is None
    tq = min(ATT_TILE, t)
    g = min(ATT_HEADS, n_heads)
    assert t % tq == 0 and tq % MOBA_BLOCK == 0 and t // MOBA_BLOCK <= LANES and n_heads % g == 0
    w = g * HEAD_DIM
    c0 = head0 // g
    in_specs = [
        pl.BlockSpec((None, tq, w), lambda bi, h, i: (bi, i, c0 + h)),
        pl.BlockSpec((None, t, w), lambda bi, h, i: (bi, 0, c0 + n_heads // g + h)),
        pl.BlockSpec((None, t, w), lambda bi, h, i: (bi, 0, c0 + 2 * (n_heads // g) + h)),
    ]
    args = [proj3, proj3, proj3]
    scratch = [
        pltpu.VMEM((g, t, 2 * HEAD_DIM), BF16),
        pltpu.VMEM((g, t // tq, VT_ROWS, tq), BF16),
        pltpu.VMEM((g, tq, tq), F32),
        pltpu.VMEM((g, tq, tq), F32),
        pltpu.VMEM((g, 1, tq), F32),
        pltpu.VMEM((g, 1, tq), F32),
        pltpu.VMEM((g, 1, tq), F32),
        pltpu.VMEM((g, VT_ROWS, tq), F32),
    ]
    if moba:
        scratch.append(pltpu.VMEM((g, LANES, HEAD_DIM), F32))
    else:
        in_specs += [
            pl.BlockSpec((None, g, tq, LANES), lambda bi, h, i: (bi, h, i, 0)),
            pl.BlockSpec((None, g, t, LANES), lambda bi, h, i: (bi, h, 0, 0)),
        ]
        args += list(fox_extras)
    return pl.pallas_call(
        functools.partial(_attn_kernel, moba=moba),
        grid=(b, n_heads // g, t // tq),
        in_specs=in_specs,
        out_specs=pl.BlockSpec((None, tq, w), lambda bi, h, i: (bi, i, h)),
        out_shape=jax.ShapeDtypeStruct((b, t, n_heads * HEAD_DIM), BF16),
        scratch_shapes=scratch,
        compiler_params=_cparams("arbitrary", "arbitrary", "arbitrary"),
        name="moba_attn" if moba else "fox_attn",
    )(*args)


def _mem_kv_kernel(mem_ref, g_ref, w_ref, k_ref, v_ref):
    xn = _rmsnorm(mem_ref[...], g_ref[...]).astype(BF16)
    kv = jnp.dot(xn, w_ref[...].astype(BF16), preferred_element_type=F32)
    half = kv.shape[1] // 2
    k_ref[...] = kv[:, :half].astype(k_ref.dtype)
    v_ref[...] = kv[:, half:].astype(v_ref.dtype)


def _mem_kv(mem, g, w_ckv):
    b, nm, d = mem.shape
    xd = w_ckv.shape[1] // 2
    out = jax.ShapeDtypeStruct((b, nm, xd), BF16)
    o_spec = pl.BlockSpec((None, nm, xd), lambda bi: (bi, 0, 0))
    return pl.pallas_call(
        _mem_kv_kernel,
        grid=(b,),
        in_specs=[
            pl.BlockSpec((None, nm, d), lambda bi: (bi, 0, 0)),
            pl.BlockSpec((1, d), lambda bi: (0, 0)),
            pl.BlockSpec((d, 2 * xd), lambda bi: (0, 0)),
        ],
        out_specs=[o_spec, o_spec],
        out_shape=[out, out],
        compiler_params=_cparams("arbitrary"),
        name="mem_kv",
    )(mem, g, w_ckv)


def _mid_kernel(x_ref, om_ref, of_ref, wom_ref, wof_ref, gx_ref, wcq_ref, kc_ref, vc_ref,
                wco_ref, gf_ref, h_ref, xn_ref):
    h1 = (x_ref[...]
          + jnp.dot(om_ref[...], wom_ref[...], preferred_element_type=F32)
          + jnp.dot(of_ref[...], wof_ref[...], preferred_element_type=F32))
    xn2 = _rmsnorm(h1, gx_ref[...]).astype(BF16)
    q = (jnp.dot(xn2, wcq_ref[...], preferred_element_type=F32) * SCALE).astype(BF16)
    outs = []
    for h in range(N_XATTN_HEADS):
        sl = slice(h * HEAD_DIM, (h + 1) * HEAD_DIM)
        s = lax.dot_general(q[:, sl], kc_ref[:, sl], _NT, preferred_element_type=F32)
        m, l, acc = _flash_init(s, vc_ref[:, sl])
        outs.append((acc / l).astype(BF16))
    oc = jnp.concatenate(outs, axis=1)
    h2 = h1 + jnp.dot(oc, wco_ref[...], preferred_element_type=F32)
    h_ref[...] = h2
    xn_ref[...] = _rmsnorm(h2, gf_ref[...]).astype(xn_ref.dtype)


def _mid(x2, o_moba, o_fox, w_om, w_of, gx, w_cq, kc, vc, w_co, gf, *, seq):
    n, d = x2.shape
    wm, wf = o_moba.shape[1], o_fox.shape[1]
    nm, xd = kc.shape[1], kc.shape[2]
    tm = min(512, seq)
    t_tiles = seq // tm
    const = lambda i: (0, 0)
    return pl.pallas_call(
        _mid_kernel,
        grid=(n // tm,),
        in_specs=[
            pl.BlockSpec((tm, d), lambda i: (i, 0)),
            pl.BlockSpec((tm, wm), lambda i: (i, 0)),
            pl.BlockSpec((tm, wf), lambda i: (i, 0)),
            pl.BlockSpec((wm, d), const),
            pl.BlockSpec((wf, d), const),
            pl.BlockSpec((1, d), const),
            pl.BlockSpec((d, xd), const),
            pl.BlockSpec((None, nm, xd), lambda i: (i // t_tiles, 0, 0)),
            pl.BlockSpec((None, nm, xd), lambda i: (i // t_tiles, 0, 0)),
            pl.BlockSpec((xd, d), const),
            pl.BlockSpec((1, d), const),
        ],
        out_specs=[
            pl.BlockSpec((tm, d), lambda i: (i, 0)),
            pl.BlockSpec((tm, d), lambda i: (i, 0)),
        ],
        out_shape=[
            jax.ShapeDtypeStruct((n, d), F32),
            jax.ShapeDtypeStruct((n, d), BF16),
        ],
        compiler_params=_cparams("arbitrary"),
        name="mid",
    )(x2, o_moba, o_fox, w_om, w_of, gx, w_cq, kc, vc, w_co, gf)


def _conv_ffn_kernel(xn_ref, h_ref, wg_ref, wu_ref, cwg_ref, cwu_ref, cbg_ref, cbu_ref,
                     wd_ref, fg_ref, o_ref, hs_sc, tail_sc, *, tiles_per_seq, n_res):
    i = pl.program_id(0)
    f = pl.program_id(1)
    nf = pl.num_programs(1)
    tm = xn_ref.shape[0]
    tf = wg_ref.shape[1]
    halo = SUBLANES
    first = (i % tiles_per_seq) == 0

    @pl.when((i == 0) & (f == 0))
    def _():
        tail_sc[...] = jnp.zeros_like(tail_sc)

    def up():
        xn = xn_ref[...]
        hs_sc[0:halo, :] = jnp.where(first, 0.0, tail_sc[f])
        hs_sc[halo:, :tf] = jnp.dot(xn, wg_ref[...].astype(BF16), preferred_element_type=F32)
        hs_sc[halo:, tf:] = jnp.dot(xn, wu_ref[...].astype(BF16), preferred_element_type=F32)
        tail_sc[f] = hs_sc[tm:tm + halo, :]

    def gate(cols):
        def conv(base, cw_ref, cb_ref):
            y = cb_ref[:, cols]
            for k in range(CONV_WIDTH):
                lo = halo - (CONV_WIDTH - 1) + k
                y = y + hs_sc[lo:lo + tm, base + cols.start:base + cols.stop] * cw_ref[k:k + 1, cols]
            return y
        gte = conv(0, cwg_ref, cbg_ref)
        return (gte * (1.0 / (1.0 + jnp.exp2(gte * (-LOG2E)))) * conv(tf, cwu_ref, cbu_ref)
                ).astype(BF16)

    res_rows = h_ref.shape[0]

    @pl.when(f == 0)
    def _():
        o_ref[...] = jnp.zeros_like(o_ref)

    @pl.when(f < n_res)
    def _():
        r0 = pl.multiple_of(f * res_rows, res_rows)
        o_ref[pl.ds(r0, res_rows), :] += h_ref[...]

    up()
    hk = tf // 2
    o_ref[...] += (
        jnp.dot(gate(slice(0, hk)), wd_ref[:hk, :].astype(BF16), preferred_element_type=F32)
        + jnp.dot(gate(slice(hk, tf)), wd_ref[hk:, :].astype(BF16), preferred_element_type=F32))

    @pl.when(f == nf - 1)
    def _():
        o_ref[...] = _rmsnorm(o_ref[...], fg_ref[...])


def _conv_ffn(xn3, h2, w_up, conv_w, conv_b, w_down, fg, *, seq):
    n, d = h2.shape
    ff = w_down.shape[0]
    tm = min(1024, seq)
    tf = 512
    assert ff % tf == 0
    nf = ff // tf
    n_res = 8 if nf >= 8 else 4
    assert nf >= n_res and tm % (n_res * SUBLANES) == 0
    return pl.pallas_call(
        functools.partial(_conv_ffn_kernel, tiles_per_seq=seq // tm, n_res=n_res),
        grid=(n // tm, nf),
        in_specs=[
            pl.BlockSpec((tm, d), lambda i, f: (i, 0)),
            pl.BlockSpec((tm // n_res, d), lambda i, f: (i * n_res + jnp.minimum(f, n_res - 1), 0)),
            pl.BlockSpec((d, tf), lambda i, f: (0, f)),
            pl.BlockSpec((d, tf), lambda i, f: (0, f + nf)),
            pl.BlockSpec((CONV_WIDTH, tf), lambda i, f: (0, f)),
            pl.BlockSpec((CONV_WIDTH, tf), lambda i, f: (0, f + nf)),
            pl.BlockSpec((1, tf), lambda i, f: (0, f)),
            pl.BlockSpec((1, tf), lambda i, f: (0, f + nf)),
            pl.BlockSpec((tf, d), lambda i, f: (f, 0)),
            pl.BlockSpec((1, d), lambda i, f: (0, 0)),
        ],
        out_specs=pl.BlockSpec((tm, d), lambda i, f: (i, 0)),
        out_shape=jax.ShapeDtypeStruct((n, d), F32),
        scratch_shapes=[
            pltpu.VMEM((tm + SUBLANES, 2 * tf), F32),
            pltpu.VMEM((nf, SUBLANES, 2 * tf), F32),
        ],
        compiler_params=_cparams("arbitrary", "arbitrary", vmem_limit_bytes=FFN_VMEM_LIMIT_BYTES),
        name="conv_ffn",
    )(xn3, h2, w_up, w_up, conv_w, conv_w, conv_b, conv_b, w_down, fg)


def _rope_tables(t):
    half = HEAD_DIM // 2
    inv = ROPE_THETA ** (-jnp.arange(half, dtype=F32) / half)
    ang = jnp.arange(t, dtype=F32)[:, None] * inv[None, :]
    cos, sin = jnp.cos(ang), jnp.sin(ang)
    return jnp.concatenate([cos, cos], axis=1), jnp.concatenate([-sin, sin], axis=1)


def kernel(x, mem, attn_norm_g, w_in, b_f, w_o, xattn_norm_g, mem_norm_g, w_cq, w_ckv, w_co,
           ffn_norm_g, w_up, conv_w, conv_b, w_down, final_norm_g):
    b, t, d = x.shape
    assert w_in.shape[0] == 1, "one layer: the final rmsnorm is fused into the ffn kernel"
    n_heads = d // HEAD_DIM
    n_moba = n_heads // 2
    n_fox = n_heads - n_moba
    assert n_moba == n_fox and n_fox <= LANES
    wgrp = n_moba * HEAD_DIM
    main_cols = 6 * wgrp
    cos, sin = _rope_tables(t)
    row = lambda v: v.reshape(1, -1)

    w_in_t = w_in[0].T
    w_z_t = jnp.pad(w_in_t[main_cols:], ((0, LANES - n_fox), (0, 0))).astype(BF16)
    bf = row(jnp.pad(b_f[0].astype(F32), (0, LANES - n_fox)))
    xn, qx, kx = _norm_gate(x, row(attn_norm_g[0]), w_z_t, bf, n_fox)
    proj = _in_proj(xn.reshape(b * t, d), w_in_t, cos, sin, seq=t, group_width=wgrp)
    proj3 = proj.reshape(b, t, main_cols)
    o_moba = _attn(proj3, n_moba, 0)
    o_fox = _attn(proj3, n_fox, 3 * n_moba, (qx, kx))
    kc, vc = _mem_kv(mem, row(mem_norm_g[0]), w_ckv[0])
    wo = w_o[0].astype(BF16)
    h2, xn3 = _mid(x.reshape(b * t, d), o_moba.reshape(b * t, wgrp), o_fox.reshape(b * t, wgrp),
                   wo[:wgrp], wo[wgrp:], row(xattn_norm_g[0]), w_cq[0].astype(BF16), kc, vc,
                   w_co[0].astype(BF16), row(ffn_norm_g[0]), seq=t)
    out = _conv_ffn(xn3, h2, w_up[0], conv_w[0], row(conv_b[0]),
                    w_down[0], row(final_norm_g), seq=t)
    return out.reshape(b, t, d)
```

```python
import functools
import math

import jax
import jax.numpy as jnp
import numpy as np
from jax import lax
from jax.experimental import pallas as pl
from jax.experimental.pallas import tpu as pltpu

HEAD_DIM = 128
MOBA_BLOCK = 256
MOBA_TOPK = 3
ROPE_THETA = 10000.0
N_XATTN_HEADS = 4
CONV_WIDTH = 3
RMS_EPS = 1e-6
NEG = -1e30
SCALE = 1.0 / math.sqrt(HEAD_DIM)
LOG2E = math.log2(math.e)
QSCALE = SCALE * LOG2E

LANES = 128
SUBLANES = 8
VMEM_LIMIT_BYTES = 56 * 1024 * 1024
FFN_VMEM_LIMIT_BYTES = 62 * 1024 * 1024

F32 = jnp.float32
BF16 = jnp.bfloat16

_NT = (((1,), (1,)), ((), ()))


def _cparams(*sem, vmem_limit_bytes=VMEM_LIMIT_BYTES):
    return pltpu.CompilerParams(dimension_semantics=sem, vmem_limit_bytes=vmem_limit_bytes)


def _rmsnorm(x, g):
    ms = jnp.mean(x * x, axis=-1, keepdims=True)
    return x * lax.rsqrt(ms + RMS_EPS) * g


def _split3(x):
    hi = x.astype(BF16)
    r = x - hi.astype(F32)
    mid = r.astype(BF16)
    lo = (r - mid.astype(F32)).astype(BF16)
    return hi, mid, lo


def _in_proj_kernel(xn_ref, w_ref, cos_ref, sin_ref, o_ref, w_sc, *, group_of_tile):
    j = pl.program_id(0)

    @pl.when(pl.program_id(1) == 0)
    def _():
        w_sc[...] = w_ref[...].astype(BF16)

    grp = group_of_tile(j)
    rot = (grp <= 1).astype(F32)
    scale = jnp.where((grp == 0) | (grp == 3), QSCALE, 1.0).astype(F32)
    a = (cos_ref[...] * rot + (1.0 - rot)) * scale
    b = sin_ref[...] * (rot * scale)
    tm, tn = o_ref.shape
    chunk = min(256, tm)
    for r0 in range(0, tm, chunk):
        rows = slice(r0, r0 + chunk)
        y = lax.dot_general(xn_ref[rows, :], w_sc[...], _NT, preferred_element_type=F32)
        for c0 in range(0, tn, HEAD_DIM):
            yh = y[:, c0:c0 + HEAD_DIM]
            out = yh * a[rows] + pltpu.roll(yh, HEAD_DIM // 2, 1) * b[rows]
            o_ref[rows, c0:c0 + HEAD_DIM] = out.astype(o_ref.dtype)


def _in_proj(xn, w_in_t, cos, sin, *, seq, group_width):
    n, d = xn.shape
    cols = 6 * group_width
    tm = min(2048, seq)
    tn = min(1024, group_width)
    tiles_per_group = group_width // tn
    t_tiles = seq // tm
    kern = functools.partial(_in_proj_kernel, group_of_tile=lambda j: j // tiles_per_group)
    return pl.pallas_call(
        kern,
        grid=(cols // tn, n // tm),
        in_specs=[
            pl.BlockSpec((tm, d), lambda j, i: (i, 0)),
            pl.BlockSpec((tn, d), lambda j, i: (j, 0)),
            pl.BlockSpec((tm, HEAD_DIM), lambda j, i: (i % t_tiles, 0)),
            pl.BlockSpec((tm, HEAD_DIM), lambda j, i: (i % t_tiles, 0)),
        ],
        out_specs=pl.BlockSpec((tm, tn), lambda j, i: (i, j)),
        out_shape=jax.ShapeDtypeStruct((n, cols), BF16),
        scratch_shapes=[pltpu.VMEM((tn, d), BF16)],
        compiler_params=_cparams("arbitrary", "arbitrary"),
        name="in_proj",
    )(xn, w_in_t, cos, sin)


def _norm_gate_kernel(x_ref, g_ref, wz_ref, bf_ref, route_ref, xn_ref, qx_ref, kx_ref, carry_sc,
                      *, n_heads):
    t = pl.program_id(1)

    @pl.when(t == 0)
    def _():
        carry_sc[...] = jnp.zeros_like(carry_sc)

    xn = _rmsnorm(x_ref[...], g_ref[...]).astype(BF16)
    xn_ref[...] = xn
    z = lax.dot_general(xn, wz_ref[...], _NT, preferred_element_type=F32) + bf_ref[...]
    lf = jnp.minimum(z, 0.0) - jnp.log(1.0 + jnp.exp(-jnp.abs(z)))

    tc = lf.shape[0]
    row = lax.broadcasted_iota(jnp.int32, (tc, tc), 0)
    col = lax.broadcasted_iota(jnp.int32, (tc, tc), 1)
    tri = (col <= row).astype(BF16)
    c = carry_sc[0:1, :]
    for part in _split3(lf):
        c = c + jnp.dot(tri, part, preferred_element_type=F32)
    carry_sc[0:1, :] = c[tc - 1:tc, :]

    parts = jnp.concatenate(_split3(c * LOG2E), axis=1)
    routed = jnp.dot(parts, route_ref[...], preferred_element_type=F32)
    lane = lax.broadcasted_iota(jnp.int32, (tc, LANES), 1)
    q_ones = ((lane >= 3) & (lane < 6)).astype(F32)
    k_ones = (lane < 3).astype(F32)
    for h in range(n_heads):
        qx_ref[h] = (routed[:, (2 * h) * LANES:(2 * h + 1) * LANES] + q_ones).astype(BF16)
        kx_ref[h] = (routed[:, (2 * h + 1) * LANES:(2 * h + 2) * LANES] + k_ones).astype(BF16)


def _gate_routing(n_heads):
    r = np.zeros((3 * LANES, n_heads * 2 * LANES), np.float32)
    for h in range(n_heads):
        for p in range(3):
            r[p * LANES + h, (2 * h) * LANES + p] = 1.0
            r[p * LANES + h, (2 * h + 1) * LANES + 3 + p] = -1.0
    return jnp.asarray(r, BF16)


def _norm_gate(x3, g, w_z, b_f, n_heads):
    b, t, d = x3.shape
    tc = min(512, t)
    ext = jax.ShapeDtypeStruct((b, n_heads, t, LANES), BF16)
    ext_spec = pl.BlockSpec((None, n_heads, tc, LANES), lambda bi, ti: (bi, 0, ti, 0))
    const = lambda bi, ti: (0, 0)
    return pl.pallas_call(
        functools.partial(_norm_gate_kernel, n_heads=n_heads),
        grid=(b, t // tc),
        in_specs=[
            pl.BlockSpec((None, tc, d), lambda bi, ti: (bi, ti, 0)),
            pl.BlockSpec((1, d), const),
            pl.BlockSpec((LANES, d), const),
            pl.BlockSpec((1, LANES), const),
            pl.BlockSpec((3 * LANES, n_heads * 2 * LANES), const),
        ],
        out_specs=[pl.BlockSpec((None, tc, d), lambda bi, ti: (bi, ti, 0)), ext_spec, ext_spec],
        out_shape=[jax.ShapeDtypeStruct((b, t, d), BF16), ext, ext],
        scratch_shapes=[pltpu.VMEM((SUBLANES, LANES), F32)],
        compiler_params=_cparams("arbitrary", "arbitrary"),
        name="norm_gate",
    )(x3, g, w_z, b_f, _gate_routing(n_heads))


ATT_TILE = 512
ATT_HEADS = 4
VT_ROWS = HEAD_DIM + 16


def _flash_init(s, v):
    m = jnp.max(s, axis=1, keepdims=True)
    p = jnp.exp(s - m)
    l = jnp.sum(p, axis=1, keepdims=True)
    acc = jnp.dot(p.astype(BF16), v, preferred_element_type=F32)
    return m, l, acc


def _moba_bias(q, kmean, iq):
    tq = q.shape[0]
    nb = kmean.shape[0]
    km_hi = kmean.astype(BF16)
    km_lo = (kmean - km_hi.astype(F32)).astype(BF16)
    gate = (lax.dot_general(km_hi, q, _NT, preferred_element_type=F32)
            + lax.dot_general(km_lo, q, _NT, preferred_element_type=F32))
    blk_i = lax.broadcasted_iota(jnp.int32, gate.shape, 0)
    qry_i = lax.broadcasted_iota(jnp.int32, gate.shape, 1)
    own = iq * (tq // MOBA_BLOCK) + qry_i // MOBA_BLOCK
    blk_f = blk_i.astype(F32)
    past = blk_i < own
    g = jnp.where(past, gate, -jnp.inf)
    sel = jnp.zeros(gate.shape, jnp.bool_)
    for _ in range(MOBA_TOPK):
        mx = jnp.max(g, axis=0, keepdims=True)
        idx = jnp.min(jnp.where(g == mx, blk_f, float(LANES)), axis=0, keepdims=True)
        pick = blk_f == idx
        sel = sel | pick
        g = jnp.where(pick, -jnp.inf, g)
    bias_t = jnp.where((sel & past) | (blk_i == own), 0.0, NEG)
    bias_t = jnp.concatenate([bias_t, jnp.zeros((LANES - nb, tq), F32)], axis=0)
    return bias_t.T.astype(BF16)


def _attn_kernel(*refs, moba):
    if moba:
        q_ref, k_ref, v_ref, o_ref = refs[:4]
        kmean_sc = refs[-1]
        refs = refs[:-1]
    else:
        q_ref, k_ref, v_ref, qx_ref, kx_ref, o_ref = refs[:6]
    kaug_sc, vt_sc, sa_sc, sb_sc, mxa_sc, mxb_sc, m_sc, acc_sc = refs[-8:]
    iq = pl.program_id(2)
    tq = q_ref.shape[0]
    t = k_ref.shape[0]
    n_heads = q_ref.shape[1] // HEAD_DIM
    heads = [slice(g * HEAD_DIM, (g + 1) * HEAD_DIM) for g in range(n_heads)]

    @pl.when(iq == 0)
    def _():
        for g, hs in enumerate(heads):
            kaug_sc[g, :, :HEAD_DIM] = k_ref[:, hs]
            if moba:
                row = lax.broadcasted_iota(jnp.int32, (t, LANES), 0)
                lane = lax.broadcasted_iota(jnp.int32, (t, LANES), 1)
                kaug_sc[g, :, HEAD_DIM:] = ((row // MOBA_BLOCK) == lane).astype(BF16)
                kmean_sc[g] = jnp.zeros((LANES, HEAD_DIM), F32)
                for n in range(t // MOBA_BLOCK):
                    kb = k_ref[n * MOBA_BLOCK:(n + 1) * MOBA_BLOCK, hs].astype(F32)
                    kmean_sc[g, n:n + 1, :] = jnp.mean(kb, axis=0, keepdims=True)
            else:
                kaug_sc[g, :, HEAD_DIM:] = kx_ref[g]
            pad_row = lax.broadcasted_iota(jnp.int32, (VT_ROWS - HEAD_DIM, tq), 0)
            for j in range(t // tq):
                vt_sc[g, j, :HEAD_DIM, :] = (
                    v_ref[j * tq:(j + 1) * tq, hs].astype(F32).T.astype(BF16))
                vt_sc[g, j, HEAD_DIM:, :] = (pad_row == 0).astype(BF16)

    qa = []
    for g, hs in enumerate(heads):
        q = q_ref[:, hs]
        if moba:
            nb = -(-(t // MOBA_BLOCK) // 16) * 16
            extra = _moba_bias(q, kmean_sc[g, :nb, :], iq)
        else:
            extra = qx_ref[g]
        qa.append(jnp.concatenate([q, extra], axis=1))

    def scores(g, j):
        off = pl.multiple_of(j * tq, tq)
        return lax.dot_general(kaug_sc[g, pl.ds(off, tq), :], qa[g], _NT,
                               preferred_element_type=F32)

    half = tq // 2
    halves = (slice(0, half), slice(half, tq))

    def produce(j, s_ref, mx_ref):
        for g in range(n_heads):
            s = scores(g, j)
            s_ref[g] = s
            mx_ref[g] = jnp.max(s, axis=0, keepdims=True)

    def consume(j, s_ref, mx_ref, diagonal=False):
        for g in range(n_heads):
            m = m_sc[g]
            m_new = jnp.maximum(m, mx_ref[g])
            alpha = jnp.exp2(m - m_new)
            m_sc[g] = m_new
            for qi, qs in enumerate(halves):
                pv = alpha[:, qs] * acc_sc[g, :, qs]
                for ki, ks in enumerate(halves):
                    if diagonal and ki > qi:
                        continue
                    p = jnp.exp2(s_ref[g, ks, qs] - m_new[:, qs])
                    pv = pv + jnp.dot(vt_sc[g, j, :, ks], p.astype(BF16),
                                      preferred_element_type=F32)
                acc_sc[g, :, qs] = pv

    for g in range(n_heads):
        m_sc[g] = jnp.full((1, tq), NEG, F32)
        acc_sc[g] = jnp.zeros((VT_ROWS, tq), F32)
    produce(0, sa_sc, mxa_sc)

    def body(jj, carry):
        j = 2 * jj
        produce(j + 1, sb_sc, mxb_sc)
        consume(j, sa_sc, mxa_sc)
        produce(j + 2, sa_sc, mxa_sc)
        consume(j + 1, sb_sc, mxb_sc)
        return carry

    lax.fori_loop(0, iq // 2, body, 0)

    def finish(s_ref, mx_ref):
        key = lax.broadcasted_iota(jnp.int32, (half, half), 0)
        qry = lax.broadcasted_iota(jnp.int32, (half, half), 1)
        lo, hi = halves
        for g in range(n_heads):
            d_lo = jnp.where(key <= qry, s_ref[g, lo, lo], NEG)
            d_hi = jnp.where(key <= qry, s_ref[g, hi, hi], NEG)
            s_ref[g, lo, lo] = d_lo
            s_ref[g, hi, hi] = d_hi
            mx_ref[g, :, lo] = jnp.max(d_lo, axis=0, keepdims=True)
            mx_ref[g, :, hi] = jnp.maximum(jnp.max(s_ref[g, lo, hi], axis=0, keepdims=True),
                                           jnp.max(d_hi, axis=0, keepdims=True))
        consume(iq, s_ref, mx_ref, diagonal=True)
        for g, hs in enumerate(heads):
            out = acc_sc[g, :HEAD_DIM, :] / acc_sc[g, HEAD_DIM:HEAD_DIM + 1, :]
            o_ref[:, hs] = out.T.astype(o_ref.dtype)

    @pl.when(iq % 2 == 0)
    def _():
        finish(sa_sc, mxa_sc)

    @pl.when(iq % 2 == 1)
    def _():
        produce(iq, sb_sc, mxb_sc)
        consume(iq - 1, sa_sc, mxa_sc)
        finish(sb_sc, mxb_sc)


def _attn(proj3, n_heads, head0, fox_extras=None):
    b, t, _ = proj3.shape
    moba = fox_extras is None
    tq = min(ATT_TILE, t)
    g = min(ATT_HEADS, n_heads)
    assert t % tq == 0 and tq % MOBA_BLOCK == 0 and t // MOBA_BLOCK <= LANES and n_heads % g == 0
    w = g * HEAD_DIM
    c0 = head0 // g
    in_specs = [
        pl.BlockSpec((None, tq, w), lambda bi, h, i: (bi, i, c0 + h)),
        pl.BlockSpec((None, t, w), lambda bi, h, i: (bi, 0, c0 + n_heads // g + h)),
        pl.BlockSpec((None, t, w), lambda bi, h, i: (bi, 0, c0 + 2 * (n_heads // g) + h)),
    ]
    args = [proj3, proj3, proj3]
    scratch = [
        pltpu.VMEM((g, t, 2 * HEAD_DIM), BF16),
        pltpu.VMEM((g, t // tq, VT_ROWS, tq), BF16),
        pltpu.VMEM((g, tq, tq), F32),
        pltpu.VMEM((g, tq, tq), F32),
        pltpu.VMEM((g, 1, tq), F32),
        pltpu.VMEM((g, 1, tq), F32),
        pltpu.VMEM((g, 1, tq), F32),
        pltpu.VMEM((g, VT_ROWS, tq), F32),
    ]
    if moba:
        scratch.append(pltpu.VMEM((g, LANES, HEAD_DIM), F32))
    else:
        in_specs += [
            pl.BlockSpec((None, g, tq, LANES), lambda bi, h, i: (bi, h, i, 0)),
            pl.BlockSpec((None, g, t, LANES), lambda bi, h, i: (bi, h, 0, 0)),
        ]
        args += list(fox_extras)
    return pl.pallas_call(
        functools.partial(_attn_kernel, moba=moba),
        grid=(b, n_heads // g, t // tq),
        in_specs=in_specs,
        out_specs=pl.BlockSpec((None, tq, w), lambda bi, h, i: (bi, i, h)),
        out_shape=jax.ShapeDtypeStruct((b, t, n_heads * HEAD_DIM), BF16),
        scratch_shapes=scratch,
        compiler_params=_cparams("arbitrary", "arbitrary", "arbitrary"),
        name="moba_attn" if moba else "fox_attn",
    )(*args)


def _mem_kv_kernel(mem_ref, g_ref, w_ref, k_ref, v_ref):
    xn = _rmsnorm(mem_ref[...], g_ref[...]).astype(BF16)
    kv = jnp.dot(xn, w_ref[...].astype(BF16), preferred_element_type=F32)
    half = kv.shape[1] // 2
    k_ref[...] = kv[:, :half].astype(k_ref.dtype)
    v_ref[...] = kv[:, half:].astype(v_ref.dtype)


def _mem_kv(mem, g, w_ckv):
    b, nm, d = mem.shape
    xd = w_ckv.shape[1] // 2
    out = jax.ShapeDtypeStruct((b, nm, xd), BF16)
    o_spec = pl.BlockSpec((None, nm, xd), lambda bi: (bi, 0, 0))
    return pl.pallas_call(
        _mem_kv_kernel,
        grid=(b,),
        in_specs=[
            pl.BlockSpec((None, nm, d), lambda bi: (bi, 0, 0)),
            pl.BlockSpec((1, d), lambda bi: (0, 0)),
            pl.BlockSpec((d, 2 * xd), lambda bi: (0, 0)),
        ],
        out_specs=[o_spec, o_spec],
        out_shape=[out, out],
        compiler_params=_cparams("arbitrary"),
        name="mem_kv",
    )(mem, g, w_ckv)


def _mid_kernel(x_ref, om_ref, of_ref, wom_ref, wof_ref, gx_ref, wcq_ref, kc_ref, vc_ref,
                wco_ref, gf_ref, h_ref, xn_ref):
    h1 = (x_ref[...]
          + jnp.dot(om_ref[...], wom_ref[...], preferred_element_type=F32)
          + jnp.dot(of_ref[...], wof_ref[...], preferred_element_type=F32))
    xn2 = _rmsnorm(h1, gx_ref[...]).astype(BF16)
    q = (jnp.dot(xn2, wcq_ref[...], preferred_element_type=F32) * SCALE).astype(BF16)
    outs = []
    for h in range(N_XATTN_HEADS):
        sl = slice(h * HEAD_DIM, (h + 1) * HEAD_DIM)
        s = lax.dot_general(q[:, sl], kc_ref[:, sl], _NT, preferred_element_type=F32)
        m, l, acc = _flash_init(s, vc_ref[:, sl])
        outs.append((acc / l).astype(BF16))
    oc = jnp.concatenate(outs, axis=1)
    h2 = h1 + jnp.dot(oc, wco_ref[...], preferred_element_type=F32)
    h_ref[...] = h2
    xn_ref[...] = _rmsnorm(h2, gf_ref[...]).astype(xn_ref.dtype)


def _mid(x2, o_moba, o_fox, w_o, gx, w_cq, kc, vc, w_co, gf, *, seq):
    n, d = x2.shape
    wm, wf = o_moba.shape[1], o_fox.shape[1]
    assert wm == wf and w_o.shape[0] == wm + wf
    nm, xd = kc.shape[1], kc.shape[2]
    tm = min(512, seq)
    t_tiles = seq // tm
    const = lambda i: (0, 0)
    return pl.pallas_call(
        _mid_kernel,
        grid=(n // tm,),
        in_specs=[
            pl.BlockSpec((tm, d), lambda i: (i, 0)),
            pl.BlockSpec((tm, wm), lambda i: (i, 0)),
            pl.BlockSpec((tm, wf), lambda i: (i, 0)),
            pl.BlockSpec((wm, d), const),
            pl.BlockSpec((wf, d), lambda i: (1, 0)),
            pl.BlockSpec((1, d), const),
            pl.BlockSpec((d, xd), const),
            pl.BlockSpec((None, nm, xd), lambda i: (i // t_tiles, 0, 0)),
            pl.BlockSpec((None, nm, xd), lambda i: (i // t_tiles, 0, 0)),
            pl.BlockSpec((xd, d), const),
            pl.BlockSpec((1, d), const),
        ],
        out_specs=[
            pl.BlockSpec((tm, d), lambda i: (i, 0)),
            pl.BlockSpec((tm, d), lambda i: (i, 0)),
        ],
        out_shape=[
            jax.ShapeDtypeStruct((n, d), F32),
            jax.ShapeDtypeStruct((n, d), BF16),
        ],
        compiler_params=_cparams("arbitrary"),
        name="mid",
    )(x2, o_moba, o_fox, w_o, w_o, gx, w_cq, kc, vc, w_co, gf)


def _conv_ffn_kernel(xn_ref, h_ref, wg_ref, wu_ref, cwg_ref, cwu_ref, cbg_ref, cbu_ref,
                     wd_ref, fg_ref, o_ref, hs_sc, tail_sc, *, tiles_per_seq, n_res):
    i = pl.program_id(0)
    f = pl.program_id(1)
    nf = pl.num_programs(1)
    tm = xn_ref.shape[0]
    tf = wg_ref.shape[1]
    halo = SUBLANES
    first = (i % tiles_per_seq) == 0

    @pl.when((i == 0) & (f == 0))
    def _():
        tail_sc[...] = jnp.zeros_like(tail_sc)

    def up():
        xn = xn_ref[...]
        hs_sc[0:halo, :] = jnp.where(first, 0.0, tail_sc[f])
        hs_sc[halo:, :tf] = jnp.dot(xn, wg_ref[...].astype(BF16), preferred_element_type=F32)
        hs_sc[halo:, tf:] = jnp.dot(xn, wu_ref[...].astype(BF16), preferred_element_type=F32)
        tail_sc[f] = hs_sc[tm:tm + halo, :]

    def gate(cols):
        def conv(base, cw_ref, cb_ref):
            y = cb_ref[:, cols]
            for k in range(CONV_WIDTH):
                lo = halo - (CONV_WIDTH - 1) + k
                y = y + hs_sc[lo:lo + tm, base + cols.start:base + cols.stop] * cw_ref[k:k + 1, cols]
            return y
        gte = conv(0, cwg_ref, cbg_ref)
        return (gte * (1.0 / (1.0 + jnp.exp2(gte * (-LOG2E)))) * conv(tf, cwu_ref, cbu_ref)
                ).astype(BF16)

    res_rows = h_ref.shape[0]

    @pl.when(f == 0)
    def _():
        o_ref[...] = jnp.zeros_like(o_ref)

    @pl.when(f < n_res)
    def _():
        r0 = pl.multiple_of(f * res_rows, res_rows)
        o_ref[pl.ds(r0, res_rows), :] += h_ref[...]

    up()
    hk = tf // 2
    o_ref[...] += (
        jnp.dot(gate(slice(0, hk)), wd_ref[:hk, :].astype(BF16), preferred_element_type=F32)
        + jnp.dot(gate(slice(hk, tf)), wd_ref[hk:, :].astype(BF16), preferred_element_type=F32))

    @pl.when(f == nf - 1)
    def _():
        o_ref[...] = _rmsnorm(o_ref[...], fg_ref[...])


def _conv_ffn(xn3, h2, w_up, conv_w, conv_b, w_down, fg, *, seq):
    n, d = h2.shape
    ff = w_down.shape[0]
    tm = min(1024, seq)
    tf = 512
    assert ff % tf == 0
    nf = ff // tf
    n_res = 8 if nf >= 8 else 4
    assert nf >= n_res and tm % (n_res * SUBLANES) == 0
    return pl.pallas_call(
        functools.partial(_conv_ffn_kernel, tiles_per_seq=seq // tm, n_res=n_res),
        grid=(n // tm, nf),
        in_specs=[
            pl.BlockSpec((tm, d), lambda i, f: (i, 0)),
            pl.BlockSpec((tm // n_res, d), lambda i, f: (i * n_res + jnp.minimum(f, n_res - 1), 0)),
            pl.BlockSpec((d, tf), lambda i, f: (0, f)),
            pl.BlockSpec((d, tf), lambda i, f: (0, f + nf)),
            pl.BlockSpec((CONV_WIDTH, tf), lambda i, f: (0, f)),
            pl.BlockSpec((CONV_WIDTH, tf), lambda i, f: (0, f + nf)),
            pl.BlockSpec((1, tf), lambda i, f: (0, f)),
            pl.BlockSpec((1, tf), lambda i, f: (0, f + nf)),
            pl.BlockSpec((tf, d), lambda i, f: (f, 0)),
            pl.BlockSpec((1, d), lambda i, f: (0, 0)),
        ],
        out_specs=pl.BlockSpec((tm, d), lambda i, f: (i, 0)),
        out_shape=jax.ShapeDtypeStruct((n, d), F32),
        scratch_shapes=[
            pltpu.VMEM((tm + SUBLANES, 2 * tf), F32),
            pltpu.VMEM((nf, SUBLANES, 2 * tf), F32),
        ],
        compiler_params=_cparams("arbitrary", "arbitrary", vmem_limit_bytes=FFN_VMEM_LIMIT_BYTES),
        name="conv_ffn",
    )(xn3, h2, w_up, w_up, conv_w, conv_w, conv_b, conv_b, w_down, fg)


def _rope_tables(t):
    half = HEAD_DIM // 2
    inv = (ROPE_THETA ** (-np.arange(half, dtype=np.float32) / half)).astype(np.float32)
    ang = (np.arange(t, dtype=np.float32)[:, None] * inv[None, :]).astype(np.float32)
    cos, sin = np.cos(ang), np.sin(ang)
    return (jnp.asarray(np.concatenate([cos, cos], axis=1)),
            jnp.asarray(np.concatenate([-sin, sin], axis=1)))


def kernel(x, mem, attn_norm_g, w_in, b_f, w_o, xattn_norm_g, mem_norm_g, w_cq, w_ckv, w_co,
           ffn_norm_g, w_up, conv_w, conv_b, w_down, final_norm_g):
    b, t, d = x.shape
    assert w_in.shape[0] == 1, "one layer: the final rmsnorm is fused into the ffn kernel"
    n_heads = d // HEAD_DIM
    n_moba = n_heads // 2
    n_fox = n_heads - n_moba
    assert n_moba == n_fox and n_fox <= LANES
    wgrp = n_moba * HEAD_DIM
    main_cols = 6 * wgrp
    cos, sin = _rope_tables(t)
    row = lambda v: v.reshape(1, -1)

    w_in_t = w_in[0].T
    w_z_t = jnp.pad(w_in_t[main_cols:], ((0, LANES - n_fox), (0, 0))).astype(BF16)
    bf = row(jnp.pad(b_f[0].astype(F32), (0, LANES - n_fox)))
    xn, qx, kx = _norm_gate(x, row(attn_norm_g[0]), w_z_t, bf, n_fox)
    proj = _in_proj(xn.reshape(b * t, d), w_in_t, cos, sin, seq=t, group_width=wgrp)
    proj3 = proj.reshape(b, t, main_cols)
    o_moba = _attn(proj3, n_moba, 0)
    o_fox = _attn(proj3, n_fox, 3 * n_moba, (qx, kx))
    kc, vc = _mem_kv(mem, row(mem_norm_g[0]), w_ckv[0])
    h2, xn3 = _mid(x.reshape(b * t, d), o_moba.reshape(b * t, wgrp), o_fox.reshape(b * t, wgrp),
                   w_o[0].astype(BF16), row(xattn_norm_g[0]), w_cq[0].astype(BF16), kc, vc,
                   w_co[0].astype(BF16), row(ffn_norm_g[0]), seq=t)
    out = _conv_ffn(xn3, h2, w_up[0], conv_w[0], row(conv_b[0]),
                    w_down[0], row(final_norm_g), seq=t)
    return out.reshape(b, t, d)
```

```python
import functools
import math

import jax
import jax.numpy as jnp
import numpy as np
from jax import lax
from jax.experimental import pallas as pl
from jax.experimental.pallas import tpu as pltpu

HEAD_DIM = 128
MOBA_BLOCK = 256
MOBA_TOPK = 3
ROPE_THETA = 10000.0
N_XATTN_HEADS = 4
CONV_WIDTH = 3
RMS_EPS = 1e-6
NEG = -1e30
SCALE = 1.0 / math.sqrt(HEAD_DIM)
LOG2E = math.log2(math.e)
QSCALE = SCALE * LOG2E

LANES = 128
SUBLANES = 8
VMEM_LIMIT_BYTES = 56 * 1024 * 1024
FFN_VMEM_LIMIT_BYTES = 62 * 1024 * 1024

F32 = jnp.float32
BF16 = jnp.bfloat16

_NT = (((1,), (1,)), ((), ()))


def _cparams(*sem, vmem_limit_bytes=VMEM_LIMIT_BYTES):
    return pltpu.CompilerParams(dimension_semantics=sem, vmem_limit_bytes=vmem_limit_bytes)


def _rmsnorm(x, g):
    ms = jnp.mean(x * x, axis=-1, keepdims=True)
    return x * lax.rsqrt(ms + RMS_EPS) * g


def _split3(x):
    hi = x.astype(BF16)
    r = x - hi.astype(F32)
    mid = r.astype(BF16)
    lo = (r - mid.astype(F32)).astype(BF16)
    return hi, mid, lo


def _in_proj_kernel(xn_ref, w_ref, cos_ref, sin_ref, o_ref, w_sc, *, group_of_tile):
    j = pl.program_id(0)

    @pl.when(pl.program_id(1) == 0)
    def _():
        w_sc[...] = w_ref[...].astype(BF16)

    grp = group_of_tile(j)
    rot = (grp <= 1).astype(F32)
    scale = jnp.where((grp == 0) | (grp == 3), QSCALE, 1.0).astype(F32)
    a = (cos_ref[...] * rot + (1.0 - rot)) * scale
    b = sin_ref[...] * (rot * scale)
    tm, tn = o_ref.shape
    chunk = min(256, tm)
    for r0 in range(0, tm, chunk):
        rows = slice(r0, r0 + chunk)
        y = lax.dot_general(xn_ref[rows, :], w_sc[...], _NT, preferred_element_type=F32)
        for c0 in range(0, tn, HEAD_DIM):
            yh = y[:, c0:c0 + HEAD_DIM]
            out = yh * a[rows] + pltpu.roll(yh, HEAD_DIM // 2, 1) * b[rows]
            o_ref[rows, c0:c0 + HEAD_DIM] = out.astype(o_ref.dtype)


def _in_proj(xn, w_in_t, cos, sin, *, seq, group_width):
    n, d = xn.shape
    cols = 6 * group_width
    tm = min(2048, seq)
    tn = min(1024, group_width)
    tiles_per_group = group_width // tn
    t_tiles = seq // tm
    kern = functools.partial(_in_proj_kernel, group_of_tile=lambda j: j // tiles_per_group)
    return pl.pallas_call(
        kern,
        grid=(cols // tn, n // tm),
        in_specs=[
            pl.BlockSpec((tm, d), lambda j, i: (i, 0)),
            pl.BlockSpec((tn, d), lambda j, i: (j, 0)),
            pl.BlockSpec((tm, HEAD_DIM), lambda j, i: (i % t_tiles, 0)),
            pl.BlockSpec((tm, HEAD_DIM), lambda j, i: (i % t_tiles, 0)),
        ],
        out_specs=pl.BlockSpec((tm, tn), lambda j, i: (i, j)),
        out_shape=jax.ShapeDtypeStruct((n, cols), BF16),
        scratch_shapes=[pltpu.VMEM((tn, d), BF16)],
        compiler_params=_cparams("arbitrary", "arbitrary"),
        name="in_proj",
    )(xn, w_in_t, cos, sin)


def _norm_gate_kernel(x_ref, g_ref, wz_ref, bf_ref, route_ref, xn_ref, qx_ref, kx_ref, carry_sc,
                      *, n_heads):
    t = pl.program_id(1)

    @pl.when(t == 0)
    def _():
        carry_sc[...] = jnp.zeros_like(carry_sc)

    xn = _rmsnorm(x_ref[...], g_ref[...]).astype(BF16)
    xn_ref[...] = xn
    z = lax.dot_general(xn, wz_ref[...], _NT, preferred_element_type=F32) + bf_ref[...]
    lf = jnp.minimum(z, 0.0) - jnp.log(1.0 + jnp.exp(-jnp.abs(z)))

    tc = lf.shape[0]
    row = lax.broadcasted_iota(jnp.int32, (tc, tc), 0)
    col = lax.broadcasted_iota(jnp.int32, (tc, tc), 1)
    tri = (col <= row).astype(BF16)
    c = carry_sc[0:1, :]
    for part in _split3(lf):
        c = c + jnp.dot(tri, part, preferred_element_type=F32)
    carry_sc[0:1, :] = c[tc - 1:tc, :]

    parts = jnp.concatenate(_split3(c * LOG2E), axis=1)
    routed = jnp.dot(parts, route_ref[...], preferred_element_type=F32)
    lane = lax.broadcasted_iota(jnp.int32, (tc, LANES), 1)
    q_ones = ((lane >= 3) & (lane < 6)).astype(F32)
    k_ones = (lane < 3).astype(F32)
    for h in range(n_heads):
        qx_ref[h] = (routed[:, (2 * h) * LANES:(2 * h + 1) * LANES] + q_ones).astype(BF16)
        kx_ref[h] = (routed[:, (2 * h + 1) * LANES:(2 * h + 2) * LANES] + k_ones).astype(BF16)


def _gate_routing(n_heads):
    r = np.zeros((3 * LANES, n_heads * 2 * LANES), np.float32)
    for h in range(n_heads):
        for p in range(3):
            r[p * LANES + h, (2 * h) * LANES + p] = 1.0
            r[p * LANES + h, (2 * h + 1) * LANES + 3 + p] = -1.0
    return jnp.asarray(r, BF16)


def _norm_gate(x3, g, w_z, b_f, n_heads):
    b, t, d = x3.shape
    tc = min(512, t)
    ext = jax.ShapeDtypeStruct((b, n_heads, t, LANES), BF16)
    ext_spec = pl.BlockSpec((None, n_heads, tc, LANES), lambda bi, ti: (bi, 0, ti, 0))
    const = lambda bi, ti: (0, 0)
    return pl.pallas_call(
        functools.partial(_norm_gate_kernel, n_heads=n_heads),
        grid=(b, t // tc),
        in_specs=[
            pl.BlockSpec((None, tc, d), lambda bi, ti: (bi, ti, 0)),
            pl.BlockSpec((1, d), const),
            pl.BlockSpec((LANES, d), const),
            pl.BlockSpec((1, LANES), const),
            pl.BlockSpec((3 * LANES, n_heads * 2 * LANES), const),
        ],
        out_specs=[pl.BlockSpec((None, tc, d), lambda bi, ti: (bi, ti, 0)), ext_spec, ext_spec],
        out_shape=[jax.ShapeDtypeStruct((b, t, d), BF16), ext, ext],
        scratch_shapes=[pltpu.VMEM((SUBLANES, LANES), F32)],
        compiler_params=_cparams("arbitrary", "arbitrary"),
        name="norm_gate",
    )(x3, g, w_z, b_f, _gate_routing(n_heads))


ATT_TILE = 512
ATT_HEADS = 4
VT_ROWS = HEAD_DIM + 16


def _flash_init(s, v):
    m = jnp.max(s, axis=1, keepdims=True)
    p = jnp.exp(s - m)
    l = jnp.sum(p, axis=1, keepdims=True)
    acc = jnp.dot(p.astype(BF16), v, preferred_element_type=F32)
    return m, l, acc


def _moba_bias(q, kmean, iq):
    tq = q.shape[0]
    nb = kmean.shape[0]
    km_hi = kmean.astype(BF16)
    km_lo = (kmean - km_hi.astype(F32)).astype(BF16)
    gate = (lax.dot_general(km_hi, q, _NT, preferred_element_type=F32)
            + lax.dot_general(km_lo, q, _NT, preferred_element_type=F32))
    blk_i = lax.broadcasted_iota(jnp.int32, gate.shape, 0)
    qry_i = lax.broadcasted_iota(jnp.int32, gate.shape, 1)
    own = iq * (tq // MOBA_BLOCK) + qry_i // MOBA_BLOCK
    blk_f = blk_i.astype(F32)
    past = blk_i < own
    g = jnp.where(past, gate, -jnp.inf)
    sel = jnp.zeros(gate.shape, jnp.bool_)
    for _ in range(MOBA_TOPK):
        mx = jnp.max(g, axis=0, keepdims=True)
        idx = jnp.min(jnp.where(g == mx, blk_f, float(LANES)), axis=0, keepdims=True)
        pick = blk_f == idx
        sel = sel | pick
        g = jnp.where(pick, -jnp.inf, g)
    bias_t = jnp.where((sel & past) | (blk_i == own), 0.0, NEG)
    bias_t = jnp.concatenate([bias_t, jnp.zeros((LANES - nb, tq), F32)], axis=0)
    return bias_t.T.astype(BF16)


def _attn_kernel(*refs, moba):
    if moba:
        q_ref, k_ref, v_ref, o_ref = refs[:4]
        kmean_sc = refs[-1]
        refs = refs[:-1]
    else:
        q_ref, k_ref, v_ref, qx_ref, kx_ref, o_ref = refs[:6]
    kaug_sc, vt_sc, sa_sc, sb_sc, mxa_sc, mxb_sc, m_sc, acc_sc = refs[-8:]
    iq = pl.program_id(2)
    tq = q_ref.shape[0]
    t = k_ref.shape[0]
    n_heads = q_ref.shape[1] // HEAD_DIM
    heads = [slice(g * HEAD_DIM, (g + 1) * HEAD_DIM) for g in range(n_heads)]

    @pl.when(iq == 0)
    def _():
        for g, hs in enumerate(heads):
            kaug_sc[g, :, :HEAD_DIM] = k_ref[:, hs]
            if moba:
                row = lax.broadcasted_iota(jnp.int32, (t, LANES), 0)
                lane = lax.broadcasted_iota(jnp.int32, (t, LANES), 1)
                kaug_sc[g, :, HEAD_DIM:] = ((row // MOBA_BLOCK) == lane).astype(BF16)
                kmean_sc[g] = jnp.zeros((LANES, HEAD_DIM), F32)
                for n in range(t // MOBA_BLOCK):
                    kb = k_ref[n * MOBA_BLOCK:(n + 1) * MOBA_BLOCK, hs].astype(F32)
                    kmean_sc[g, n:n + 1, :] = jnp.mean(kb, axis=0, keepdims=True)
            else:
                kaug_sc[g, :, HEAD_DIM:] = kx_ref[g]
            pad_row = lax.broadcasted_iota(jnp.int32, (VT_ROWS - HEAD_DIM, tq), 0)
            for j in range(t // tq):
                vt_sc[g, j, :HEAD_DIM, :] = (
                    v_ref[j * tq:(j + 1) * tq, hs].astype(F32).T.astype(BF16))
                vt_sc[g, j, HEAD_DIM:, :] = (pad_row == 0).astype(BF16)

    qa = []
    for g, hs in enumerate(heads):
        q = q_ref[:, hs]
        if moba:
            nb = -(-(t // MOBA_BLOCK) // 16) * 16
            extra = _moba_bias(q, kmean_sc[g, :nb, :], iq)
        else:
            extra = qx_ref[g]
        qa.append(jnp.concatenate([q, extra], axis=1))

    def scores(g, j):
        off = pl.multiple_of(j * tq, tq)
        return lax.dot_general(kaug_sc[g, pl.ds(off, tq), :], qa[g], _NT,
                               preferred_element_type=F32)

    half = tq // 2
    halves = (slice(0, half), slice(half, tq))

    def produce(j, s_ref, mx_ref):
        for g in range(n_heads):
            s = scores(g, j)
            s_ref[g] = s
            mx_ref[g] = jnp.max(s, axis=0, keepdims=True)

    def consume(j, s_ref, mx_ref, diagonal=False):
        for g in range(n_heads):
            m = m_sc[g]
            m_new = jnp.maximum(m, mx_ref[g])
            alpha = jnp.exp2(m - m_new)
            m_sc[g] = m_new
            for qi, qs in enumerate(halves):
                pv = alpha[:, qs] * acc_sc[g, :, qs]
                for ki, ks in enumerate(halves):
                    if diagonal and ki > qi:
                        continue
                    p = jnp.exp2(s_ref[g, ks, qs] - m_new[:, qs])
                    pv = pv + jnp.dot(vt_sc[g, j, :, ks], p.astype(BF16),
                                      preferred_element_type=F32)
                acc_sc[g, :, qs] = pv

    for g in range(n_heads):
        m_sc[g] = jnp.full((1, tq), NEG, F32)
        acc_sc[g] = jnp.zeros((VT_ROWS, tq), F32)
    produce(0, sa_sc, mxa_sc)

    def body(jj, carry):
        j = 2 * jj
        produce(j + 1, sb_sc, mxb_sc)
        consume(j, sa_sc, mxa_sc)
        produce(j + 2, sa_sc, mxa_sc)
        consume(j + 1, sb_sc, mxb_sc)
        return carry

    lax.fori_loop(0, iq // 2, body, 0)

    def finish(s_ref, mx_ref):
        key = lax.broadcasted_iota(jnp.int32, (half, half), 0)
        qry = lax.broadcasted_iota(jnp.int32, (half, half), 1)
        lo, hi = halves
        for g in range(n_heads):
            d_lo = jnp.where(key <= qry, s_ref[g, lo, lo], NEG)
            d_hi = jnp.where(key <= qry, s_ref[g, hi, hi], NEG)
            s_ref[g, lo, lo] = d_lo
            s_ref[g, hi, hi] = d_hi
            mx_ref[g, :, lo] = jnp.max(d_lo, axis=0, keepdims=True)
            mx_ref[g, :, hi] = jnp.maximum(jnp.max(s_ref[g, lo, hi], axis=0, keepdims=True),
                                           jnp.max(d_hi, axis=0, keepdims=True))
        consume(iq, s_ref, mx_ref, diagonal=True)
        for g, hs in enumerate(heads):
            out = acc_sc[g, :HEAD_DIM, :] / acc_sc[g, HEAD_DIM:HEAD_DIM + 1, :]
            o_ref[:, hs] = out.T.astype(o_ref.dtype)

    @pl.when(iq % 2 == 0)
    def _():
        finish(sa_sc, mxa_sc)

    @pl.when(iq % 2 == 1)
    def _():
        produce(iq, sb_sc, mxb_sc)
        consume(iq - 1, sa_sc, mxa_sc)
        finish(sb_sc, mxb_sc)


def _attn(proj3, n_heads, head0, fox_extras=None):
    b, t, _ = proj3.shape
    moba = fox_extras is None
    tq = min(ATT_TILE, t)
    g = min(ATT_HEADS, n_heads)
    assert t % tq == 0 and tq % MOBA_BLOCK == 0 and t // MOBA_BLOCK <= LANES and n_heads % g == 0
    w = g * HEAD_DIM
    c0 = head0 // g
    in_specs = [
        pl.BlockSpec((None, tq, w), lambda bi, h, i: (bi, i, c0 + h)),
        pl.BlockSpec((None, t, w), lambda bi, h, i: (bi, 0, c0 + n_heads // g + h)),
        pl.BlockSpec((None, t, w), lambda bi, h, i: (bi, 0, c0 + 2 * (n_heads // g) + h)),
    ]
    args = [proj3, proj3, proj3]
    scratch = [
        pltpu.VMEM((g, t, 2 * HEAD_DIM), BF16),
        pltpu.VMEM((g, t // tq, VT_ROWS, tq), BF16),
        pltpu.VMEM((g, tq, tq), F32),
        pltpu.VMEM((g, tq, tq), F32),
        pltpu.VMEM((g, 1, tq), F32),
        pltpu.VMEM((g, 1, tq), F32),
        pltpu.VMEM((g, 1, tq), F32),
        pltpu.VMEM((g, VT_ROWS, tq), F32),
    ]
    if moba:
        scratch.append(pltpu.VMEM((g, LANES, HEAD_DIM), F32))
    else:
        in_specs += [
            pl.BlockSpec((None, g, tq, LANES), lambda bi, h, i: (bi, h, i, 0)),
            pl.BlockSpec((None, g, t, LANES), lambda bi, h, i: (bi, h, 0, 0)),
        ]
        args += list(fox_extras)
    return pl.pallas_call(
        functools.partial(_attn_kernel, moba=moba),
        grid=(b, n_heads // g, t // tq),
        in_specs=in_specs,
        out_specs=pl.BlockSpec((None, tq, w), lambda bi, h, i: (bi, i, h)),
        out_shape=jax.ShapeDtypeStruct((b, t, n_heads * HEAD_DIM), BF16),
        scratch_shapes=scratch,
        compiler_params=_cparams("arbitrary", "arbitrary", "arbitrary"),
        name="moba_attn" if moba else "fox_attn",
    )(*args)


def _mem_kv_kernel(mem_ref, g_ref, w_ref, k_ref, v_ref):
    xn = _rmsnorm(mem_ref[...], g_ref[...]).astype(BF16)
    kv = jnp.dot(xn, w_ref[...].astype(BF16), preferred_element_type=F32)
    half = kv.shape[1] // 2
    k_ref[...] = kv[:, :half].astype(k_ref.dtype)
    v_ref[...] = kv[:, half:].astype(v_ref.dtype)


def _mem_kv(mem, g, w_ckv):
    b, nm, d = mem.shape
    xd = w_ckv.shape[1] // 2
    out = jax.ShapeDtypeStruct((b, nm, xd), BF16)
    o_spec = pl.BlockSpec((None, nm, xd), lambda bi: (bi, 0, 0))
    return pl.pallas_call(
        _mem_kv_kernel,
        grid=(b,),
        in_specs=[
            pl.BlockSpec((None, nm, d), lambda bi: (bi, 0, 0)),
            pl.BlockSpec((1, d), lambda bi: (0, 0)),
            pl.BlockSpec((d, 2 * xd), lambda bi: (0, 0)),
        ],
        out_specs=[o_spec, o_spec],
        out_shape=[out, out],
        compiler_params=_cparams("arbitrary"),
        name="mem_kv",
    )(mem, g, w_ckv)


def _mid_kernel(x_ref, om_ref, of_ref, wom_ref, wof_ref, gx_ref, wcq_ref, kc_ref, vc_ref,
                wco_ref, gf_ref, h_ref, xn_ref):
    h1 = (x_ref[...]
          + jnp.dot(om_ref[...], wom_ref[...].astype(BF16), preferred_element_type=F32)
          + jnp.dot(of_ref[...], wof_ref[...].astype(BF16), preferred_element_type=F32))
    xn2 = _rmsnorm(h1, gx_ref[...]).astype(BF16)
    q = (jnp.dot(xn2, wcq_ref[...].astype(BF16), preferred_element_type=F32) * SCALE).astype(BF16)
    outs = []
    for h in range(N_XATTN_HEADS):
        sl = slice(h * HEAD_DIM, (h + 1) * HEAD_DIM)
        s = lax.dot_general(q[:, sl], kc_ref[:, sl], _NT, preferred_element_type=F32)
        m, l, acc = _flash_init(s, vc_ref[:, sl])
        outs.append((acc / l).astype(BF16))
    oc = jnp.concatenate(outs, axis=1)
    h2 = h1 + jnp.dot(oc, wco_ref[...].astype(BF16), preferred_element_type=F32)
    h_ref[...] = h2
    xn_ref[...] = _rmsnorm(h2, gf_ref[...]).astype(xn_ref.dtype)


def _mid(x2, o_moba, o_fox, w_o, gx, w_cq, kc, vc, w_co, gf, *, seq):
    n, d = x2.shape
    wm, wf = o_moba.shape[1], o_fox.shape[1]
    assert wm == wf and w_o.shape[0] == wm + wf
    nm, xd = kc.shape[1], kc.shape[2]
    tm = min(512, seq)
    t_tiles = seq // tm
    const = lambda i: (0, 0)
    return pl.pallas_call(
        _mid_kernel,
        grid=(n // tm,),
        in_specs=[
            pl.BlockSpec((tm, d), lambda i: (i, 0)),
            pl.BlockSpec((tm, wm), lambda i: (i, 0)),
            pl.BlockSpec((tm, wf), lambda i: (i, 0)),
            pl.BlockSpec((wm, d), const, pipeline_mode=pl.Buffered(1)),
            pl.BlockSpec((wf, d), lambda i: (1, 0), pipeline_mode=pl.Buffered(1)),
            pl.BlockSpec((1, d), const),
            pl.BlockSpec((d, xd), const, pipeline_mode=pl.Buffered(1)),
            pl.BlockSpec((None, nm, xd), lambda i: (i // t_tiles, 0, 0)),
            pl.BlockSpec((None, nm, xd), lambda i: (i // t_tiles, 0, 0)),
            pl.BlockSpec((xd, d), const, pipeline_mode=pl.Buffered(1)),
            pl.BlockSpec((1, d), const),
        ],
        out_specs=[
            pl.BlockSpec((tm, d), lambda i: (i, 0)),
            pl.BlockSpec((tm, d), lambda i: (i, 0)),
        ],
        out_shape=[
            jax.ShapeDtypeStruct((n, d), F32),
            jax.ShapeDtypeStruct((n, d), BF16),
        ],
        compiler_params=_cparams("arbitrary"),
        name="mid",
    )(x2, o_moba, o_fox, w_o, w_o, gx, w_cq, kc, vc, w_co, gf)


def _conv_ffn_kernel(xn_ref, h_ref, wg_ref, wu_ref, cwg_ref, cwu_ref, cbg_ref, cbu_ref,
                     wd_ref, fg_ref, o_ref, hs_sc, tail_sc, *, tiles_per_seq, n_res):
    i = pl.program_id(0)
    f = pl.program_id(1)
    nf = pl.num_programs(1)
    tm = xn_ref.shape[0]
    tf = wg_ref.shape[1]
    halo = SUBLANES
    first = (i % tiles_per_seq) == 0

    @pl.when((i == 0) & (f == 0))
    def _():
        tail_sc[...] = jnp.zeros_like(tail_sc)

    def up():
        xn = xn_ref[...]
        hs_sc[0:halo, :] = jnp.where(first, 0.0, tail_sc[f])
        hs_sc[halo:, :tf] = jnp.dot(xn, wg_ref[...].astype(BF16), preferred_element_type=F32)
        hs_sc[halo:, tf:] = jnp.dot(xn, wu_ref[...].astype(BF16), preferred_element_type=F32)
        tail_sc[f] = hs_sc[tm:tm + halo, :]

    def gate(cols):
        def conv(base, cw_ref, cb_ref):
            y = cb_ref[:, cols]
            for k in range(CONV_WIDTH):
                lo = halo - (CONV_WIDTH - 1) + k
                y = y + hs_sc[lo:lo + tm, base + cols.start:base + cols.stop] * cw_ref[k:k + 1, cols]
            return y
        gte = conv(0, cwg_ref, cbg_ref)
        return (gte * (1.0 / (1.0 + jnp.exp2(gte * (-LOG2E)))) * conv(tf, cwu_ref, cbu_ref)
                ).astype(BF16)

    res_rows = h_ref.shape[0]

    @pl.when(f == 0)
    def _():
        o_ref[...] = jnp.zeros_like(o_ref)

    @pl.when(f < n_res)
    def _():
        r0 = pl.multiple_of(f * res_rows, res_rows)
        o_ref[pl.ds(r0, res_rows), :] += h_ref[...]

    up()
    hk = tf // 2
    o_ref[...] += (
        jnp.dot(gate(slice(0, hk)), wd_ref[:hk, :].astype(BF16), preferred_element_type=F32)
        + jnp.dot(gate(slice(hk, tf)), wd_ref[hk:, :].astype(BF16), preferred_element_type=F32))

    @pl.when(f == nf - 1)
    def _():
        o_ref[...] = _rmsnorm(o_ref[...], fg_ref[...])


def _conv_ffn(xn3, h2, w_up, conv_w, conv_b, w_down, fg, *, seq):
    n, d = h2.shape
    ff = w_down.shape[0]
    tm = min(1024, seq)
    tf = 512
    assert ff % tf == 0
    nf = ff // tf
    n_res = 8 if nf >= 8 else 4
    assert nf >= n_res and tm % (n_res * SUBLANES) == 0
    return pl.pallas_call(
        functools.partial(_conv_ffn_kernel, tiles_per_seq=seq // tm, n_res=n_res),
        grid=(n // tm, nf),
        in_specs=[
            pl.BlockSpec((tm, d), lambda i, f: (i, 0)),
            pl.BlockSpec((tm // n_res, d), lambda i, f: (i * n_res + jnp.minimum(f, n_res - 1), 0)),
            pl.BlockSpec((d, tf), lambda i, f: (0, f)),
            pl.BlockSpec((d, tf), lambda i, f: (0, f + nf)),
            pl.BlockSpec((CONV_WIDTH, tf), lambda i, f: (0, f)),
            pl.BlockSpec((CONV_WIDTH, tf), lambda i, f: (0, f + nf)),
            pl.BlockSpec((1, tf), lambda i, f: (0, f)),
            pl.BlockSpec((1, tf), lambda i, f: (0, f + nf)),
            pl.BlockSpec((tf, d), lambda i, f: (f, 0)),
            pl.BlockSpec((1, d), lambda i, f: (0, 0)),
        ],
        out_specs=pl.BlockSpec((tm, d), lambda i, f: (i, 0)),
        out_shape=jax.ShapeDtypeStruct((n, d), F32),
        scratch_shapes=[
            pltpu.VMEM((tm + SUBLANES, 2 * tf), F32),
            pltpu.VMEM((nf, SUBLANES, 2 * tf), F32),
        ],
        compiler_params=_cparams("arbitrary", "arbitrary", vmem_limit_bytes=FFN_VMEM_LIMIT_BYTES),
        name="conv_ffn",
    )(xn3, h2, w_up, w_up, conv_w, conv_w, conv_b, conv_b, w_down, fg)


def _rope_tables(t):
    half = HEAD_DIM // 2
    inv = (ROPE_THETA ** (-np.arange(half, dtype=np.float32) / half)).astype(np.float32)
    ang = (np.arange(t, dtype=np.float32)[:, None] * inv[None, :]).astype(np.float32)
    cos, sin = np.cos(ang), np.sin(ang)
    return (jnp.asarray(np.concatenate([cos, cos], axis=1)),
            jnp.asarray(np.concatenate([-sin, sin], axis=1)))


def kernel(x, mem, attn_norm_g, w_in, b_f, w_o, xattn_norm_g, mem_norm_g, w_cq, w_ckv, w_co,
           ffn_norm_g, w_up, conv_w, conv_b, w_down, final_norm_g):
    b, t, d = x.shape
    assert w_in.shape[0] == 1, "one layer: the final rmsnorm is fused into the ffn kernel"
    n_heads = d // HEAD_DIM
    n_moba = n_heads // 2
    n_fox = n_heads - n_moba
    assert n_moba == n_fox and n_fox <= LANES
    wgrp = n_moba * HEAD_DIM
    main_cols = 6 * wgrp
    cos, sin = _rope_tables(t)
    row = lambda v: v.reshape(1, -1)

    w_in_t = w_in[0].T
    w_z_t = jnp.pad(w_in_t[main_cols:], ((0, LANES - n_fox), (0, 0))).astype(BF16)
    bf = row(jnp.pad(b_f[0].astype(F32), (0, LANES - n_fox)))
    xn, qx, kx = _norm_gate(x, row(attn_norm_g[0]), w_z_t, bf, n_fox)
    proj = _in_proj(xn.reshape(b * t, d), w_in_t, cos, sin, seq=t, group_width=wgrp)
    proj3 = proj.reshape(b, t, main_cols)
    o_moba = _attn(proj3, n_moba, 0)
    o_fox = _attn(proj3, n_fox, 3 * n_moba, (qx, kx))
    kc, vc = _mem_kv(mem, row(mem_norm_g[0]), w_ckv[0])
    h2, xn3 = _mid(x.reshape(b * t, d), o_moba.reshape(b * t, wgrp), o_fox.reshape(b * t, wgrp),
                   w_o[0], row(xattn_norm_g[0]), w_cq[0], kc, vc,
                   w_co[0], row(ffn_norm_g[0]), seq=t)
    out = _conv_ffn(xn3, h2, w_up[0], conv_w[0], row(conv_b[0]),
                    w_down[0], row(final_norm_g), seq=t)
    return out.reshape(b, t, d)
```
